```python
import functools
import jax, jax.numpy as jnp
from jax import lax
import numpy as np

D_MODEL = 1024
BATCH = 16
SEQ = 256
DEPTH = 4
DEC_BATCH = 8
DEC_SEQ = 1024
PAST_LEN = 512

GRID_W = 64
CONV_CH = 512
CONV_TAPS = 3
NA_HEADS = 8
HEAD_DIM = 64
NA_WIDTH = NA_HEADS * HEAD_DIM
WIN_ROWS = 8
WIN_COLS = 16
COL_BAND = 2 * WIN_COLS
MIX_WIDTH = CONV_CH + NA_WIDTH
SPLITS = (CONV_CH, 2 * CONV_CH, 3 * CONV_CH, 3 * CONV_CH + NA_WIDTH, 3 * CONV_CH + 2 * NA_WIDTH)
IN_WIDTH = 3 * CONV_CH + 3 * NA_WIDTH
FOURIER_GROUPS = 4
FFN_DENSE = 2816
N_EXPERTS = 8
TOP_K = 2
FFN_EXPERT = 3584
N_EVEN = (DEPTH + 1) // 2
N_ODD = DEPTH // 2
RMS_EPS = 1e-6
NEG_INF = -1e30

kernel_name = 'hybrid_diffusion_prefix_trunk_step'


def rmsnorm(x, g):
    xf = x.astype(jnp.float32)
    y = xf * lax.rsqrt(jnp.mean(xf * xf, axis=-1, keepdims=True) + RMS_EPS)
    return (y * g.astype(jnp.float32)).astype(x.dtype)


def adaln(cvec, w, b):
    m = jax.nn.silu(cvec) @ w + b
    return jnp.split(m[:, None, :], 6, axis=-1)


def short_conv(u, w):
    pad = CONV_TAPS // 2
    n = u.shape[1]
    up = jnp.pad(u, ((0, 0), (pad, pad), (0, 0)))
    y = up[:, 0:n, :] * w[0]
    for j in range(1, CONV_TAPS):
        y = y + up[:, j:j + n, :] * w[j]
    return y


def heads(t):
    b, n, _ = t.shape
    return t.reshape(b, n, NA_HEADS, HEAD_DIM).transpose(0, 2, 1, 3)


def context_attention(q, k, v):
    b, n, _ = q.shape
    qh, kh, vh = heads(q), heads(k), heads(v)
    s = jnp.einsum('bhqd,bhkd->bhqk', qh, kh).astype(jnp.float32) * (HEAD_DIM ** -0.5)
    p = jax.nn.softmax(s, axis=-1).astype(vh.dtype)
    o = jnp.einsum('bhqk,bhkd->bhqd', p, vh)
    return o.transpose(0, 2, 1, 3).reshape(b, n, NA_WIDTH), (kh, vh)


def neighbourhood_attention(q, k, v, rpb, k_ctx, v_ctx):
    b, n, _ = q.shape
    rows = n // GRID_W
    wr = min(WIN_ROWS, rows)
    ncb = GRID_W // WIN_COLS

    def grid(t):
        return t.reshape(b, rows, GRID_W, NA_HEADS, HEAD_DIM).transpose(0, 3, 1, 2, 4)

    qg, kg, vg = grid(q), grid(k), grid(v)
    r = jnp.arange(rows)
    row_idx = jnp.clip(r - wr // 2, 0, rows - wr)[:, None] + jnp.arange(wr)[None, :]
    qcol = jnp.arange(GRID_W).reshape(ncb, WIN_COLS)
    col_start = jnp.clip(qcol - WIN_COLS // 2, 0, GRID_W - WIN_COLS)
    band_idx = (jnp.clip(qcol[:, 0] - WIN_COLS // 2, 0, GRID_W - COL_BAND)[:, None]
                + jnp.arange(COL_BAND)[None, :])
    gi_r = row_idx[:, None, :, None]
    gi_c = band_idx[None, :, None, :]
    k_nb = kg[:, :, gi_r, gi_c]
    v_nb = vg[:, :, gi_r, gi_c]
    q_blk = qg.reshape(b, NA_HEADS, rows, ncb, WIN_COLS, HEAD_DIM)
    scale = HEAD_DIM ** -0.5
    s_loc = jnp.einsum('bhrcqd,bhrcwkd->bhrcqwk', q_blk, k_nb).astype(jnp.float32) * scale
    kcol = band_idx[:, None, :]
    cs = col_start[:, :, None]
    valid = (kcol >= cs) & (kcol < cs + WIN_COLS)
    roff = row_idx - r[:, None] + (WIN_ROWS - 1)
    coff = jnp.clip(kcol - qcol[:, :, None], -(WIN_COLS - 1), WIN_COLS - 1) + (WIN_COLS - 1)
    bias = rpb[:, roff[:, None, None, :, None], coff[None, :, :, None, :]]
    s_loc = jnp.where(valid[:, :, None, :], s_loc + bias[None].astype(jnp.float32), NEG_INF)
    n_loc = wr * COL_BAND
    s_loc = s_loc.reshape(b, NA_HEADS, rows, ncb, WIN_COLS, n_loc)
    s_ctx = jnp.einsum('bhrcqd,bhld->bhrcql', q_blk, k_ctx).astype(jnp.float32) * scale
    p = jax.nn.softmax(jnp.concatenate([s_loc, s_ctx], axis=-1), axis=-1).astype(v.dtype)
    v_loc = v_nb.reshape(b, NA_HEADS, rows, ncb, n_loc, HEAD_DIM)
    o = (jnp.einsum('bhrcqk,bhrckd->bhrcqd', p[..., :n_loc], v_loc)
         + jnp.einsum('bhrcql,bhld->bhrcqd', p[..., n_loc:], v_ctx))
    return o.transpose(0, 2, 3, 4, 1, 5).reshape(b, n, NA_WIDTH)


def mixer_ab(h, w_in, conv_w, w_out, attend):
    a_b, a_c, a_x, q, k, v = jnp.split(h @ w_in, SPLITS, axis=-1)
    y_a = a_b * short_conv(a_c * a_x, conv_w)
    y_b, kv = attend(q, k, v)
    return jnp.concatenate([y_a, y_b], axis=-1) @ w_out, kv


def fourier_mix(h, w_f):
    b, n, d = h.shape
    hg = h.astype(jnp.float32).reshape(b, n, FOURIER_GROUPS, d // FOURIER_GROUPS)
    f = jnp.fft.fft2(hg, axes=(1, 3), norm='ortho').real
    return f.reshape(b, n, d).astype(h.dtype) @ w_f


def swiglu(h, w1, w3, w2):
    return (jax.nn.silu(h @ w1) * (h @ w3)) @ w2


def moe(h, router_w, router_b, w1, w3, w2):
    b, n, d = h.shape
    t = h.reshape(b * n, d)
    logits = (t @ router_w).astype(jnp.float32) + router_b.astype(jnp.float32)
    top_v, top_i = lax.top_k(logits, TOP_K)
    wts = jax.nn.softmax(top_v, axis=-1)
    gates = jnp.sum(jax.nn.one_hot(top_i, N_EXPERTS, dtype=jnp.float32) * wts[..., None], axis=1).astype(t.dtype)
    out = gates[:, 0:1] * swiglu(t, w1[0], w3[0], w2[0])
    for e in range(1, N_EXPERTS):
        out = out + gates[:, e:e + 1] * swiglu(t, w1[e], w3[e], w2[e])
    return out.reshape(b, n, d)


def trunk(x, cvec, attend, ada_w, ada_b, norm1_g, norm2_g, w_in, conv_w, w_out, ffn_w1, ffn_w3, ffn_w2,
          fourier_w, router_w, router_b, moe_w1, moe_w3, moe_w2, final_g):
    kvs = []
    for l in range(DEPTH):
        i = l // 2
        sh1, sc1, g1, sh2, sc2, g2 = adaln(cvec, ada_w[l], ada_b[l])
        h = rmsnorm(x, norm1_g[l]) * (1 + sc1) + sh1
        if l % 2 == 0:
            mix, kv = mixer_ab(h, w_in[i], conv_w[i], w_out[i], functools.partial(attend, i))
            kvs.append(kv)
        else:
            mix = fourier_mix(h, fourier_w[i])
        x = x + g1 * mix
        h = rmsnorm(x, norm2_g[l]) * (1 + sc2) + sh2
        if l % 2 == 0:
            ff = swiglu(h, ffn_w1[i], ffn_w3[i], ffn_w2[i])
        else:
            ff = moe(h, router_w[i], router_b[i], moe_w1[i], moe_w3[i], moe_w2[i])
        x = x + g2 * ff
    return rmsnorm(x, final_g), kvs


def setup_inputs(seed: int = 0) -> dict:
    key = jax.random.key(seed)
    ks = jax.random.split(key, 24)
    nrm = jax.random.normal
    f32 = jnp.float32
    D = D_MODEL
    return {
        'x_prompt': nrm(ks[0], (BATCH, SEQ, D), f32),
        'x_sample': nrm(ks[1], (DEC_BATCH, DEC_SEQ, D), f32),
        'c': nrm(ks[2], (DEC_BATCH, D), f32),
        'cache_k': nrm(ks[3], (DEC_BATCH, N_EVEN, NA_HEADS, PAST_LEN, HEAD_DIM), f32),
        'cache_v': nrm(ks[4], (DEC_BATCH, N_EVEN, NA_HEADS, PAST_LEN, HEAD_DIM), f32),
        'c_ctx': nrm(ks[5], (D,), f32),
        'ada_w': nrm(ks[6], (DEPTH, D, 6 * D), f32) * (0.5 * D ** -0.5),
        'ada_b': nrm(ks[7], (DEPTH, 6 * D), f32) * 0.02,
        'norm1_g': 1.0 + 0.02 * nrm(ks[8], (DEPTH, D), f32),
        'norm2_g': 1.0 + 0.02 * nrm(ks[9], (DEPTH, D), f32),
        'w_in': nrm(ks[10], (N_EVEN, D, IN_WIDTH), f32) * D ** -0.5,
        'conv_w': nrm(ks[11], (N_EVEN, CONV_TAPS, CONV_CH), f32) * CONV_TAPS ** -0.5,
        'rpb': nrm(ks[12], (N_EVEN, NA_HEADS, 2 * WIN_ROWS - 1, 2 * WIN_COLS - 1), f32) * 0.1,
        'w_out': nrm(ks[13], (N_EVEN, MIX_WIDTH, D), f32) * MIX_WIDTH ** -0.5,
        'ffn_w1': nrm(ks[14], (N_EVEN, D, FFN_DENSE), f32) * D ** -0.5,
        'ffn_w3': nrm(ks[15], (N_EVEN, D, FFN_DENSE), f32) * D ** -0.5,
        'ffn_w2': nrm(ks[16], (N_EVEN, FFN_DENSE, D), f32) * FFN_DENSE ** -0.5,
        'fourier_w': nrm(ks[17], (N_ODD, D, D), f32) * D ** -0.5,
        'router_w': nrm(ks[18], (N_ODD, D, N_EXPERTS), f32) * D ** -0.5,
        'router_b': nrm(ks[19], (N_ODD, N_EXPERTS), f32) * 0.01,
        'moe_w1': nrm(ks[20], (N_ODD, N_EXPERTS, D, FFN_EXPERT), f32) * D ** -0.5,
        'moe_w3': nrm(ks[21], (N_ODD, N_EXPERTS, D, FFN_EXPERT), f32) * D ** -0.5,
        'moe_w2': nrm(ks[22], (N_ODD, N_EXPERTS, FFN_EXPERT, D), f32) * FFN_EXPERT ** -0.5,
        'final_g': 1.0 + 0.02 * nrm(ks[23], (D,), f32),
    }


def reference(x_prompt, x_sample, c, cache_k, cache_v, c_ctx, ada_w, ada_b, norm1_g, norm2_g, w_in, conv_w,
              rpb, w_out, ffn_w1, ffn_w3, ffn_w2, fourier_w, router_w, router_b, moe_w1, moe_w3, moe_w2,
              final_g):
    shared = (ada_w, ada_b, norm1_g, norm2_g, w_in, conv_w, w_out, ffn_w1, ffn_w3, ffn_w2,
              fourier_w, router_w, router_b, moe_w1, moe_w3, moe_w2, final_g)

    def ctx_attend(i, q, k, v):
        return context_attention(q, k, v)

    y_prompt, kvs = trunk(x_prompt, c_ctx[None, :], ctx_attend, *shared)
    new_cache_k = jnp.stack([kv[0] for kv in kvs], axis=1)
    new_cache_v = jnp.stack([kv[1] for kv in kvs], axis=1)

    def latent_attend(i, q, k, v):
        return neighbourhood_attention(q, k, v, rpb[i], cache_k[:, i], cache_v[:, i]), None

    y_sample, _ = trunk(x_sample, c, latent_attend, *shared)
    return (y_prompt, y_sample, new_cache_k, new_cache_v)
```

```python
import functools

import numpy as np
import jax
import jax.numpy as jnp
from jax import lax
from jax.experimental import pallas as pl
from jax.experimental.pallas import tpu as pltpu

f32 = jnp.float32
bf16 = jnp.bfloat16

D = 1024
BATCH = 16
SEQ = 256
DEPTH = 4
DEC_BATCH = 8
DEC_SEQ = 1024
PAST_LEN = 512
GRID_W = 64
CONV_CH = 512
NA_HEADS = 8
HEAD_DIM = 64
NA_WIDTH = 512
WIN_ROWS = 8
WIN_COLS = 16
IN_WIDTH = 3072
FOURIER_GROUPS = 4
FFN_DENSE = 2816
N_EXPERTS = 8
FFN_EXPERT = 3584
N_EVEN = 2
RMS_EPS = 1e-6
NEG_INF = -1e30
SCALE = HEAD_DIM ** -0.5

LANES = 128
SUBLANES = 8
TM = 1024
T_P = BATCH * SEQ
T_S = DEC_BATCH * DEC_SEQ
T = T_P + T_S
NT = T // TM
NTP = T_P // TM
MOD_ROWS = 16
GRID_ROWS = DEC_SEQ // GRID_W
CHUNK_ROWS = 4
N_CHUNKS = GRID_ROWS // CHUNK_ROWS
CHUNK_Q = CHUNK_ROWS * GRID_W
KWIN_ROWS = 12
KWIN = KWIN_ROWS * GRID_W
KB0 = (0, 0, 4, 4)
TF_DENSE = 256
TF_EXP = 512
TM_E = 1024
N_ASSIGN = 2 * T
NT_E = N_ASSIGN // TM_E + N_EXPERTS
R_E = NT_E * TM_E
VMEM_LIMIT = 56 * 1024 * 1024


def _cp(sem, vmem=VMEM_LIMIT):
    return pltpu.CompilerParams(dimension_semantics=sem, vmem_limit_bytes=vmem)


def _mod_index(t):
    return jnp.where(t < NTP, 0, t - (NTP - 1))


def _mod_spec(layer, part):
    def index(t, *_):
        return ((layer * MOD_ROWS + _mod_index(t)) * 6 + part, 0, 0)
    return pl.BlockSpec((1, 1, D), index)


def _norm_mod(x, g, sc, sh):
    y = x * lax.rsqrt(jnp.mean(x * x, axis=-1, keepdims=True) + RMS_EPS)
    return (y * g) * (1 + sc) + sh


def _dot(a, b):
    return jnp.dot(a, b, preferred_element_type=f32)


def _dot_nt(a, b):
    return lax.dot_general(a, b, (((1,), (1,)), ((), ())), preferred_element_type=f32)


def _mod_kernel(cv_ref, w_ref, b_ref, o_ref):
    s = jax.nn.silu(cv_ref[...]).astype(bf16)
    o_ref[0] = _dot(s, w_ref[0].astype(bf16)) + b_ref[0]


def _modulation(cvecs, ada_w, ada_b):
    tn = 1536
    return pl.pallas_call(
        _mod_kernel,
        grid=(DEPTH, 6 * D // tn),
        in_specs=[pl.BlockSpec((MOD_ROWS, D), lambda l, j: (0, 0)),
                  pl.BlockSpec((1, D, tn), lambda l, j: (l, 0, j)),
                  pl.BlockSpec((1, 1, tn), lambda l, j: (l, 0, j))],
        out_specs=pl.BlockSpec((1, MOD_ROWS, tn), lambda l, j: (l, 0, j)),
        out_shape=jax.ShapeDtypeStruct((DEPTH, MOD_ROWS, 6 * D), f32),
        compiler_params=_cp(("arbitrary", "arbitrary")),
    )(cvecs, ada_w, ada_b.reshape(DEPTH, 1, 6 * D))


def _inproj_kernel(x_ref, g_ref, sc_ref, sh_ref, w_ref, proj_ref, kc_ref, vc_ref, h_ref):
    i = pl.program_id(0)
    j = pl.program_id(1)

    @pl.when(j == 0)
    def _():
        h_ref[...] = _norm_mod(x_ref[...], g_ref[0], sc_ref[0], sh_ref[0]).astype(bf16)

    acc = _dot(h_ref[...], w_ref[...].astype(bf16))
    proj_ref[...] = acc.astype(bf16)

    def write_cache(ref):
        for s in range(TM // SEQ):
            for hd in range(NA_HEADS):
                ref[s, hd] = acc[s * SEQ:(s + 1) * SEQ, hd * HEAD_DIM:(hd + 1) * HEAD_DIM]

    @pl.when((j == 4) & (i < NTP))
    def _():
        write_cache(kc_ref)

    @pl.when((j == 5) & (i < NTP))
    def _():
        write_cache(vc_ref)


def _inproj(x, mod, norm_g, w_in, layer):
    li = layer // 2
    tn = 512
    seqs = TM // SEQ
    cache_spec = pl.BlockSpec((seqs, NA_HEADS, SEQ, HEAD_DIM),
                              lambda i, j: (jnp.minimum(i, NTP - 1), 0, 0, 0))
    cache_shape = jax.ShapeDtypeStruct((BATCH, NA_HEADS, SEQ, HEAD_DIM), f32)
    return pl.pallas_call(
        _inproj_kernel,
        grid=(NT, IN_WIDTH // tn),
        in_specs=[pl.BlockSpec((TM, D), lambda i, j: (i, 0)),
                  pl.BlockSpec((1, 1, D), lambda i, j: (layer, 0, 0)),
                  _mod_spec(layer, 1), _mod_spec(layer, 0),
                  pl.BlockSpec((None, D, tn), lambda i, j: (li, 0, j))],
        out_specs=[pl.BlockSpec((TM, tn), lambda i, j: (i, j)), cache_spec, cache_spec],
        out_shape=[jax.ShapeDtypeStruct((T, IN_WIDTH), bf16), cache_shape, cache_shape],
        scratch_shapes=[pltpu.VMEM((TM, D), bf16)],
        compiler_params=_cp(("arbitrary", "arbitrary")),
    )(x, norm_g, mod, mod, w_in)


def _softmax_pv(s_parts, v_parts):
    m = functools.reduce(jnp.maximum, [jnp.max(s, axis=-1, keepdims=True) for s in s_parts])
    ps = [jnp.exp(s - m) for s in s_parts]
    l = functools.reduce(jnp.add, [jnp.sum(p, axis=-1, keepdims=True) for p in ps])
    o = functools.reduce(jnp.add, [_dot(p.astype(bf16), v) for p, v in zip(ps, v_parts)])
    return o / l


def _ctx_attn_kernel(q_ref, k_ref, v_ref, o_ref):
    for h in range(NA_HEADS):
        sl = slice(h * HEAD_DIM, (h + 1) * HEAD_DIM)
        kh = k_ref[:, sl]
        s = _dot_nt(q_ref[:, sl], kh) * SCALE
        o_ref[:, sl] = _softmax_pv([s], [v_ref[:, sl]]).astype(bf16)


def _ctx_attention(proj):
    col = IN_WIDTH // NA_WIDTH - 3
    return pl.pallas_call(
        _ctx_attn_kernel,
        grid=(BATCH,),
        in_specs=[pl.BlockSpec((SEQ, NA_WIDTH), lambda b: (b, col)),
                  pl.BlockSpec((SEQ, NA_WIDTH), lambda b: (b, col + 1)),
                  pl.BlockSpec((SEQ, NA_WIDTH), lambda b: (b, col + 2))],
        out_specs=pl.BlockSpec((SEQ, NA_WIDTH), lambda b: (b, 0)),
        out_shape=jax.ShapeDtypeStruct((T_P, NA_WIDTH), bf16),
        compiler_params=_cp(("arbitrary",)),
    )(proj, proj, proj)


def _win_start(qr):
    return min(max(qr - WIN_ROWS // 2, 0), GRID_ROWS - WIN_ROWS)


N_ROFF = 2 * WIN_ROWS - 1
N_COFF = 2 * WIN_COLS - 1


def _bias_kernel(rpb_ref, o_ref):
    base = (pl.program_id(0) * NA_HEADS + pl.program_id(1)) * (N_ROFF * N_COFF)
    qc = lax.broadcasted_iota(jnp.int32, (GRID_W, GRID_W), 0)
    kc = lax.broadcasted_iota(jnp.int32, (GRID_W, GRID_W), 1)
    coff = kc - qc + (WIN_COLS - 1)
    cs = jnp.clip(qc - WIN_COLS // 2, 0, GRID_W - WIN_COLS)
    valid = (kc >= cs) & (kc < cs + WIN_COLS)
    neg = jnp.full((GRID_W, GRID_W), NEG_INF, f32)
    blocks = []
    for a in range(N_ROFF):
        t = jnp.zeros((GRID_W, GRID_W), f32)
        for b in range(N_COFF):
            t = jnp.where(coff == b, rpb_ref[base + a * N_COFF + b], t)
        blocks.append(jnp.where(valid, t, neg))
    for c in range(N_CHUNKS):
        for ql in range(CHUNK_ROWS):
            qr = c * CHUNK_ROWS + ql
            st = _win_start(qr)
            for kl in range(KWIN_ROWS):
                kr = KB0[c] + kl
                blk = blocks[kr - qr + WIN_ROWS - 1] if st <= kr < st + WIN_ROWS else neg
                o_ref[c, ql * GRID_W:(ql + 1) * GRID_W, kl * GRID_W:(kl + 1) * GRID_W] = blk


def _nbr_bias(rpb):
    return pl.pallas_call(
        _bias_kernel,
        grid=(N_EVEN, NA_HEADS),
        in_specs=[pl.BlockSpec(memory_space=pltpu.SMEM)],
        out_specs=pl.BlockSpec((None, None, N_CHUNKS, CHUNK_Q, KWIN), lambda i, h: (i, h, 0, 0, 0)),
        out_shape=jax.ShapeDtypeStruct((N_EVEN, NA_HEADS, N_CHUNKS, CHUNK_Q, KWIN), f32),
        compiler_params=_cp(("arbitrary", "arbitrary")),
    )(rpb.reshape(-1))


HEADS_PER_STEP = LANES // HEAD_DIM


def _nbr_attn_kernel(q_ref, k_ref, v_ref, kc_ref, vc_ref, bias_ref, o_ref):
    for hh in range(HEADS_PER_STEP):
        sl = slice(hh * HEAD_DIM, (hh + 1) * HEAD_DIM)
        kctx = kc_ref[hh].astype(bf16)
        vctx = vc_ref[hh].astype(bf16)
        for c in range(N_CHUNKS):
            rows = slice(c * CHUNK_Q, (c + 1) * CHUNK_Q)
            win = slice(KB0[c] * GRID_W, KB0[c] * GRID_W + KWIN)
            q = q_ref[rows, sl]
            s_loc = _dot_nt(q, k_ref[win, sl]) * SCALE + bias_ref[hh, c]
            s_ctx = _dot_nt(q, kctx) * SCALE
            o = _softmax_pv([s_loc, s_ctx], [v_ref[win, sl], vctx])
            o_ref[rows, sl] = o.astype(bf16)


def _nbr_attention(proj, cache_k, cache_v, bias, li):
    qcol = 3 * CONV_CH // LANES
    ncol = NA_WIDTH // LANES
    hp_steps = NA_HEADS // HEADS_PER_STEP

    def col_spec(which):
        return pl.BlockSpec((DEC_SEQ, LANES), lambda hp, b: (NTP + b, qcol + which * ncol + hp))

    ctx_spec = pl.BlockSpec((None, None, HEADS_PER_STEP, PAST_LEN, HEAD_DIM),
                            lambda hp, b: (b, li, hp, 0, 0))
    return pl.pallas_call(
        _nbr_attn_kernel,
        grid=(hp_steps, DEC_BATCH),
        in_specs=[col_spec(0), col_spec(1), col_spec(2), ctx_spec, ctx_spec,
                  pl.BlockSpec((None, HEADS_PER_STEP, N_CHUNKS, CHUNK_Q, KWIN),
                               lambda hp, b: (li, hp, 0, 0, 0))],
        out_specs=pl.BlockSpec((DEC_SEQ, LANES), lambda hp, b: (b, hp)),
        out_shape=jax.ShapeDtypeStruct((T_S, NA_WIDTH), bf16),
        compiler_params=_cp(("arbitrary", "arbitrary")),
    )(proj, proj, proj, cache_k, cache_v, bias)


def _mixout_kernel(a_ref, ybp_ref, ybs_ref, x_ref, g1_ref, cw_ref, w_ref, o_ref):
    t = pl.program_id(0)
    a_b = a_ref[:, 0:CONV_CH].astype(f32)
    a_c = a_ref[:, CONV_CH:2 * CONV_CH].astype(f32)
    a_x = a_ref[:, 2 * CONV_CH:3 * CONV_CH].astype(f32)
    u = a_c * a_x
    r = lax.broadcasted_iota(jnp.int32, (TM, 1), 0)
    pos = jnp.where(t < NTP, r % SEQ, r)
    last = jnp.where(t < NTP, SEQ - 1, DEC_SEQ - 1)
    u_prev = jnp.where(pos == 0, 0.0, pltpu.roll(u, 1, axis=0))
    u_next = jnp.where(pos == last, 0.0, pltpu.roll(u, TM - 1, axis=0))
    y_a = a_b * (u_prev * cw_ref[0:1, :] + u * cw_ref[1:2, :] + u_next * cw_ref[2:3, :])
    y_b = jnp.where(t < NTP, ybp_ref[...], ybs_ref[...])
    y = (_dot(y_a.astype(bf16), w_ref[0:CONV_CH, :].astype(bf16))
         + _dot(y_b, w_ref[CONV_CH:, :].astype(bf16)))
    o_ref[...] = x_ref[...] + g1_ref[0] * y


def _mixout(proj, yb_p, yb_s, x, mod, conv_w, w_out, layer):
    li = layer // 2
    return pl.pallas_call(
        _mixout_kernel,
        grid=(NT,),
        in_specs=[pl.BlockSpec((TM, 3 * CONV_CH), lambda t: (t, 0)),
                  pl.BlockSpec((TM, NA_WIDTH), lambda t: (jnp.minimum(t, NTP - 1), 0)),
                  pl.BlockSpec((TM, NA_WIDTH), lambda t: (jnp.maximum(t - NTP, 0), 0)),
                  pl.BlockSpec((TM, D), lambda t: (t, 0)),
                  _mod_spec(layer, 2),
                  pl.BlockSpec((None, 3, CONV_CH), lambda t: (li, 0, 0)),
                  pl.BlockSpec((None, D, D), lambda t: (li, 0, 0))],
        out_specs=pl.BlockSpec((TM, D), lambda t: (t, 0)),
        out_shape=jax.ShapeDtypeStruct((T, D), f32),
        compiler_params=_cp(("arbitrary",)),
    )(proj, yb_p, yb_s, x, mod, conv_w, w_out)


def _ffn_kernel(x_ref, g_ref, sc_ref, sh_ref, g2_ref, w1_ref, w3_ref, w2_ref, o_ref, h_ref, acc_ref):
    j = pl.program_id(1)

    @pl.when(j == 0)
    def _():
        h_ref[...] = _norm_mod(x_ref[...], g_ref[0], sc_ref[0], sh_ref[0]).astype(bf16)
        acc_ref[...] = jnp.zeros_like(acc_ref)

    h = h_ref[...]
    gate = jax.nn.silu(_dot(h, w1_ref[...].astype(bf16))) * _dot(h, w3_ref[...].astype(bf16))
    acc_ref[...] += _dot(gate.astype(bf16), w2_ref[...].astype(bf16))

    @pl.when(j == pl.num_programs(1) - 1)
    def _():
        o_ref[...] = x_ref[...] + g2_ref[0] * acc_ref[...]


def _dense_ffn(x, mod, norm_g, w1, w3, w2, layer):
    li = layer // 2
    tf = TF_DENSE
    return pl.pallas_call(
        _ffn_kernel,
        grid=(NT, FFN_DENSE // tf),
        in_specs=[pl.BlockSpec((TM, D), lambda i, j: (i, 0)),
                  pl.BlockSpec((1, 1, D), lambda i, j: (layer, 0, 0)),
                  _mod_spec(layer, 4), _mod_spec(layer, 3), _mod_spec(layer, 5),
                  pl.BlockSpec((None, D, tf), lambda i, j: (li, 0, j)),
                  pl.BlockSpec((None, D, tf), lambda i, j: (li, 0, j)),
                  pl.BlockSpec((None, tf, D), lambda i, j: (li, j, 0))],
        out_specs=pl.BlockSpec((TM, D), lambda i, j: (i, 0)),
        out_shape=jax.ShapeDtypeStruct((T, D), f32),
        scratch_shapes=[pltpu.VMEM((TM, D), bf16), pltpu.VMEM((TM, D), f32)],
        compiler_params=_cp(("arbitrary", "arbitrary")),
    )(x, norm_g, mod, mod, mod, w1, w3, w2)


GROUP_CH = D // FOURIER_GROUPS


def _dft_mats(n):
    k = np.arange(n, dtype=np.int64)
    ang = 2.0 * np.pi * ((k[:, None] * k[None, :]) % n).astype(np.float64) / n
    return np.cos(ang).astype(np.float32), np.sin(ang).astype(np.float32)


def _fourier_kernel(x_ref, g_ref, sc_ref, sh_ref, g1_ref, cs_ref, ss_ref, cl_ref, sl_ref, wf_ref,
                    o_ref, f_ref):
    t = pl.program_id(0)
    h = _norm_mod(x_ref[...], g_ref[0], sc_ref[0], sh_ref[0]).astype(bf16)
    cs = cs_ref[...].astype(bf16)
    ss = ss_ref[...].astype(bf16)
    ys, zs = [], []
    for g in range(FOURIER_GROUPS):
        hg = h[:, g * GROUP_CH:(g + 1) * GROUP_CH]
        ys.append(_dot(hg, cs))
        zs.append(_dot(hg, ss))
    y = jnp.concatenate(ys, axis=-1).astype(bf16)
    z = jnp.concatenate(zs, axis=-1).astype(bf16)

    @pl.when(t < NTP)
    def _():
        for s in range(TM // SEQ):
            rows = slice(s * SEQ, (s + 1) * SEQ)
            f = _dot(cs, y[rows]) - _dot(ss, z[rows])
            f_ref[rows, :] = (f * ((SEQ * GROUP_CH) ** -0.5)).astype(bf16)

    @pl.when(t >= NTP)
    def _():
        f = _dot(cl_ref[...].astype(bf16), y) - _dot(sl_ref[...].astype(bf16), z)
        f_ref[...] = (f * ((DEC_SEQ * GROUP_CH) ** -0.5)).astype(bf16)

    o_ref[...] = x_ref[...] + g1_ref[0] * _dot(f_ref[...], wf_ref[...].astype(bf16))


def _fourier(x, mod, norm_g, fourier_w, layer):
    li = layer // 2
    c_s, s_s = _dft_mats(SEQ)
    c_l, s_l = _dft_mats(DEC_SEQ)
    const = lambda shape: pl.BlockSpec(shape, lambda t: (0, 0))
    return pl.pallas_call(
        _fourier_kernel,
        grid=(NT,),
        in_specs=[pl.BlockSpec((TM, D), lambda t: (t, 0)),
                  pl.BlockSpec((1, 1, D), lambda t: (layer, 0, 0)),
                  _mod_spec(layer, 1), _mod_spec(layer, 0), _mod_spec(layer, 2),
                  const((SEQ, SEQ)), const((SEQ, SEQ)),
                  const((DEC_SEQ, DEC_SEQ)), const((DEC_SEQ, DEC_SEQ)),
                  pl.BlockSpec((None, D, D), lambda t: (li, 0, 0))],
        out_specs=pl.BlockSpec((TM, D), lambda t: (t, 0)),
        out_shape=jax.ShapeDtypeStruct((T, D), f32),
        scratch_shapes=[pltpu.VMEM((TM, D), bf16)],
        compiler_params=_cp(("arbitrary",)),
    )(x, norm_g, mod, mod, mod, jnp.asarray(c_s), jnp.asarray(s_s), jnp.asarray(c_l), jnp.asarray(s_l),
      fourier_w)


ROW_TILE = D // LANES


def _to_token_tiles(ref, val, rows):
    for s in range(ROW_TILE):
        ref[pl.ds(s, rows, stride=ROW_TILE), :] = val[:, s * LANES:(s + 1) * LANES]


M_E0, M_E1, M_R0, M_R1, M_W0, M_W1 = range(6)


def _router_kernel(x_ref, g_ref, sc_ref, sh_ref, rw_ref, rb_ref, h_ref, meta_ref, cnt_ref, carry_ref):
    t = pl.program_id(0)

    @pl.when(t == 0)
    def _():
        carry_ref[...] = jnp.zeros_like(carry_ref)

    h = _norm_mod(x_ref[...], g_ref[0], sc_ref[0], sh_ref[0])
    _to_token_tiles(h_ref, h, TM)
    logits = jnp.dot(h, rw_ref[...], preferred_element_type=f32,
                     precision=lax.Precision.HIGHEST) + rb_ref[...]
    lane = lax.broadcasted_iota(jnp.int32, (TM, LANES), 1)
    m1 = jnp.max(logits, axis=-1, keepdims=True)
    i1 = jnp.min(jnp.where(logits == m1, lane, LANES), axis=-1, keepdims=True)
    rest = jnp.where(lane == i1, -jnp.inf, logits)
    m2 = jnp.max(rest, axis=-1, keepdims=True)
    i2 = jnp.min(jnp.where(rest == m2, lane, LANES), axis=-1, keepdims=True)
    e = jnp.exp(m2 - m1)
    w0 = 1.0 / (1.0 + e)
    w1 = e / (1.0 + e)
    oh0 = (lane == i1).astype(f32)
    oh1 = (lane == i2).astype(f32)
    oh = oh0 + oh1
    row = lax.broadcasted_iota(jnp.int32, (TM, TM), 0)
    col = lax.broadcasted_iota(jnp.int32, (TM, TM), 1)
    before = jnp.where(col < row, 1.0, 0.0).astype(bf16)
    base = carry_ref[...] + _dot(before, oh.astype(bf16))
    r0 = jnp.sum(oh0 * base, axis=-1, keepdims=True)
    r1 = jnp.sum(oh1 * base, axis=-1, keepdims=True)
    carry_ref[...] += jnp.sum(oh, axis=0, keepdims=True)
    rec = jnp.zeros((TM, LANES), f32)
    for idx, val in ((M_E0, i1.astype(f32)), (M_E1, i2.astype(f32)), (M_R0, r0), (M_R1, r1),
                     (M_W0, w0), (M_W1, w1)):
        rec = jnp.where(lane == idx, val, rec)
    meta_ref[...] = rec
    cnt_ref[...] = carry_ref[...]


def _router(x, mod, norm_g, router_w, router_b, layer):
    li = layer // 2
    rw = jnp.pad(router_w[li], ((0, 0), (0, LANES - N_EXPERTS)))
    rb = jnp.pad(router_b[li], (0, LANES - N_EXPERTS), constant_values=NEG_INF).reshape(1, LANES)
    return pl.pallas_call(
        _router_kernel,
        grid=(NT,),
        in_specs=[pl.BlockSpec((TM, D), lambda t: (t, 0)),
                  pl.BlockSpec((1, 1, D), lambda t: (layer, 0, 0)),
                  _mod_spec(layer, 4), _mod_spec(layer, 3),
                  pl.BlockSpec((D, LANES), lambda t: (0, 0)),
                  pl.BlockSpec((1, LANES), lambda t: (0, 0))],
        out_specs=[pl.BlockSpec((TM * ROW_TILE, LANES), lambda t: (t, 0)),
                   pl.BlockSpec((TM, LANES), lambda t: (t, 0)),
                   pl.BlockSpec((1, LANES), lambda t: (0, 0))],
        out_shape=[jax.ShapeDtypeStruct((T * ROW_TILE, LANES), f32),
                   jax.ShapeDtypeStruct((T, LANES), f32),
                   jax.ShapeDtypeStruct((1, LANES), f32)],
        scratch_shapes=[pltpu.VMEM((1, LANES), f32)],
        compiler_params=_cp(("arbitrary",)),
    )(x, norm_g, mod, mod, rw, rb)


def _token_copy(src_ref, src_row, dst_ref, dst_row, sem):
    return pltpu.make_async_copy(
        src_ref.at[pl.ds(pl.multiple_of(src_row * ROW_TILE, ROW_TILE), ROW_TILE)],
        dst_ref.at[pl.ds(pl.multiple_of(dst_row * ROW_TILE, ROW_TILE), ROW_TILE)], sem)


def _dispatch_kernel(dest_ref, h_ref, xg_in_ref, xg_ref, sem):
    del xg_in_ref

    def issue(r, c):
        for k in range(2):
            _token_copy(h_ref, r, xg_ref, dest_ref[0, 0, 2 * r + k], sem).start()
        return c

    def drain(r, c):
        for k in range(2):
            _token_copy(h_ref, r, xg_ref, dest_ref[0, 0, 2 * r + k], sem).wait()
        return c

    lax.fori_loop(0, TM, issue, 0)
    lax.fori_loop(0, TM, drain, 0)


def _dispatch(dest, h_tiles):
    xg0 = jnp.zeros((R_E * ROW_TILE, LANES), f32)
    return pl.pallas_call(
        _dispatch_kernel,
        grid=(NT,),
        in_specs=[pl.BlockSpec((1, 1, 2 * TM), lambda t: (t, 0, 0), memory_space=pltpu.SMEM),
                  pl.BlockSpec((TM * ROW_TILE, LANES), lambda t: (t, 0)),
                  pl.BlockSpec(memory_space=pl.ANY)],
        out_specs=pl.BlockSpec(memory_space=pl.ANY),
        out_shape=jax.ShapeDtypeStruct((R_E * ROW_TILE, LANES), f32),
        scratch_shapes=[pltpu.SemaphoreType.DMA(())],
        input_output_aliases={2: 0},
        compiler_params=_cp(("arbitrary",)),
    )(dest.reshape(NT, 1, 2 * TM), h_tiles, xg0)


def _expert_kernel(te_ref, nu_ref, xg_ref, w1_ref, w3_ref, w2_ref, y_ref, h_ref, acc_ref):
    t = pl.program_id(0)
    j = pl.program_id(1)
    nj = pl.num_programs(1)
    used = t < nu_ref[0]

    @pl.when(used & (j == 0))
    def _():
        for s in range(ROW_TILE):
            h_ref[:, s * LANES:(s + 1) * LANES] = xg_ref[pl.ds(s, TM_E, stride=ROW_TILE), :].astype(bf16)
        acc_ref[...] = jnp.zeros_like(acc_ref)

    @pl.when(used)
    def _():
        h = h_ref[...]
        gate = jax.nn.silu(_dot(h, w1_ref[...].astype(bf16))) * _dot(h, w3_ref[...].astype(bf16))
        acc_ref[...] += _dot(gate.astype(bf16), w2_ref[...].astype(bf16))

    @pl.when(used & (j == nj - 1))
    def _():
        _to_token_tiles(y_ref, acc_ref[...], TM_E)

    @pl.when(jnp.logical_not(used) & (j == nj - 1))
    def _():
        y_ref[...] = jnp.zeros_like(y_ref)


def _experts(tile_expert, n_used, xg, w1, w3, w2, layer):
    li = layer // 2
    tf = TF_EXP
    nj = FFN_EXPERT // tf

    def jj(t, j, nu):
        return jnp.where(t < nu[0], j, nj - 1)

    def tt(t, nu):
        return jnp.minimum(t, nu[0] - 1)

    grid_spec = pltpu.PrefetchScalarGridSpec(
        num_scalar_prefetch=2,
        grid=(NT_E, nj),
        in_specs=[pl.BlockSpec((TM_E * ROW_TILE, LANES), lambda t, j, te, nu: (tt(t, nu), 0)),
                  pl.BlockSpec((None, None, D, tf), lambda t, j, te, nu: (li, te[t], 0, jj(t, j, nu))),
                  pl.BlockSpec((None, None, D, tf), lambda t, j, te, nu: (li, te[t], 0, jj(t, j, nu))),
                  pl.BlockSpec((None, None, tf, D), lambda t, j, te, nu: (li, te[t], jj(t, j, nu), 0))],
        out_specs=pl.BlockSpec((TM_E * ROW_TILE, LANES), lambda t, j, te, nu: (t, 0)),
        scratch_shapes=[pltpu.VMEM((TM_E, D), bf16), pltpu.VMEM((TM_E, D), f32)],
    )
    return pl.pallas_call(
        _expert_kernel,
        grid_spec=grid_spec,
        out_shape=jax.ShapeDtypeStruct((R_E * ROW_TILE, LANES), f32),
        compiler_params=_cp(("arbitrary", "arbitrary")),
    )(tile_expert, n_used, xg, w1, w3, w2)


def _combine_kernel(dest_ref, x_ref, g2_ref, meta_ref, y_ref, o_ref, buf_ref, sem):
    def issue(r, c):
        for k in range(2):
            _token_copy(y_ref, dest_ref[0, 0, 2 * r + k], buf_ref.at[k], r, sem).start()
        return c

    def drain(r, c):
        for k in range(2):
            _token_copy(y_ref, dest_ref[0, 0, 2 * r + k], buf_ref.at[k], r, sem).wait()
        return c

    lax.fori_loop(0, TM, issue, 0)
    lax.fori_loop(0, TM, drain, 0)
    w0 = meta_ref[:, M_W0:M_W0 + 1]
    w1 = meta_ref[:, M_W1:M_W1 + 1]
    for s in range(ROW_TILE):
        cols = slice(s * LANES, (s + 1) * LANES)
        y0 = buf_ref[0, pl.ds(s, TM, stride=ROW_TILE), :]
        y1 = buf_ref[1, pl.ds(s, TM, stride=ROW_TILE), :]
        o_ref[:, cols] = x_ref[:, cols] + g2_ref[0][:, cols] * (w0 * y0 + w1 * y1)


def _combine(dest, x, mod, meta, y, layer):
    return pl.pallas_call(
        _combine_kernel,
        grid=(NT,),
        in_specs=[pl.BlockSpec((1, 1, 2 * TM), lambda t: (t, 0, 0), memory_space=pltpu.SMEM),
                  pl.BlockSpec((TM, D), lambda t: (t, 0)),
                  _mod_spec(layer, 5),
                  pl.BlockSpec((TM, LANES), lambda t: (t, 0)),
                  pl.BlockSpec(memory_space=pl.ANY)],
        out_specs=pl.BlockSpec((TM, D), lambda t: (t, 0)),
        out_shape=jax.ShapeDtypeStruct((T, D), f32),
        scratch_shapes=[pltpu.VMEM((2, TM * ROW_TILE, LANES), f32), pltpu.SemaphoreType.DMA(())],
        compiler_params=_cp(("arbitrary",)),
    )(dest.reshape(NT, 1, 2 * TM), x, mod, meta, y)


def _moe(x, mod, norm_g, router_w, router_b, w1, w3, w2, layer):
    h_tiles, meta, counts = _router(x, mod, norm_g, router_w, router_b, layer)
    cnt = counts[0, :N_EXPERTS].astype(jnp.int32)
    padded = ((cnt + TM_E - 1) // TM_E) * TM_E
    ends = jnp.cumsum(padded)
    starts = ends - padded
    experts = meta[:, M_E0:M_E1 + 1].astype(jnp.int32)
    ranks = meta[:, M_R0:M_R1 + 1].astype(jnp.int32)
    dest = starts[experts] + ranks
    n_used = (ends[-1] // TM_E).astype(jnp.int32).reshape(1)
    tile_start = jnp.minimum(jnp.arange(NT_E, dtype=jnp.int32), n_used[0] - 1) * TM_E
    tile_expert = jnp.sum((tile_start[:, None] >= ends[None, :]).astype(jnp.int32), axis=1)
    tile_expert = jnp.minimum(tile_expert, N_EXPERTS - 1).astype(jnp.int32)
    xg = _dispatch(dest, h_tiles)
    y = _experts(tile_expert, n_used, xg, w1, w3, w2, layer)
    return _combine(dest, x, mod, meta, y, layer)


def _final_kernel(x_ref, g_ref, o_ref):
    x = x_ref[...]
    o_ref[...] = (x * lax.rsqrt(jnp.mean(x * x, axis=-1, keepdims=True) + RMS_EPS)) * g_ref[...]


def _final_norm(x, final_g, first_tile, n_tiles):
    return pl.pallas_call(
        _final_kernel,
        grid=(n_tiles,),
        in_specs=[pl.BlockSpec((TM, D), lambda t: (first_tile + t, 0)),
                  pl.BlockSpec((1, D), lambda t: (0, 0))],
        out_specs=pl.BlockSpec((TM, D), lambda t: (t, 0)),
        out_shape=jax.ShapeDtypeStruct((n_tiles * TM, D), f32),
        compiler_params=_cp(("arbitrary",)),
    )(x, final_g.reshape(1, D))


def kernel(x_prompt, x_sample, c, cache_k, cache_v, c_ctx, ada_w, ada_b, norm1_g, norm2_g, w_in, conv_w, rpb,
           w_out, ffn_w1, ffn_w3, ffn_w2, fourier_w, router_w, router_b, moe_w1, moe_w3, moe_w2, final_g):
    cvecs = jnp.concatenate([c_ctx[None, :], c, jnp.zeros((MOD_ROWS - 1 - DEC_BATCH, D), f32)], axis=0)
    mod = _modulation(cvecs, ada_w, ada_b).reshape(DEPTH * MOD_ROWS * 6, 1, D)
    n1 = norm1_g.reshape(DEPTH, 1, D)
    n2 = norm2_g.reshape(DEPTH, 1, D)
    bias = _nbr_bias(rpb)
    x = jnp.concatenate([x_prompt.reshape(T_P, D), x_sample.reshape(T_S, D)], axis=0)
    new_k, new_v = [], []
    for layer in range(DEPTH):
        li = layer // 2
        if layer % 2 == 0:
            proj, kc, vc = _inproj(x, mod, n1, w_in, layer)
            new_k.append(kc)
            new_v.append(vc)
            yb_p = _ctx_attention(proj)
            yb_s = _nbr_attention(proj, cache_k, cache_v, bias, li)
            x = _mixout(proj, yb_p, yb_s, x, mod, conv_w, w_out, layer)
            x = _dense_ffn(x, mod, n2, ffn_w1, ffn_w3, ffn_w2, layer)
        else:
            x = _fourier(x, mod, n1, fourier_w, layer)
            x = _moe(x, mod, n2, router_w, router_b, moe_w1, moe_w3, moe_w2, layer)
    y_prompt = _final_norm(x, final_g, 0, NTP).reshape(BATCH, SEQ, D)
    y_sample = _final_norm(x, final_g, NTP, NT - NTP).reshape(DEC_BATCH, DEC_SEQ, D)
    return (y_prompt, y_sample, jnp.stack(new_k, axis=1), jnp.stack(new_v, axis=1))
```

```python
import functools

import numpy as np
import jax
import jax.numpy as jnp
from jax import lax
from jax.experimental import pallas as pl
from jax.experimental.pallas import tpu as pltpu

f32 = jnp.float32
bf16 = jnp.bfloat16

D = 1024
BATCH = 16
SEQ = 256
DEPTH = 4
DEC_BATCH = 8
DEC_SEQ = 1024
PAST_LEN = 512
GRID_W = 64
CONV_CH = 512
NA_HEADS = 8
HEAD_DIM = 64
NA_WIDTH = 512
WIN_ROWS = 8
WIN_COLS = 16
IN_WIDTH = 3072
FOURIER_GROUPS = 4
FFN_DENSE = 2816
N_EXPERTS = 8
FFN_EXPERT = 3584
N_EVEN = 2
RMS_EPS = 1e-6
NEG_INF = -1e30
SCALE = HEAD_DIM ** -0.5

LANES = 128
TM = 1024
T_P = BATCH * SEQ
T_S = DEC_BATCH * DEC_SEQ
T = T_P + T_S
NT = T // TM
NTP = T_P // TM
MOD_ROWS = 16
GRID_ROWS = DEC_SEQ // GRID_W
CHUNK_ROWS = 4
N_CHUNKS = GRID_ROWS // CHUNK_ROWS
CHUNK_Q = CHUNK_ROWS * GRID_W
KWIN_ROWS = 12
KWIN = KWIN_ROWS * GRID_W
KB0 = (0, 0, 4, 4)
CHUNKS_PER_GROUP = 4
TF_DENSE = 256
TF_EXP = 512
TM_E = 1024
N_ASSIGN = 2 * T
NT_E = N_ASSIGN // TM_E + N_EXPERTS
R_E = NT_E * TM_E
VMEM_LIMIT = 56 * 1024 * 1024


def _cp(sem, vmem=VMEM_LIMIT):
    return pltpu.CompilerParams(dimension_semantics=sem, vmem_limit_bytes=vmem)


def _mod_index(t):
    return jnp.where(t < NTP, 0, t - (NTP - 1))


def _mod_spec(layer, part):
    def index(t, *_):
        return ((layer * MOD_ROWS + _mod_index(t)) * 6 + part, 0, 0)
    return pl.BlockSpec((1, 1, D), index)


def _norm_mod(x, g, sc, sh):
    y = x * lax.rsqrt(jnp.mean(x * x, axis=-1, keepdims=True) + RMS_EPS)
    return (y * g) * (1 + sc) + sh


def _dot(a, b):
    return jnp.dot(a, b, preferred_element_type=f32)


def _dot_nt(a, b):
    return lax.dot_general(a, b, (((1,), (1,)), ((), ())), preferred_element_type=f32)


def _mod_kernel(cv_ref, w_ref, b_ref, o_ref):
    s = jax.nn.silu(cv_ref[...]).astype(bf16)
    o_ref[0] = _dot(s, w_ref[0].astype(bf16)) + b_ref[0]


def _modulation(cvecs, ada_w, ada_b):
    tn = 1536
    return pl.pallas_call(
        _mod_kernel,
        grid=(DEPTH, 6 * D // tn),
        in_specs=[pl.BlockSpec((MOD_ROWS, D), lambda l, j: (0, 0)),
                  pl.BlockSpec((1, D, tn), lambda l, j: (l, 0, j)),
                  pl.BlockSpec((1, 1, tn), lambda l, j: (l, 0, j))],
        out_specs=pl.BlockSpec((1, MOD_ROWS, tn), lambda l, j: (l, 0, j)),
        out_shape=jax.ShapeDtypeStruct((DEPTH, MOD_ROWS, 6 * D), f32),
        compiler_params=_cp(("arbitrary", "arbitrary")),
    )(cvecs, ada_w, ada_b.reshape(DEPTH, 1, 6 * D))


TN_IN = IN_WIDTH // 2


def _inproj_kernel(x_ref, g_ref, sc_ref, sh_ref, w_ref, proj_ref, kc_ref, vc_ref, h_ref):
    i = pl.program_id(0)
    j = pl.program_id(1)

    @pl.when(j == 0)
    def _():
        h_ref[...] = _norm_mod(x_ref[...], g_ref[0], sc_ref[0], sh_ref[0]).astype(bf16)

    acc = _dot(h_ref[...], w_ref[...].astype(bf16))
    proj_ref[...] = acc.astype(bf16)

    @pl.when((j == 1) & (i < NTP))
    def _():
        for ref, col0 in ((kc_ref, NA_WIDTH), (vc_ref, 2 * NA_WIDTH)):
            for s in range(TM // SEQ):
                for hd in range(NA_HEADS):
                    c0 = col0 + hd * HEAD_DIM
                    ref[s, hd] = acc[s * SEQ:(s + 1) * SEQ, c0:c0 + HEAD_DIM]


def _inproj(x, mod, norm_g, w_in, layer):
    li = layer // 2
    tn = TN_IN
    seqs = TM // SEQ
    cache_spec = pl.BlockSpec((seqs, NA_HEADS, SEQ, HEAD_DIM),
                              lambda i, j: (jnp.minimum(i, NTP - 1), 0, 0, 0))
    cache_shape = jax.ShapeDtypeStruct((BATCH, NA_HEADS, SEQ, HEAD_DIM), f32)
    return pl.pallas_call(
        _inproj_kernel,
        grid=(NT, IN_WIDTH // tn),
        in_specs=[pl.BlockSpec((TM, D), lambda i, j: (i, 0)),
                  pl.BlockSpec((1, 1, D), lambda i, j: (layer, 0, 0)),
                  _mod_spec(layer, 1), _mod_spec(layer, 0),
                  pl.BlockSpec((None, D, tn), lambda i, j: (li, 0, j))],
        out_specs=[pl.BlockSpec((TM, tn), lambda i, j: (i, j)), cache_spec, cache_spec],
        out_shape=[jax.ShapeDtypeStruct((T, IN_WIDTH), bf16), cache_shape, cache_shape],
        scratch_shapes=[pltpu.VMEM((TM, D), bf16)],
        compiler_params=_cp(("arbitrary", "arbitrary")),
    )(x, norm_g, mod, mod, w_in)


HEADS_PER_STEP = LANES // HEAD_DIM


def _pair_attention(problems):
    lane = lax.broadcasted_iota(jnp.int32, (1, LANES), 1)
    first = lane < HEAD_DIM
    mine = (first, jnp.logical_not(first))
    scores = []
    for q, k_parts, _, bias_parts in problems:
        q = q * SCALE
        for hh in range(HEADS_PER_STEP):
            qh = jnp.where(mine[hh], q, 0)
            ss = []
            for k, b in zip(k_parts, bias_parts):
                s = _dot_nt(qh, k)
                ss.append(s if b is None else s + b[hh])
            scores.append(ss)
    probs = []
    for ss in scores:
        m = functools.reduce(jnp.maximum, [jnp.max(s, axis=-1, keepdims=True) for s in ss])
        probs.append([jnp.exp((s - m).astype(bf16)) for s in ss])
    results = []
    for n, (_, _, v_parts, _) in enumerate(problems):
        outs = []
        for hh in range(HEADS_PER_STEP):
            pv = functools.reduce(jnp.add, [_dot(p, jnp.where(mine[hh], v, 1))
                                            for p, v in zip(probs[n * HEADS_PER_STEP + hh], v_parts)])
            denom = pv[:, HEAD_DIM:HEAD_DIM + 1] if hh == 0 else pv[:, 0:1]
            outs.append(pv / denom)
        results.append(jnp.where(first, outs[0], outs[1]))
    return results


def _ctx_attn_kernel(q_ref, k_ref, v_ref, o_ref):
    cols = [slice(hp * LANES, (hp + 1) * LANES) for hp in range(NA_HEADS // HEADS_PER_STEP)]
    outs = _pair_attention([(q_ref[:, sl], [k_ref[:, sl]], [v_ref[:, sl]], [None]) for sl in cols])
    for sl, o in zip(cols, outs):
        o_ref[:, sl] = o.astype(bf16)


def _ctx_attention(proj):
    col = IN_WIDTH // NA_WIDTH - 3
    return pl.pallas_call(
        _ctx_attn_kernel,
        grid=(BATCH,),
        in_specs=[pl.BlockSpec((SEQ, NA_WIDTH), lambda b: (b, col)),
                  pl.BlockSpec((SEQ, NA_WIDTH), lambda b: (b, col + 1)),
                  pl.BlockSpec((SEQ, NA_WIDTH), lambda b: (b, col + 2))],
        out_specs=pl.BlockSpec((SEQ, NA_WIDTH), lambda b: (b, 0)),
        out_shape=jax.ShapeDtypeStruct((T_P, NA_WIDTH), bf16),
        compiler_params=_cp(("arbitrary",)),
    )(proj, proj, proj)


def _win_start(qr):
    return min(max(qr - WIN_ROWS // 2, 0), GRID_ROWS - WIN_ROWS)


N_ROFF = 2 * WIN_ROWS - 1
N_COFF = 2 * WIN_COLS - 1


def _bias_kernel(rpb_ref, o_ref):
    base = (pl.program_id(0) * NA_HEADS + pl.program_id(1)) * (N_ROFF * N_COFF)
    qc = lax.broadcasted_iota(jnp.int32, (GRID_W, GRID_W), 0)
    kc = lax.broadcasted_iota(jnp.int32, (GRID_W, GRID_W), 1)
    coff = kc - qc + (WIN_COLS - 1)
    cs = jnp.clip(qc - WIN_COLS // 2, 0, GRID_W - WIN_COLS)
    valid = (kc >= cs) & (kc < cs + WIN_COLS)
    neg = jnp.full((GRID_W, GRID_W), NEG_INF, f32)
    blocks = []
    for a in range(N_ROFF):
        t = jnp.zeros((GRID_W, GRID_W), f32)
        for b in range(N_COFF):
            t = jnp.where(coff == b, rpb_ref[base + a * N_COFF + b], t)
        blocks.append(jnp.where(valid, t, neg))
    for c in range(N_CHUNKS):
        for ql in range(CHUNK_ROWS):
            qr = c * CHUNK_ROWS + ql
            st = _win_start(qr)
            for kl in range(KWIN_ROWS):
                kr = KB0[c] + kl
                blk = blocks[kr - qr + WIN_ROWS - 1] if st <= kr < st + WIN_ROWS else neg
                o_ref[c, ql * GRID_W:(ql + 1) * GRID_W, kl * GRID_W:(kl + 1) * GRID_W] = blk


def _nbr_bias(rpb):
    return pl.pallas_call(
        _bias_kernel,
        grid=(N_EVEN, NA_HEADS),
        in_specs=[pl.BlockSpec(memory_space=pltpu.SMEM)],
        out_specs=pl.BlockSpec((None, None, N_CHUNKS, CHUNK_Q, KWIN), lambda i, h: (i, h, 0, 0, 0)),
        out_shape=jax.ShapeDtypeStruct((N_EVEN, NA_HEADS, N_CHUNKS, CHUNK_Q, KWIN), f32),
        compiler_params=_cp(("arbitrary", "arbitrary")),
    )(rpb.reshape(-1))


def _nbr_attn_kernel(q_ref, k_ref, v_ref, kc_ref, vc_ref, bias_ref, o_ref):
    kctx = jnp.concatenate([kc_ref[0], kc_ref[1]], axis=-1).astype(bf16)
    vctx = jnp.concatenate([vc_ref[0], vc_ref[1]], axis=-1).astype(bf16)
    for c0 in range(0, N_CHUNKS, CHUNKS_PER_GROUP):
        problems = []
        for c in range(c0, c0 + CHUNKS_PER_GROUP):
            rows = slice(c * CHUNK_Q, (c + 1) * CHUNK_Q)
            win = slice(KB0[c] * GRID_W, KB0[c] * GRID_W + KWIN)
            bias = [bias_ref[hh, c] for hh in range(HEADS_PER_STEP)]
            problems.append((q_ref[rows, :], [k_ref[win, :], kctx], [v_ref[win, :], vctx], [bias, None]))
        for c, o in zip(range(c0, c0 + CHUNKS_PER_GROUP), _pair_attention(problems)):
            o_ref[c * CHUNK_Q:(c + 1) * CHUNK_Q, :] = o.astype(bf16)


def _nbr_attention(proj, cache_k, cache_v, bias, li):
    qcol = 3 * CONV_CH // LANES
    ncol = NA_WIDTH // LANES
    hp_steps = NA_HEADS // HEADS_PER_STEP

    def col_spec(which):
        return pl.BlockSpec((DEC_SEQ, LANES), lambda hp, b: (NTP + b, qcol + which * ncol + hp))

    ctx_spec = pl.BlockSpec((None, None, HEADS_PER_STEP, PAST_LEN, HEAD_DIM),
                            lambda hp, b: (b, li, hp, 0, 0))
    return pl.pallas_call(
        _nbr_attn_kernel,
        grid=(hp_steps, DEC_BATCH),
        in_specs=[col_spec(0), col_spec(1), col_spec(2), ctx_spec, ctx_spec,
                  pl.BlockSpec((None, HEADS_PER_STEP, N_CHUNKS, CHUNK_Q, KWIN),
                               lambda hp, b: (li, hp, 0, 0, 0))],
        out_specs=pl.BlockSpec((DEC_SEQ, LANES), lambda hp, b: (b, hp)),
        out_shape=jax.ShapeDtypeStruct((T_S, NA_WIDTH), bf16),
        compiler_params=_cp(("arbitrary", "arbitrary")),
    )(proj, proj, proj, cache_k, cache_v, bias)


def _mixout_kernel(a_ref, ybp_ref, ybs_ref, x_ref, g1_ref, cw_ref, w_ref, o_ref):
    t = pl.program_id(0)
    a_b = a_ref[:, 0:CONV_CH].astype(f32)
    a_c = a_ref[:, CONV_CH:2 * CONV_CH].astype(f32)
    a_x = a_ref[:, 2 * CONV_CH:3 * CONV_CH].astype(f32)
    u = a_c * a_x
    r = lax.broadcasted_iota(jnp.int32, (TM, 1), 0)
    pos = jnp.where(t < NTP, r % SEQ, r)
    last = jnp.where(t < NTP, SEQ - 1, DEC_SEQ - 1)
    u_prev = jnp.where(pos == 0, 0.0, pltpu.roll(u, 1, axis=0))
    u_next = jnp.where(pos == last, 0.0, pltpu.roll(u, TM - 1, axis=0))
    y_a = a_b * (u_prev * cw_ref[0:1, :] + u * cw_ref[1:2, :] + u_next * cw_ref[2:3, :])
    y_b = jnp.where(t < NTP, ybp_ref[...], ybs_ref[...])
    y = (_dot(y_a.astype(bf16), w_ref[0:CONV_CH, :].astype(bf16))
         + _dot(y_b, w_ref[CONV_CH:, :].astype(bf16)))
    o_ref[...] = x_ref[...] + g1_ref[0] * y


def _mixout(proj, yb_p, yb_s, x, mod, conv_w, w_out, layer):
    li = layer // 2
    return pl.pallas_call(
        _mixout_kernel,
        grid=(NT,),
        in_specs=[pl.BlockSpec((TM, 3 * CONV_CH), lambda t: (t, 0)),
                  pl.BlockSpec((TM, NA_WIDTH), lambda t: (jnp.minimum(t, NTP - 1), 0)),
                  pl.BlockSpec((TM, NA_WIDTH), lambda t: (jnp.maximum(t - NTP, 0), 0)),
                  pl.BlockSpec((TM, D), lambda t: (t, 0)),
                  _mod_spec(layer, 2),
                  pl.BlockSpec((None, 3, CONV_CH), lambda t: (li, 0, 0)),
                  pl.BlockSpec((None, D, D), lambda t: (li, 0, 0))],
        out_specs=pl.BlockSpec((TM, D), lambda t: (t, 0)),
        out_shape=jax.ShapeDtypeStruct((T, D), f32),
        compiler_params=_cp(("arbitrary",)),
    )(proj, yb_p, yb_s, x, mod, conv_w, w_out)


def _ffn_kernel(x_ref, g_ref, sc_ref, sh_ref, g2_ref, w1_ref, w3_ref, w2_ref, o_ref, h_ref, acc_ref):
    j = pl.program_id(1)

    @pl.when(j == 0)
    def _():
        h_ref[...] = _norm_mod(x_ref[...], g_ref[0], sc_ref[0], sh_ref[0]).astype(bf16)
        acc_ref[...] = jnp.zeros_like(acc_ref)

    h = h_ref[...]
    gate = jax.nn.silu(_dot(h, w1_ref[...].astype(bf16))) * _dot(h, w3_ref[...].astype(bf16))
    acc_ref[...] += _dot(gate.astype(bf16), w2_ref[...].astype(bf16))

    @pl.when(j == pl.num_programs(1) - 1)
    def _():
        o_ref[...] = x_ref[...] + g2_ref[0] * acc_ref[...]


def _dense_ffn(x, mod, norm_g, w1, w3, w2, layer):
    li = layer // 2
    tf = TF_DENSE
    return pl.pallas_call(
        _ffn_kernel,
        grid=(NT, FFN_DENSE // tf),
        in_specs=[pl.BlockSpec((TM, D), lambda i, j: (i, 0)),
                  pl.BlockSpec((1, 1, D), lambda i, j: (layer, 0, 0)),
                  _mod_spec(layer, 4), _mod_spec(layer, 3), _mod_spec(layer, 5),
                  pl.BlockSpec((None, D, tf), lambda i, j: (li, 0, j)),
                  pl.BlockSpec((None, D, tf), lambda i, j: (li, 0, j)),
                  pl.BlockSpec((None, tf, D), lambda i, j: (li, j, 0))],
        out_specs=pl.BlockSpec((TM, D), lambda i, j: (i, 0)),
        out_shape=jax.ShapeDtypeStruct((T, D), f32),
        scratch_shapes=[pltpu.VMEM((TM, D), bf16), pltpu.VMEM((TM, D), f32)],
        compiler_params=_cp(("arbitrary", "arbitrary")),
    )(x, norm_g, mod, mod, mod, w1, w3, w2)


GROUP_CH = D // FOURIER_GROUPS


def _dft_mats(n):
    k = np.arange(n, dtype=np.int64)
    ang = 2.0 * np.pi * ((k[:, None] * k[None, :]) % n).astype(np.float64) / n
    return np.cos(ang).astype(np.float32), np.sin(ang).astype(np.float32)


def _fourier_kernel(x_ref, g_ref, sc_ref, sh_ref, g1_ref, cs_ref, ss_ref, cl_ref, sl_ref, wf_ref,
                    o_ref, f_ref):
    t = pl.program_id(0)
    h = _norm_mod(x_ref[...], g_ref[0], sc_ref[0], sh_ref[0]).astype(bf16)
    cs = cs_ref[...].astype(bf16)
    ss = ss_ref[...].astype(bf16)
    ys, zs = [], []
    for g in range(FOURIER_GROUPS):
        hg = h[:, g * GROUP_CH:(g + 1) * GROUP_CH]
        ys.append(_dot(hg, cs))
        zs.append(_dot(hg, ss))
    y = jnp.concatenate(ys, axis=-1).astype(bf16)
    z = jnp.concatenate(zs, axis=-1).astype(bf16)

    @pl.when(t < NTP)
    def _():
        for s in range(TM // SEQ):
            rows = slice(s * SEQ, (s + 1) * SEQ)
            f = _dot(cs, y[rows]) - _dot(ss, z[rows])
            f_ref[rows, :] = (f * ((SEQ * GROUP_CH) ** -0.5)).astype(bf16)

    @pl.when(t >= NTP)
    def _():
        f = _dot(cl_ref[...].astype(bf16), y) - _dot(sl_ref[...].astype(bf16), z)
        f_ref[...] = (f * ((DEC_SEQ * GROUP_CH) ** -0.5)).astype(bf16)

    o_ref[...] = x_ref[...] + g1_ref[0] * _dot(f_ref[...], wf_ref[...].astype(bf16))


def _fourier(x, mod, norm_g, fourier_w, layer):
    li = layer // 2
    c_s, s_s = _dft_mats(SEQ)
    c_l, s_l = _dft_mats(DEC_SEQ)
    const = lambda shape: pl.BlockSpec(shape, lambda t: (0, 0))
    return pl.pallas_call(
        _fourier_kernel,
        grid=(NT,),
        in_specs=[pl.BlockSpec((TM, D), lambda t: (t, 0)),
                  pl.BlockSpec((1, 1, D), lambda t: (layer, 0, 0)),
                  _mod_spec(layer, 1), _mod_spec(layer, 0), _mod_spec(layer, 2),
                  const((SEQ, SEQ)), const((SEQ, SEQ)),
                  const((DEC_SEQ, DEC_SEQ)), const((DEC_SEQ, DEC_SEQ)),
                  pl.BlockSpec((None, D, D), lambda t: (li, 0, 0))],
        out_specs=pl.BlockSpec((TM, D), lambda t: (t, 0)),
        out_shape=jax.ShapeDtypeStruct((T, D), f32),
        scratch_shapes=[pltpu.VMEM((TM, D), bf16)],
        compiler_params=_cp(("arbitrary",)),
    )(x, norm_g, mod, mod, mod, jnp.asarray(c_s), jnp.asarray(s_s), jnp.asarray(c_l), jnp.asarray(s_l),
      fourier_w)


M_E0, M_E1, M_R0, M_R1, M_W0, M_W1 = range(6)
ROW_UNROLL = 8


def _router_kernel(x_ref, g_ref, sc_ref, sh_ref, rw_ref, rb_ref, h_ref, meta_ref, cnt_ref, carry_ref):
    t = pl.program_id(0)

    @pl.when(t == 0)
    def _():
        carry_ref[...] = jnp.zeros_like(carry_ref)

    h = _norm_mod(x_ref[...], g_ref[0], sc_ref[0], sh_ref[0])
    h_ref[...] = h
    logits = jnp.dot(h, rw_ref[...], preferred_element_type=f32,
                     precision=lax.Precision.HIGHEST) + rb_ref[...]
    lane = lax.broadcasted_iota(jnp.int32, (TM, LANES), 1)
    m1 = jnp.max(logits, axis=-1, keepdims=True)
    i1 = jnp.min(jnp.where(logits == m1, lane, LANES), axis=-1, keepdims=True)
    rest = jnp.where(lane == i1, -jnp.inf, logits)
    m2 = jnp.max(rest, axis=-1, keepdims=True)
    i2 = jnp.min(jnp.where(rest == m2, lane, LANES), axis=-1, keepdims=True)
    e = jnp.exp(m2 - m1)
    w0 = 1.0 / (1.0 + e)
    w1 = e / (1.0 + e)
    oh0 = (lane == i1).astype(f32)
    oh1 = (lane == i2).astype(f32)
    oh = oh0 + oh1
    row = lax.broadcasted_iota(jnp.int32, (TM, TM), 0)
    col = lax.broadcasted_iota(jnp.int32, (TM, TM), 1)
    before = jnp.where(col < row, 1.0, 0.0).astype(bf16)
    base = carry_ref[...] + _dot(before, oh.astype(bf16))
    r0 = jnp.sum(oh0 * base, axis=-1, keepdims=True)
    r1 = jnp.sum(oh1 * base, axis=-1, keepdims=True)
    carry_ref[...] += jnp.sum(oh, axis=0, keepdims=True)
    rec = jnp.zeros((TM, LANES), f32)
    for idx, val in ((M_E0, i1.astype(f32)), (M_E1, i2.astype(f32)), (M_R0, r0), (M_R1, r1),
                     (M_W0, w0), (M_W1, w1)):
        rec = jnp.where(lane == idx, val, rec)
    meta_ref[...] = rec
    cnt_ref[...] = carry_ref[...]


def _router(x, mod, norm_g, router_w, router_b, layer):
    li = layer // 2
    rw = jnp.pad(router_w[li], ((0, 0), (0, LANES - N_EXPERTS)))
    rb = jnp.pad(router_b[li], (0, LANES - N_EXPERTS), constant_values=NEG_INF).reshape(1, LANES)
    return pl.pallas_call(
        _router_kernel,
        grid=(NT,),
        in_specs=[pl.BlockSpec((TM, D), lambda t: (t, 0)),
                  pl.BlockSpec((1, 1, D), lambda t: (layer, 0, 0)),
                  _mod_spec(layer, 4), _mod_spec(layer, 3),
                  pl.BlockSpec((D, LANES), lambda t: (0, 0)),
                  pl.BlockSpec((1, LANES), lambda t: (0, 0))],
        out_specs=[pl.BlockSpec((TM, D), lambda t: (t, 0)),
                   pl.BlockSpec((TM, LANES), lambda t: (t, 0)),
                   pl.BlockSpec((1, LANES), lambda t: (0, 0))],
        out_shape=[jax.ShapeDtypeStruct((T, D), f32),
                   jax.ShapeDtypeStruct((T, LANES), f32),
                   jax.ShapeDtypeStruct((1, LANES), f32)],
        scratch_shapes=[pltpu.VMEM((1, LANES), f32)],
        compiler_params=_cp(("arbitrary",)),
    )(x, norm_g, mod, mod, rw, rb)


def _row_copy(src_ref, src_row, dst_ref, dst_row, sem):
    return pltpu.make_async_copy(src_ref.at[pl.ds(src_row, 1)], dst_ref.at[pl.ds(dst_row, 1)], sem)


def _dispatch_kernel(dest_ref, h_ref, xg_in_ref, xg_ref, sem):
    del xg_in_ref

    def issue(g, c):
        for u in range(ROW_UNROLL):
            r = g * ROW_UNROLL + u
            for k in range(2):
                _row_copy(h_ref, r, xg_ref, dest_ref[0, 0, 2 * r + k], sem).start()
        return c

    lax.fori_loop(0, TM // ROW_UNROLL, issue, 0)
    for k in range(2):
        pltpu.make_async_copy(h_ref, xg_ref.at[pl.ds(0, TM)], sem).wait()


def _dispatch(dest, h):
    xg0 = jnp.zeros((R_E, D), f32)
    return pl.pallas_call(
        _dispatch_kernel,
        grid=(NT,),
        in_specs=[pl.BlockSpec((1, 1, 2 * TM), lambda t: (t, 0, 0), memory_space=pltpu.SMEM),
                  pl.BlockSpec((TM, D), lambda t: (t, 0)),
                  pl.BlockSpec(memory_space=pl.ANY)],
        out_specs=pl.BlockSpec(memory_space=pl.ANY),
        out_shape=jax.ShapeDtypeStruct((R_E, D), f32),
        scratch_shapes=[pltpu.SemaphoreType.DMA(())],
        input_output_aliases={2: 0},
        compiler_params=_cp(("arbitrary",)),
    )(dest.reshape(NT, 1, 2 * TM), h, xg0)


def _expert_kernel(te_ref, nu_ref, xg_ref, w1_ref, w3_ref, w2_ref, y_ref, h_ref):
    t = pl.program_id(0)
    j = pl.program_id(1)
    used = t < nu_ref[0]

    @pl.when(j == 0)
    def _():
        h_ref[...] = xg_ref[...].astype(bf16)
        y_ref[...] = jnp.zeros_like(y_ref)

    @pl.when(used)
    def _():
        h = h_ref[...]
        gate = jax.nn.silu(_dot(h, w1_ref[...].astype(bf16))) * _dot(h, w3_ref[...].astype(bf16))
        y_ref[...] += _dot(gate.astype(bf16), w2_ref[...].astype(bf16))


def _experts(tile_expert, n_used, xg, w1, w3, w2, layer):
    li = layer // 2
    tf = TF_EXP
    nj = FFN_EXPERT // tf

    def jj(t, j, nu):
        return jnp.where(t < nu[0], j, nj - 1)

    def tt(t, nu):
        return jnp.minimum(t, nu[0] - 1)

    grid_spec = pltpu.PrefetchScalarGridSpec(
        num_scalar_prefetch=2,
        grid=(NT_E, nj),
        in_specs=[pl.BlockSpec((TM_E, D), lambda t, j, te, nu: (tt(t, nu), 0)),
                  pl.BlockSpec((None, None, D, tf), lambda t, j, te, nu: (li, te[t], 0, jj(t, j, nu))),
                  pl.BlockSpec((None, None, D, tf), lambda t, j, te, nu: (li, te[t], 0, jj(t, j, nu))),
                  pl.BlockSpec((None, None, tf, D), lambda t, j, te, nu: (li, te[t], jj(t, j, nu), 0))],
        out_specs=pl.BlockSpec((TM_E, D), lambda t, j, te, nu: (t, 0)),
        scratch_shapes=[pltpu.VMEM((TM_E, D), bf16)],
    )
    return pl.pallas_call(
        _expert_kernel,
        grid_spec=grid_spec,
        out_shape=jax.ShapeDtypeStruct((R_E, D), f32),
        compiler_params=_cp(("arbitrary", "arbitrary")),
    )(tile_expert, n_used, xg, w1, w3, w2)


def _combine_kernel(dest_ref, x_ref, g2_ref, meta_ref, y_ref, o_ref, buf_ref, sem):
    def issue(g, c):
        for u in range(ROW_UNROLL):
            r = g * ROW_UNROLL + u
            for k in range(2):
                _row_copy(y_ref, dest_ref[0, 0, 2 * r + k], buf_ref.at[k], r, sem).start()
        return c

    lax.fori_loop(0, TM // ROW_UNROLL, issue, 0)
    for k in range(2):
        pltpu.make_async_copy(y_ref.at[pl.ds(0, TM)], buf_ref.at[k], sem).wait()
    w0 = meta_ref[:, M_W0:M_W0 + 1]
    w1 = meta_ref[:, M_W1:M_W1 + 1]
    o_ref[...] = x_ref[...] + g2_ref[0] * (w0 * buf_ref[0] + w1 * buf_ref[1])


def _combine(dest, x, mod, meta, y, layer):
    return pl.pallas_call(
        _combine_kernel,
        grid=(NT,),
        in_specs=[pl.BlockSpec((1, 1, 2 * TM), lambda t: (t, 0, 0), memory_space=pltpu.SMEM),
                  pl.BlockSpec((TM, D), lambda t: (t, 0)),
                  _mod_spec(layer, 5),
                  pl.BlockSpec((TM, LANES), lambda t: (t, 0)),
                  pl.BlockSpec(memory_space=pl.ANY)],
        out_specs=pl.BlockSpec((TM, D), lambda t: (t, 0)),
        out_shape=jax.ShapeDtypeStruct((T, D), f32),
        scratch_shapes=[pltpu.VMEM((2, TM, D), f32), pltpu.SemaphoreType.DMA(())],
        compiler_params=_cp(("arbitrary",)),
    )(dest.reshape(NT, 1, 2 * TM), x, mod, meta, y)


def _moe(x, mod, norm_g, router_w, router_b, w1, w3, w2, layer):
    h_rows, meta, counts = _router(x, mod, norm_g, router_w, router_b, layer)
    cnt = counts[0, :N_EXPERTS].astype(jnp.int32)
    padded = ((cnt + TM_E - 1) // TM_E) * TM_E
    ends = jnp.cumsum(padded)
    starts = ends - padded
    experts = meta[:, M_E0:M_E1 + 1].astype(jnp.int32)
    ranks = meta[:, M_R0:M_R1 + 1].astype(jnp.int32)
    dest = starts[experts] + ranks
    n_used = (ends[-1] // TM_E).astype(jnp.int32).reshape(1)
    tile_start = jnp.minimum(jnp.arange(NT_E, dtype=jnp.int32), n_used[0] - 1) * TM_E
    tile_expert = jnp.sum((tile_start[:, None] >= ends[None, :]).astype(jnp.int32), axis=1)
    tile_expert = jnp.minimum(tile_expert, N_EXPERTS - 1).astype(jnp.int32)
    xg = _dispatch(dest, h_rows)
    y = _experts(tile_expert, n_used, xg, w1, w3, w2, layer)
    return _combine(dest, x, mod, meta, y, layer)


def _final_kernel(x_ref, g_ref, o_ref):
    x = x_ref[...]
    o_ref[...] = (x * lax.rsqrt(jnp.mean(x * x, axis=-1, keepdims=True) + RMS_EPS)) * g_ref[...]


def _final_norm(x, final_g, first_tile, n_tiles):
    return pl.pallas_call(
        _final_kernel,
        grid=(n_tiles,),
        in_specs=[pl.BlockSpec((TM, D), lambda t: (first_tile + t, 0)),
                  pl.BlockSpec((1, D), lambda t: (0, 0))],
        out_specs=pl.BlockSpec((TM, D), lambda t: (t, 0)),
        out_shape=jax.ShapeDtypeStruct((n_tiles * TM, D), f32),
        compiler_params=_cp(("arbitrary",)),
    )(x, final_g.reshape(1, D))


def kernel(x_prompt, x_sample, c, cache_k, cache_v, c_ctx, ada_w, ada_b, norm1_g, norm2_g, w_in, conv_w, rpb,
           w_out, ffn_w1, ffn_w3, ffn_w2, fourier_w, router_w, router_b, moe_w1, moe_w3, moe_w2, final_g):
    cvecs = jnp.concatenate([c_ctx[None, :], c, jnp.zeros((MOD_ROWS - 1 - DEC_BATCH, D), f32)], axis=0)
    mod = _modulation(cvecs, ada_w, ada_b).reshape(DEPTH * MOD_ROWS * 6, 1, D)
    n1 = norm1_g.reshape(DEPTH, 1, D)
    n2 = norm2_g.reshape(DEPTH, 1, D)
    bias = _nbr_bias(rpb)
    x = jnp.concatenate([x_prompt.reshape(T_P, D), x_sample.reshape(T_S, D)], axis=0)
    new_k, new_v = [], []
    for layer in range(DEPTH):
        li = layer // 2
        if layer % 2 == 0:
            proj, kc, vc = _inproj(x, mod, n1, w_in, layer)
            new_k.append(kc)
            new_v.append(vc)
            yb_p = _ctx_attention(proj)
            yb_s = _nbr_attention(proj, cache_k, cache_v, bias, li)
            x = _mixout(proj, yb_p, yb_s, x, mod, conv_w, w_out, layer)
            x = _dense_ffn(x, mod, n2, ffn_w1, ffn_w3, ffn_w2, layer)
        else:
            x = _fourier(x, mod, n1, fourier_w, layer)
            x = _moe(x, mod, n2, router_w, router_b, moe_w1, moe_w3, moe_w2, layer)
    y_prompt = _final_norm(x, final_g, 0, NTP).reshape(BATCH, SEQ, D)
    y_sample = _final_norm(x, final_g, NTP, NT - NTP).reshape(DEC_BATCH, DEC_SEQ, D)
    return (y_prompt, y_sample, jnp.stack(new_k, axis=1), jnp.stack(new_v, axis=1))
```

```python
import functools

import numpy as np
import jax
import jax.numpy as jnp
from jax import lax
from jax.experimental import pallas as pl
from jax.experimental.pallas import tpu as pltpu

f32 = jnp.float32
bf16 = jnp.bfloat16

D = 1024
BATCH = 16
SEQ = 256
DEPTH = 4
DEC_BATCH = 8
DEC_SEQ = 1024
PAST_LEN = 512
GRID_W = 64
CONV_CH = 512
NA_HEADS = 8
HEAD_DIM = 64
NA_WIDTH = 512
WIN_ROWS = 8
WIN_COLS = 16
IN_WIDTH = 3072
FOURIER_GROUPS = 4
FFN_DENSE = 2816
N_EXPERTS = 8
FFN_EXPERT = 3584
N_EVEN = 2
RMS_EPS = 1e-6
NEG_INF = -1e30
SCALE = HEAD_DIM ** -0.5

LANES = 128
TM = 1024
T_P = BATCH * SEQ
T_S = DEC_BATCH * DEC_SEQ
T = T_P + T_S
NT = T // TM
NTP = T_P // TM
MOD_ROWS = 16
GRID_ROWS = DEC_SEQ // GRID_W
CHUNK_ROWS = 4
N_CHUNKS = GRID_ROWS // CHUNK_ROWS
CHUNK_Q = CHUNK_ROWS * GRID_W
KWIN_ROWS = 12
KWIN = KWIN_ROWS * GRID_W
KB0 = (0, 0, 4, 4)
CHUNKS_PER_GROUP = 4
TF_DENSE = 256
TF_EXP = 512
TM_E = 1024
N_ASSIGN = 2 * T
NT_E = N_ASSIGN // TM_E + N_EXPERTS
R_E = NT_E * TM_E
VMEM_LIMIT = 56 * 1024 * 1024


def _cp(sem, vmem=VMEM_LIMIT):
    return pltpu.CompilerParams(dimension_semantics=sem, vmem_limit_bytes=vmem)


def _mod_index(t):
    return jnp.where(t < NTP, 0, t - (NTP - 1))


def _mod_spec(layer, part):
    def index(t, *_):
        return ((layer * MOD_ROWS + _mod_index(t)) * 6 + part, 0, 0)
    return pl.BlockSpec((1, 1, D), index)


def _norm_mod(x, g, sc, sh):
    y = x * lax.rsqrt(jnp.mean(x * x, axis=-1, keepdims=True) + RMS_EPS)
    return (y * g) * (1 + sc) + sh


def _dot(a, b):
    return jnp.dot(a, b, preferred_element_type=f32)


def _dot_nt(a, b):
    return lax.dot_general(a, b, (((1,), (1,)), ((), ())), preferred_element_type=f32)


def _mod_kernel(cv_ref, w_ref, b_ref, o_ref):
    s = jax.nn.silu(cv_ref[...]).astype(bf16)
    o_ref[0] = _dot(s, w_ref[0].astype(bf16)) + b_ref[0]


def _modulation(cvecs, ada_w, ada_b):
    tn = 1536
    return pl.pallas_call(
        _mod_kernel,
        grid=(DEPTH, 6 * D // tn),
        in_specs=[pl.BlockSpec((MOD_ROWS, D), lambda l, j: (0, 0)),
                  pl.BlockSpec((1, D, tn), lambda l, j: (l, 0, j)),
                  pl.BlockSpec((1, 1, tn), lambda l, j: (l, 0, j))],
        out_specs=pl.BlockSpec((1, MOD_ROWS, tn), lambda l, j: (l, 0, j)),
        out_shape=jax.ShapeDtypeStruct((DEPTH, MOD_ROWS, 6 * D), f32),
        compiler_params=_cp(("arbitrary", "arbitrary")),
    )(cvecs, ada_w, ada_b.reshape(DEPTH, 1, 6 * D))


TN_IN = IN_WIDTH // 2


def _inproj_kernel(x_ref, g_ref, sc_ref, sh_ref, w_ref, proj_ref, kc_ref, vc_ref, h_ref):
    i = pl.program_id(0)
    j = pl.program_id(1)

    @pl.when(j == 0)
    def _():
        h_ref[...] = _norm_mod(x_ref[...], g_ref[0], sc_ref[0], sh_ref[0]).astype(bf16)

    acc = _dot(h_ref[...], w_ref[...].astype(bf16))
    proj_ref[...] = acc.astype(bf16)

    @pl.when((j == 1) & (i < NTP))
    def _():
        for ref, col0 in ((kc_ref, NA_WIDTH), (vc_ref, 2 * NA_WIDTH)):
            for s in range(TM // SEQ):
                for hd in range(NA_HEADS):
                    c0 = col0 + hd * HEAD_DIM
                    ref[s, hd] = acc[s * SEQ:(s + 1) * SEQ, c0:c0 + HEAD_DIM]


def _inproj(x, mod, norm_g, w_in, layer):
    li = layer // 2
    tn = TN_IN
    seqs = TM // SEQ
    cache_spec = pl.BlockSpec((seqs, NA_HEADS, SEQ, HEAD_DIM),
                              lambda i, j: (jnp.minimum(i, NTP - 1), 0, 0, 0))
    cache_shape = jax.ShapeDtypeStruct((BATCH, NA_HEADS, SEQ, HEAD_DIM), f32)
    return pl.pallas_call(
        _inproj_kernel,
        grid=(NT, IN_WIDTH // tn),
        in_specs=[pl.BlockSpec((TM, D), lambda i, j: (i, 0)),
                  pl.BlockSpec((1, 1, D), lambda i, j: (layer, 0, 0)),
                  _mod_spec(layer, 1), _mod_spec(layer, 0),
                  pl.BlockSpec((None, D, tn), lambda i, j: (li, 0, j))],
        out_specs=[pl.BlockSpec((TM, tn), lambda i, j: (i, j)), cache_spec, cache_spec],
        out_shape=[jax.ShapeDtypeStruct((T, IN_WIDTH), bf16), cache_shape, cache_shape],
        scratch_shapes=[pltpu.VMEM((TM, D), bf16)],
        compiler_params=_cp(("arbitrary", "arbitrary")),
    )(x, norm_g, mod, mod, w_in)


HEADS_PER_STEP = LANES // HEAD_DIM


def _pair_attention(problems):
    lane = lax.broadcasted_iota(jnp.int32, (1, LANES), 1)
    first = lane < HEAD_DIM
    mine = (first, jnp.logical_not(first))
    scores = []
    for q, k_parts, _, bias_parts in problems:
        q = q * SCALE
        for hh in range(HEADS_PER_STEP):
            qh = jnp.where(mine[hh], q, 0)
            ss = []
            for k, b in zip(k_parts, bias_parts):
                s = _dot_nt(qh, k)
                ss.append(s if b is None else s + b[hh])
            scores.append(ss)
    probs = []
    for ss in scores:
        m = functools.reduce(jnp.maximum, [jnp.max(s, axis=-1, keepdims=True) for s in ss])
        probs.append([jnp.exp((s - m).astype(bf16)) for s in ss])
    results = []
    for n, (_, _, v_parts, _) in enumerate(problems):
        outs = []
        for hh in range(HEADS_PER_STEP):
            pv = functools.reduce(jnp.add, [_dot(p, jnp.where(mine[hh], v, 1))
                                            for p, v in zip(probs[n * HEADS_PER_STEP + hh], v_parts)])
            denom = pv[:, HEAD_DIM:HEAD_DIM + 1] if hh == 0 else pv[:, 0:1]
            outs.append(pv / denom)
        results.append(jnp.where(first, outs[0], outs[1]))
    return results


def _ctx_attn_kernel(q_ref, k_ref, v_ref, o_ref):
    cols = [slice(hp * LANES, (hp + 1) * LANES) for hp in range(NA_HEADS // HEADS_PER_STEP)]
    outs = _pair_attention([(q_ref[:, sl], [k_ref[:, sl]], [v_ref[:, sl]], [None]) for sl in cols])
    for sl, o in zip(cols, outs):
        o_ref[:, sl] = o.astype(bf16)


def _ctx_attention(proj):
    col = IN_WIDTH // NA_WIDTH - 3
    return pl.pallas_call(
        _ctx_attn_kernel,
        grid=(BATCH,),
        in_specs=[pl.BlockSpec((SEQ, NA_WIDTH), lambda b: (b, col)),
                  pl.BlockSpec((SEQ, NA_WIDTH), lambda b: (b, col + 1)),
                  pl.BlockSpec((SEQ, NA_WIDTH), lambda b: (b, col + 2))],
        out_specs=pl.BlockSpec((SEQ, NA_WIDTH), lambda b: (b, 0)),
        out_shape=jax.ShapeDtypeStruct((T_P, NA_WIDTH), bf16),
        compiler_params=_cp(("arbitrary",)),
    )(proj, proj, proj)


def _win_start(qr):
    return min(max(qr - WIN_ROWS // 2, 0), GRID_ROWS - WIN_ROWS)


N_ROFF = 2 * WIN_ROWS - 1
N_COFF = 2 * WIN_COLS - 1


def _bias_kernel(rpb_ref, o_ref):
    base = (pl.program_id(0) * NA_HEADS + pl.program_id(1)) * (N_ROFF * N_COFF)
    qc = lax.broadcasted_iota(jnp.int32, (GRID_W, GRID_W), 0)
    kc = lax.broadcasted_iota(jnp.int32, (GRID_W, GRID_W), 1)
    coff = kc - qc + (WIN_COLS - 1)
    cs = jnp.clip(qc - WIN_COLS // 2, 0, GRID_W - WIN_COLS)
    valid = (kc >= cs) & (kc < cs + WIN_COLS)
    neg = jnp.full((GRID_W, GRID_W), NEG_INF, f32)
    blocks = []
    for a in range(N_ROFF):
        t = jnp.zeros((GRID_W, GRID_W), f32)
        for b in range(N_COFF):
            t = jnp.where(coff == b, rpb_ref[base + a * N_COFF + b], t)
        blocks.append(jnp.where(valid, t, neg))
    for c in range(N_CHUNKS):
        for ql in range(CHUNK_ROWS):
            qr = c * CHUNK_ROWS + ql
            st = _win_start(qr)
            for kl in range(KWIN_ROWS):
                kr = KB0[c] + kl
                blk = blocks[kr - qr + WIN_ROWS - 1] if st <= kr < st + WIN_ROWS else neg
                o_ref[c, ql * GRID_W:(ql + 1) * GRID_W, kl * GRID_W:(kl + 1) * GRID_W] = blk


def _nbr_bias(rpb):
    return pl.pallas_call(
        _bias_kernel,
        grid=(N_EVEN, NA_HEADS),
        in_specs=[pl.BlockSpec(memory_space=pltpu.SMEM)],
        out_specs=pl.BlockSpec((None, None, N_CHUNKS, CHUNK_Q, KWIN), lambda i, h: (i, h, 0, 0, 0)),
        out_shape=jax.ShapeDtypeStruct((N_EVEN, NA_HEADS, N_CHUNKS, CHUNK_Q, KWIN), f32),
        compiler_params=_cp(("arbitrary", "arbitrary")),
    )(rpb.reshape(-1))


def _nbr_attn_kernel(q_ref, k_ref, v_ref, kc_ref, vc_ref, bias_ref, o_ref):
    kctx = jnp.concatenate([kc_ref[0], kc_ref[1]], axis=-1).astype(bf16)
    vctx = jnp.concatenate([vc_ref[0], vc_ref[1]], axis=-1).astype(bf16)
    for c0 in range(0, N_CHUNKS, CHUNKS_PER_GROUP):
        problems = []
        for c in range(c0, c0 + CHUNKS_PER_GROUP):
            rows = slice(c * CHUNK_Q, (c + 1) * CHUNK_Q)
            win = slice(KB0[c] * GRID_W, KB0[c] * GRID_W + KWIN)
            bias = [bias_ref[hh, c] for hh in range(HEADS_PER_STEP)]
            problems.append((q_ref[rows, :], [k_ref[win, :], kctx], [v_ref[win, :], vctx], [bias, None]))
        for c, o in zip(range(c0, c0 + CHUNKS_PER_GROUP), _pair_attention(problems)):
            o_ref[c * CHUNK_Q:(c + 1) * CHUNK_Q, :] = o.astype(bf16)


def _nbr_attention(proj, cache_k, cache_v, bias, li):
    qcol = 3 * CONV_CH // LANES
    ncol = NA_WIDTH // LANES
    hp_steps = NA_HEADS // HEADS_PER_STEP

    def col_spec(which):
        return pl.BlockSpec((DEC_SEQ, LANES), lambda hp, b: (NTP + b, qcol + which * ncol + hp))

    ctx_spec = pl.BlockSpec((None, None, HEADS_PER_STEP, PAST_LEN, HEAD_DIM),
                            lambda hp, b: (b, li, hp, 0, 0))
    return pl.pallas_call(
        _nbr_attn_kernel,
        grid=(hp_steps, DEC_BATCH),
        in_specs=[col_spec(0), col_spec(1), col_spec(2), ctx_spec, ctx_spec,
                  pl.BlockSpec((None, HEADS_PER_STEP, N_CHUNKS, CHUNK_Q, KWIN),
                               lambda hp, b: (li, hp, 0, 0, 0))],
        out_specs=pl.BlockSpec((DEC_SEQ, LANES), lambda hp, b: (b, hp)),
        out_shape=jax.ShapeDtypeStruct((T_S, NA_WIDTH), bf16),
        compiler_params=_cp(("arbitrary", "arbitrary")),
    )(proj, proj, proj, cache_k, cache_v, bias)


def _mixout_kernel(a_ref, ybp_ref, ybs_ref, x_ref, g1_ref, cw_ref, w_ref, o_ref):
    t = pl.program_id(0)
    a_b = a_ref[:, 0:CONV_CH].astype(f32)
    a_c = a_ref[:, CONV_CH:2 * CONV_CH].astype(f32)
    a_x = a_ref[:, 2 * CONV_CH:3 * CONV_CH].astype(f32)
    u = a_c * a_x
    r = lax.broadcasted_iota(jnp.int32, (TM, 1), 0)
    pos = jnp.where(t < NTP, r % SEQ, r)
    last = jnp.where(t < NTP, SEQ - 1, DEC_SEQ - 1)
    u_prev = jnp.where(pos == 0, 0.0, pltpu.roll(u, 1, axis=0))
    u_next = jnp.where(pos == last, 0.0, pltpu.roll(u, TM - 1, axis=0))
    y_a = a_b * (u_prev * cw_ref[0:1, :] + u * cw_ref[1:2, :] + u_next * cw_ref[2:3, :])
    y_b = jnp.where(t < NTP, ybp_ref[...], ybs_ref[...])
    y = (_dot(y_a.astype(bf16), w_ref[0:CONV_CH, :].astype(bf16))
         + _dot(y_b, w_ref[CONV_CH:, :].astype(bf16)))
    o_ref[...] = x_ref[...] + g1_ref[0] * y


def _mixout(proj, yb_p, yb_s, x, mod, conv_w, w_out, layer):
    li = layer // 2
    return pl.pallas_call(
        _mixout_kernel,
        grid=(NT,),
        in_specs=[pl.BlockSpec((TM, 3 * CONV_CH), lambda t: (t, 0)),
                  pl.BlockSpec((TM, NA_WIDTH), lambda t: (jnp.minimum(t, NTP - 1), 0)),
                  pl.BlockSpec((TM, NA_WIDTH), lambda t: (jnp.maximum(t - NTP, 0), 0)),
                  pl.BlockSpec((TM, D), lambda t: (t, 0)),
                  _mod_spec(layer, 2),
                  pl.BlockSpec((None, 3, CONV_CH), lambda t: (li, 0, 0)),
                  pl.BlockSpec((None, D, D), lambda t: (li, 0, 0))],
        out_specs=pl.BlockSpec((TM, D), lambda t: (t, 0)),
        out_shape=jax.ShapeDtypeStruct((T, D), f32),
        compiler_params=_cp(("arbitrary",)),
    )(proj, yb_p, yb_s, x, mod, conv_w, w_out)


def _ffn_kernel(x_ref, g_ref, sc_ref, sh_ref, g2_ref, w1_ref, w3_ref, w2_ref, o_ref, h_ref, acc_ref):
    j = pl.program_id(1)

    @pl.when(j == 0)
    def _():
        h_ref[...] = _norm_mod(x_ref[...], g_ref[0], sc_ref[0], sh_ref[0]).astype(bf16)
        acc_ref[...] = jnp.zeros_like(acc_ref)

    h = h_ref[...]
    gate = jax.nn.silu(_dot(h, w1_ref[...].astype(bf16))) * _dot(h, w3_ref[...].astype(bf16))
    acc_ref[...] += _dot(gate.astype(bf16), w2_ref[...].astype(bf16))

    @pl.when(j == pl.num_programs(1) - 1)
    def _():
        o_ref[...] = x_ref[...] + g2_ref[0] * acc_ref[...]


def _dense_ffn(x, mod, norm_g, w1, w3, w2, layer):
    li = layer // 2
    tf = TF_DENSE
    return pl.pallas_call(
        _ffn_kernel,
        grid=(NT, FFN_DENSE // tf),
        in_specs=[pl.BlockSpec((TM, D), lambda i, j: (i, 0)),
                  pl.BlockSpec((1, 1, D), lambda i, j: (layer, 0, 0)),
                  _mod_spec(layer, 4), _mod_spec(layer, 3), _mod_spec(layer, 5),
                  pl.BlockSpec((None, D, tf), lambda i, j: (li, 0, j)),
                  pl.BlockSpec((None, D, tf), lambda i, j: (li, 0, j)),
                  pl.BlockSpec((None, tf, D), lambda i, j: (li, j, 0))],
        out_specs=pl.BlockSpec((TM, D), lambda i, j: (i, 0)),
        out_shape=jax.ShapeDtypeStruct((T, D), f32),
        scratch_shapes=[pltpu.VMEM((TM, D), bf16), pltpu.VMEM((TM, D), f32)],
        compiler_params=_cp(("arbitrary", "arbitrary")),
    )(x, norm_g, mod, mod, mod, w1, w3, w2)


GROUP_CH = D // FOURIER_GROUPS


def _dft_mats(n):
    k = np.arange(n, dtype=np.int64)
    ang = 2.0 * np.pi * ((k[:, None] * k[None, :]) % n).astype(np.float64) / n
    return np.cos(ang).astype(np.float32), np.sin(ang).astype(np.float32)


def _fourier_kernel(x_ref, g_ref, sc_ref, sh_ref, g1_ref, cs_ref, ss_ref, cl_ref, sl_ref, wf_ref,
                    o_ref, f_ref):
    t = pl.program_id(0)
    h = _norm_mod(x_ref[...], g_ref[0], sc_ref[0], sh_ref[0]).astype(bf16)
    cs = cs_ref[...].astype(bf16)
    ss = ss_ref[...].astype(bf16)
    ys, zs = [], []
    for g in range(FOURIER_GROUPS):
        hg = h[:, g * GROUP_CH:(g + 1) * GROUP_CH]
        ys.append(_dot(hg, cs))
        zs.append(_dot(hg, ss))
    y = jnp.concatenate(ys, axis=-1).astype(bf16)
    z = jnp.concatenate(zs, axis=-1).astype(bf16)

    @pl.when(t < NTP)
    def _():
        for s in range(TM // SEQ):
            rows = slice(s * SEQ, (s + 1) * SEQ)
            f = _dot(cs, y[rows]) - _dot(ss, z[rows])
            f_ref[rows, :] = (f * ((SEQ * GROUP_CH) ** -0.5)).astype(bf16)

    @pl.when(t >= NTP)
    def _():
        f = _dot(cl_ref[...].astype(bf16), y) - _dot(sl_ref[...].astype(bf16), z)
        f_ref[...] = (f * ((DEC_SEQ * GROUP_CH) ** -0.5)).astype(bf16)

    o_ref[...] = x_ref[...] + g1_ref[0] * _dot(f_ref[...], wf_ref[...].astype(bf16))


def _fourier(x, mod, norm_g, fourier_w, layer):
    li = layer // 2
    c_s, s_s = _dft_mats(SEQ)
    c_l, s_l = _dft_mats(DEC_SEQ)
    const = lambda shape: pl.BlockSpec(shape, lambda t: (0, 0))
    return pl.pallas_call(
        _fourier_kernel,
        grid=(NT,),
        in_specs=[pl.BlockSpec((TM, D), lambda t: (t, 0)),
                  pl.BlockSpec((1, 1, D), lambda t: (layer, 0, 0)),
                  _mod_spec(layer, 1), _mod_spec(layer, 0), _mod_spec(layer, 2),
                  const((SEQ, SEQ)), const((SEQ, SEQ)),
                  const((DEC_SEQ, DEC_SEQ)), const((DEC_SEQ, DEC_SEQ)),
                  pl.BlockSpec((None, D, D), lambda t: (li, 0, 0))],
        out_specs=pl.BlockSpec((TM, D), lambda t: (t, 0)),
        out_shape=jax.ShapeDtypeStruct((T, D), f32),
        scratch_shapes=[pltpu.VMEM((TM, D), bf16)],
        compiler_params=_cp(("arbitrary",)),
    )(x, norm_g, mod, mod, mod, jnp.asarray(c_s), jnp.asarray(s_s), jnp.asarray(c_l), jnp.asarray(s_l),
      fourier_w)


M_E0, M_E1, M_R0, M_R1, M_W0, M_W1 = range(6)
ROW_UNROLL = 8


def _router_kernel(x_ref, g_ref, sc_ref, sh_ref, rw_ref, rb_ref, h_ref, meta_ref, cnt_ref, carry_ref):
    t = pl.program_id(0)

    @pl.when(t == 0)
    def _():
        carry_ref[...] = jnp.zeros_like(carry_ref)

    h = _norm_mod(x_ref[...], g_ref[0], sc_ref[0], sh_ref[0])
    h_ref[...] = h
    logits = jnp.dot(h, rw_ref[...], preferred_element_type=f32,
                     precision=lax.Precision.HIGHEST) + rb_ref[...]
    lane = lax.broadcasted_iota(jnp.int32, (TM, LANES), 1)
    m1 = jnp.max(logits, axis=-1, keepdims=True)
    i1 = jnp.min(jnp.where(logits == m1, lane, LANES), axis=-1, keepdims=True)
    rest = jnp.where(lane == i1, -jnp.inf, logits)
    m2 = jnp.max(rest, axis=-1, keepdims=True)
    i2 = jnp.min(jnp.where(rest == m2, lane, LANES), axis=-1, keepdims=True)
    e = jnp.exp(m2 - m1)
    w0 = 1.0 / (1.0 + e)
    w1 = e / (1.0 + e)
    oh0 = (lane == i1).astype(f32)
    oh1 = (lane == i2).astype(f32)
    oh = oh0 + oh1
    row = lax.broadcasted_iota(jnp.int32, (TM, TM), 0)
    col = lax.broadcasted_iota(jnp.int32, (TM, TM), 1)
    before = jnp.where(col < row, 1.0, 0.0).astype(bf16)
    base = carry_ref[...] + _dot(before, oh.astype(bf16))
    r0 = jnp.sum(oh0 * base, axis=-1, keepdims=True)
    r1 = jnp.sum(oh1 * base, axis=-1, keepdims=True)
    carry_ref[...] += jnp.sum(oh, axis=0, keepdims=True)
    rec = jnp.zeros((TM, LANES), f32)
    for idx, val in ((M_E0, i1.astype(f32)), (M_E1, i2.astype(f32)), (M_R0, r0), (M_R1, r1),
                     (M_W0, w0), (M_W1, w1)):
        rec = jnp.where(lane == idx, val, rec)
    meta_ref[...] = rec
    cnt_ref[...] = carry_ref[...]


def _router(x, mod, norm_g, router_w, router_b, layer):
    li = layer // 2
    rw = jnp.pad(router_w[li], ((0, 0), (0, LANES - N_EXPERTS)))
    rb = jnp.pad(router_b[li], (0, LANES - N_EXPERTS), constant_values=NEG_INF).reshape(1, LANES)
    return pl.pallas_call(
        _router_kernel,
        grid=(NT,),
        in_specs=[pl.BlockSpec((TM, D), lambda t: (t, 0)),
                  pl.BlockSpec((1, 1, D), lambda t: (layer, 0, 0)),
                  _mod_spec(layer, 4), _mod_spec(layer, 3),
                  pl.BlockSpec((D, LANES), lambda t: (0, 0)),
                  pl.BlockSpec((1, LANES), lambda t: (0, 0))],
        out_specs=[pl.BlockSpec((TM, D), lambda t: (t, 0)),
                   pl.BlockSpec((TM, LANES), lambda t: (t, 0)),
                   pl.BlockSpec((1, LANES), lambda t: (0, 0))],
        out_shape=[jax.ShapeDtypeStruct((T, D), f32),
                   jax.ShapeDtypeStruct((T, LANES), f32),
                   jax.ShapeDtypeStruct((1, LANES), f32)],
        scratch_shapes=[pltpu.VMEM((1, LANES), f32)],
        compiler_params=_cp(("arbitrary",)),
    )(x, norm_g, mod, mod, rw, rb)


def _row_copy(src_ref, src_row, dst_ref, dst_row, sem):
    return pltpu.make_async_copy(src_ref.at[pl.ds(src_row, 1)], dst_ref.at[pl.ds(dst_row, 1)], sem)


NJ_EXP = FFN_EXPERT // TF_EXP
GATHER_PER_STEP = -(-TM_E // NJ_EXP)


def _expert_kernel(te_ref, nu_ref, src_ref, h_hbm, w1_ref, w3_ref, w2_ref, y_ref, xbuf_ref, h_ref, sems):
    t = pl.program_id(0)
    j = pl.program_id(1)
    n_used = nu_ref[0]
    used = t < n_used
    slot = t % 2

    def gather_row(tile, row, buf_slot):
        token = src_ref[jnp.minimum(tile * TM_E + row, R_E - 1)]
        return _row_copy(h_hbm, token, xbuf_ref.at[buf_slot], row, sems.at[buf_slot])

    @pl.when((t == 0) & (j == 0))
    def _():
        def issue(g, c):
            for u in range(ROW_UNROLL):
                gather_row(0, g * ROW_UNROLL + u, 0).start()
            return c
        lax.fori_loop(0, TM_E // ROW_UNROLL, issue, 0)

    @pl.when(used & (j == 0))
    def _():
        pltpu.make_async_copy(h_hbm.at[pl.ds(0, TM_E)], xbuf_ref.at[slot], sems.at[slot]).wait()
        h_ref[...] = xbuf_ref[slot].astype(bf16)

    @pl.when(j == 0)
    def _():
        y_ref[...] = jnp.zeros_like(y_ref)

    @pl.when(used)
    def _():
        h = h_ref[...]
        gate = jax.nn.silu(_dot(h, w1_ref[...].astype(bf16))) * _dot(h, w3_ref[...].astype(bf16))
        y_ref[...] += _dot(gate.astype(bf16), w2_ref[...].astype(bf16))
        for i in range(GATHER_PER_STEP):
            row = j * GATHER_PER_STEP + i

            @pl.when((row < TM_E) & (t + 1 < n_used))
            def _():
                gather_row(t + 1, row, 1 - slot).start()


def _experts(tile_expert, n_used, row_token, h_rows, w1, w3, w2, layer):
    li = layer // 2
    tf = TF_EXP
    nj = NJ_EXP

    def jj(t, j, nu):
        return jnp.where(t < nu[0], j, nj - 1)

    grid_spec = pltpu.PrefetchScalarGridSpec(
        num_scalar_prefetch=3,
        grid=(NT_E, nj),
        in_specs=[pl.BlockSpec(memory_space=pl.ANY),
                  pl.BlockSpec((None, None, D, tf), lambda t, j, te, nu, src: (li, te[t], 0, jj(t, j, nu))),
                  pl.BlockSpec((None, None, D, tf), lambda t, j, te, nu, src: (li, te[t], 0, jj(t, j, nu))),
                  pl.BlockSpec((None, None, tf, D), lambda t, j, te, nu, src: (li, te[t], jj(t, j, nu), 0))],
        out_specs=pl.BlockSpec((TM_E, D), lambda t, j, te, nu, src: (t, 0)),
        scratch_shapes=[pltpu.VMEM((2, TM_E, D), f32), pltpu.VMEM((TM_E, D), bf16),
                        pltpu.SemaphoreType.DMA((2,))],
    )
    return pl.pallas_call(
        _expert_kernel,
        grid_spec=grid_spec,
        out_shape=jax.ShapeDtypeStruct((R_E, D), f32),
        compiler_params=_cp(("arbitrary", "arbitrary")),
    )(tile_expert, n_used, row_token, h_rows, w1, w3, w2)


def _combine_kernel(dest_ref, x_ref, g2_ref, meta_ref, y_ref, o_ref, buf_ref, sem):
    def issue(g, c):
        for u in range(ROW_UNROLL):
            r = g * ROW_UNROLL + u
            for k in range(2):
                _row_copy(y_ref, dest_ref[0, 0, 2 * r + k], buf_ref.at[k], r, sem).start()
        return c

    lax.fori_loop(0, TM // ROW_UNROLL, issue, 0)
    for k in range(2):
        pltpu.make_async_copy(y_ref.at[pl.ds(0, TM)], buf_ref.at[k], sem).wait()
    w0 = meta_ref[:, M_W0:M_W0 + 1]
    w1 = meta_ref[:, M_W1:M_W1 + 1]
    o_ref[...] = x_ref[...] + g2_ref[0] * (w0 * buf_ref[0] + w1 * buf_ref[1])


def _combine(dest, x, mod, meta, y, layer):
    return pl.pallas_call(
        _combine_kernel,
        grid=(NT,),
        in_specs=[pl.BlockSpec((1, 1, 2 * TM), lambda t: (t, 0, 0), memory_space=pltpu.SMEM),
                  pl.BlockSpec((TM, D), lambda t: (t, 0)),
                  _mod_spec(layer, 5),
                  pl.BlockSpec((TM, LANES), lambda t: (t, 0)),
                  pl.BlockSpec(memory_space=pl.ANY)],
        out_specs=pl.BlockSpec((TM, D), lambda t: (t, 0)),
        out_shape=jax.ShapeDtypeStruct((T, D), f32),
        scratch_shapes=[pltpu.VMEM((2, TM, D), f32), pltpu.SemaphoreType.DMA(())],
        compiler_params=_cp(("arbitrary",)),
    )(dest.reshape(NT, 1, 2 * TM), x, mod, meta, y)


def _moe(x, mod, norm_g, router_w, router_b, w1, w3, w2, layer):
    h_rows, meta, counts = _router(x, mod, norm_g, router_w, router_b, layer)
    cnt = counts[0, :N_EXPERTS].astype(jnp.int32)
    padded = ((cnt + TM_E - 1) // TM_E) * TM_E
    ends = jnp.cumsum(padded)
    starts = ends - padded
    experts = meta[:, M_E0:M_E1 + 1].astype(jnp.int32)
    ranks = meta[:, M_R0:M_R1 + 1].astype(jnp.int32)
    dest = starts[experts] + ranks
    n_used = (ends[-1] // TM_E).astype(jnp.int32).reshape(1)
    tile_start = jnp.minimum(jnp.arange(NT_E, dtype=jnp.int32), n_used[0] - 1) * TM_E
    tile_expert = jnp.sum((tile_start[:, None] >= ends[None, :]).astype(jnp.int32), axis=1)
    tile_expert = jnp.minimum(tile_expert, N_EXPERTS - 1).astype(jnp.int32)
    token_ids = jnp.repeat(jnp.arange(T, dtype=jnp.int32), 2)
    row_token = jnp.zeros((R_E,), jnp.int32).at[dest.reshape(-1)].set(token_ids, unique_indices=True)
    y = _experts(tile_expert, n_used, row_token, h_rows, w1, w3, w2, layer)
    return _combine(dest, x, mod, meta, y, layer)


def _final_kernel(x_ref, g_ref, o_ref):
    x = x_ref[...]
    o_ref[...] = (x * lax.rsqrt(jnp.mean(x * x, axis=-1, keepdims=True) + RMS_EPS)) * g_ref[...]


def _final_norm(x, final_g, first_tile, n_tiles):
    return pl.pallas_call(
        _final_kernel,
        grid=(n_tiles,),
        in_specs=[pl.BlockSpec((TM, D), lambda t: (first_tile + t, 0)),
                  pl.BlockSpec((1, D), lambda t: (0, 0))],
        out_specs=pl.BlockSpec((TM, D), lambda t: (t, 0)),
        out_shape=jax.ShapeDtypeStruct((n_tiles * TM, D), f32),
        compiler_params=_cp(("arbitrary",)),
    )(x, final_g.reshape(1, D))


def kernel(x_prompt, x_sample, c, cache_k, cache_v, c_ctx, ada_w, ada_b, norm1_g, norm2_g, w_in, conv_w, rpb,
           w_out, ffn_w1, ffn_w3, ffn_w2, fourier_w, router_w, router_b, moe_w1, moe_w3, moe_w2, final_g):
    cvecs = jnp.concatenate([c_ctx[None, :], c, jnp.zeros((MOD_ROWS - 1 - DEC_BATCH, D), f32)], axis=0)
    mod = _modulation(cvecs, ada_w, ada_b).reshape(DEPTH * MOD_ROWS * 6, 1, D)
    n1 = norm1_g.reshape(DEPTH, 1, D)
    n2 = norm2_g.reshape(DEPTH, 1, D)
    bias = _nbr_bias(rpb)
    x = jnp.concatenate([x_prompt.reshape(T_P, D), x_sample.reshape(T_S, D)], axis=0)
    new_k, new_v = [], []
    for layer in range(DEPTH):
        li = layer // 2
        if layer % 2 == 0:
            proj, kc, vc = _inproj(x, mod, n1, w_in, layer)
            new_k.append(kc)
            new_v.append(vc)
            yb_p = _ctx_attention(proj)
            yb_s = _nbr_attention(proj, cache_k, cache_v, bias, li)
            x = _mixout(proj, yb_p, yb_s, x, mod, conv_w, w_out, layer)
            x = _dense_ffn(x, mod, n2, ffn_w1, ffn_w3, ffn_w2, layer)
        else:
            x = _fourier(x, mod, n1, fourier_w, layer)
            x = _moe(x, mod, n2, router_w, router_b, moe_w1, moe_w3, moe_w2, layer)
    y_prompt = _final_norm(x, final_g, 0, NTP).reshape(BATCH, SEQ, D)
    y_sample = _final_norm(x, final_g, NTP, NT - NTP).reshape(DEC_BATCH, DEC_SEQ, D)
    return (y_prompt, y_sample, jnp.stack(new_k, axis=1), jnp.stack(new_v, axis=1))
```

```python
import functools
from typing import Any, NamedTuple

import numpy as np
import jax
import jax.numpy as jnp
from jax import lax
from jax.experimental import pallas as pl
from jax.experimental.pallas import tpu as pltpu

f32 = jnp.float32
bf16 = jnp.bfloat16

D = 1024
BATCH = 16
SEQ = 256
DEPTH = 4
DEC_BATCH = 8
DEC_SEQ = 1024
PAST_LEN = 512
GRID_W = 64
CONV_CH = 512
NA_HEADS = 8
HEAD_DIM = 64
NA_WIDTH = 512
WIN_ROWS = 8
WIN_COLS = 16
IN_WIDTH = 3072
FOURIER_GROUPS = 4
FFN_DENSE = 2816
N_EXPERTS = 8
FFN_EXPERT = 3584
N_EVEN = 2
RMS_EPS = 1e-6
NEG_INF = -1e30
SCALE = HEAD_DIM ** -0.5

LANES = 128
TM = 1024
T_P = BATCH * SEQ
T_S = DEC_BATCH * DEC_SEQ
T = T_P + T_S
NT = T // TM
NTP = T_P // TM
MOD_ROWS = 16
GRID_ROWS = DEC_SEQ // GRID_W
CHUNK_ROWS = 4
N_CHUNKS = GRID_ROWS // CHUNK_ROWS
CHUNK_Q = CHUNK_ROWS * GRID_W
KWIN_ROWS = 12
KWIN = KWIN_ROWS * GRID_W
KB0 = (0, 0, 4, 4)
CHUNKS_PER_GROUP = 4
TF_DENSE = 256
TF_EXP = 896
TM_E = 1024
N_ASSIGN = 2 * T
NT_E = N_ASSIGN // TM_E + N_EXPERTS
R_E = NT_E * TM_E
VMEM_LIMIT = 56 * 1024 * 1024


def _cp(sem, vmem=VMEM_LIMIT):
    return pltpu.CompilerParams(dimension_semantics=sem, vmem_limit_bytes=vmem)


def _mod_index(t):
    return jnp.where(t < NTP, 0, t - (NTP - 1))


def _mod_spec(layer, part):
    def index(t, *_):
        return ((layer * MOD_ROWS + _mod_index(t)) * 6 + part, 0, 0)
    return pl.BlockSpec((1, 1, D), index)


def _norm_mod(x, g, sc, sh):
    y = x * lax.rsqrt(jnp.mean(x * x, axis=-1, keepdims=True) + RMS_EPS)
    return (y * g) * (1 + sc) + sh


def _dot(a, b):
    return jnp.dot(a, b, preferred_element_type=f32)


def _dot_nt(a, b):
    return lax.dot_general(a, b, (((1,), (1,)), ((), ())), preferred_element_type=f32)


def _mod_kernel(cv_ref, w_ref, b_ref, o_ref):
    s = jax.nn.silu(cv_ref[...]).astype(bf16)
    o_ref[0] = _dot(s, w_ref[0].astype(bf16)) + b_ref[0]


def _modulation(cvecs, ada_w, ada_b):
    tn = 1536
    return pl.pallas_call(
        _mod_kernel,
        grid=(DEPTH, 6 * D // tn),
        in_specs=[pl.BlockSpec((MOD_ROWS, D), lambda l, j: (0, 0)),
                  pl.BlockSpec((1, D, tn), lambda l, j: (l, 0, j)),
                  pl.BlockSpec((1, 1, tn), lambda l, j: (l, 0, j))],
        out_specs=pl.BlockSpec((1, MOD_ROWS, tn), lambda l, j: (l, 0, j)),
        out_shape=jax.ShapeDtypeStruct((DEPTH, MOD_ROWS, 6 * D), f32),
        compiler_params=_cp(("arbitrary", "arbitrary")),
    )(cvecs, ada_w, ada_b.reshape(DEPTH, 1, 6 * D))


TN_IN = IN_WIDTH // 2


def _x_specs(x):
    if isinstance(x, tuple):
        return [pl.BlockSpec((TM, D), lambda i, *_: (jnp.minimum(i, NTP - 1), 0)),
                pl.BlockSpec((TM, D), lambda i, *_: (jnp.maximum(i - NTP, 0), 0))]
    return [pl.BlockSpec((TM, D), lambda i, *_: (i, 0))]


def _load_x(i, x_refs):
    if len(x_refs) == 2:
        return jnp.where(i < NTP, x_refs[0][...], x_refs[1][...])
    return x_refs[0][...]


def _inproj_kernel(*refs, n_x, n_alias):
    x_refs, refs = refs[:n_x], refs[n_x + n_alias:]
    g_ref, sc_ref, sh_ref, w_ref, proj_ref, kt_ref, vt_ref, h_ref = refs
    i = pl.program_id(0)
    j = pl.program_id(1)

    @pl.when(j == 0)
    def _():
        h_ref[...] = _norm_mod(_load_x(i, x_refs), g_ref[0], sc_ref[0], sh_ref[0]).astype(bf16)

    acc = _dot(h_ref[...], w_ref[...].astype(bf16))
    proj_ref[...] = acc.astype(bf16)

    @pl.when((j == 1) & (i < NTP))
    def _():
        for ref, col0 in ((kt_ref, NA_WIDTH), (vt_ref, 2 * NA_WIDTH)):
            for s in range(TM // SEQ):
                t = acc[s * SEQ:(s + 1) * SEQ, col0:col0 + NA_WIDTH].T
                for hd in range(NA_HEADS):
                    ref[s, hd] = t[hd * HEAD_DIM:(hd + 1) * HEAD_DIM, :]


def _inproj(x, mod, norm_g, w_in, layer, caches):
    li = layer // 2
    tn = TN_IN
    seqs = TM // SEQ
    xs = x if isinstance(x, tuple) else (x,)
    cache_spec = pl.BlockSpec((seqs, None, NA_HEADS, HEAD_DIM, SEQ),
                              lambda i, j: (jnp.minimum(i, NTP - 1), li, 0, 0, 0))
    cache_shape = jax.ShapeDtypeStruct((BATCH, N_EVEN, NA_HEADS, HEAD_DIM, SEQ), f32)
    aliased = tuple(caches)
    return pl.pallas_call(
        functools.partial(_inproj_kernel, n_x=len(xs), n_alias=len(aliased)),
        grid=(NT, IN_WIDTH // tn),
        in_specs=_x_specs(x) + [pl.BlockSpec(memory_space=pl.ANY)] * len(aliased) + [
            pl.BlockSpec((1, 1, D), lambda i, j: (layer, 0, 0)),
            _mod_spec(layer, 1), _mod_spec(layer, 0),
            pl.BlockSpec((None, D, tn), lambda i, j: (li, 0, j))],
        out_specs=[pl.BlockSpec((TM, tn), lambda i, j: (i, j)), cache_spec, cache_spec],
        out_shape=[jax.ShapeDtypeStruct((T, IN_WIDTH), bf16), cache_shape, cache_shape],
        scratch_shapes=[pltpu.VMEM((TM, D), bf16)],
        input_output_aliases={len(xs) + n: 1 + n for n in range(len(aliased))},
        compiler_params=_cp(("arbitrary", "arbitrary")),
    )(*xs, *aliased, norm_g, mod, mod, w_in)


HEADS_PER_STEP = LANES // HEAD_DIM


class KeyValues(NamedTuple):
    k: jax.Array
    v: jax.Array
    bias: Any = None
    feature_major: bool = False


def _pair_attention(problems):
    first = lax.broadcasted_iota(jnp.int32, (1, LANES), 1) < HEAD_DIM
    mine = (first, jnp.logical_not(first))
    first_t = lax.broadcasted_iota(jnp.int32, (LANES, 1), 0) < HEAD_DIM
    mine_t = (first_t, jnp.logical_not(first_t))
    scores = []
    for q, parts in problems:
        q = q * SCALE
        for hh in range(HEADS_PER_STEP):
            qh = jnp.where(mine[hh], q, 0)
            ss = []
            for part in parts:
                s = _dot(qh, part.k) if part.feature_major else _dot_nt(qh, part.k)
                ss.append(s if part.bias is None else s + part.bias[hh])
            scores.append(ss)
    probs = []
    for ss in scores:
        m = functools.reduce(jnp.maximum, [jnp.max(s, axis=-1, keepdims=True) for s in ss])
        probs.append([jnp.exp((s - m).astype(bf16)) for s in ss])

    def p_times_v(p, part, hh):
        if part.feature_major:
            return _dot_nt(p, jnp.where(mine_t[hh], part.v, 1))
        return _dot(p, jnp.where(mine[hh], part.v, 1))

    results = []
    for n, (_, parts) in enumerate(problems):
        outs = []
        for hh in range(HEADS_PER_STEP):
            pv = functools.reduce(jnp.add, [p_times_v(p, part, hh)
                                            for p, part in zip(probs[n * HEADS_PER_STEP + hh], parts)])
            denom = pv[:, HEAD_DIM:HEAD_DIM + 1] if hh == 0 else pv[:, 0:1]
            outs.append(pv / denom)
        results.append(jnp.where(first, outs[0], outs[1]))
    return results


def _ctx_attn_kernel(q_ref, k_ref, v_ref, o_ref):
    cols = [slice(hp * LANES, (hp + 1) * LANES) for hp in range(NA_HEADS // HEADS_PER_STEP)]
    outs = _pair_attention([(q_ref[:, sl], [KeyValues(k_ref[:, sl], v_ref[:, sl])]) for sl in cols])
    for sl, o in zip(cols, outs):
        o_ref[:, sl] = o.astype(bf16)


def _ctx_attention(proj):
    col = IN_WIDTH // NA_WIDTH - 3
    return pl.pallas_call(
        _ctx_attn_kernel,
        grid=(BATCH,),
        in_specs=[pl.BlockSpec((SEQ, NA_WIDTH), lambda b: (b, col)),
                  pl.BlockSpec((SEQ, NA_WIDTH), lambda b: (b, col + 1)),
                  pl.BlockSpec((SEQ, NA_WIDTH), lambda b: (b, col + 2))],
        out_specs=pl.BlockSpec((SEQ, NA_WIDTH), lambda b: (b, 0)),
        out_shape=jax.ShapeDtypeStruct((T_P, NA_WIDTH), bf16),
        compiler_params=_cp(("arbitrary",)),
    )(proj, proj, proj)


def _win_start(qr):
    return min(max(qr - WIN_ROWS // 2, 0), GRID_ROWS - WIN_ROWS)


N_ROFF = 2 * WIN_ROWS - 1
N_COFF = 2 * WIN_COLS - 1


def _bias_kernel(rpb_ref, o_ref):
    base = (pl.program_id(0) * NA_HEADS + pl.program_id(1)) * (N_ROFF * N_COFF)
    qc = lax.broadcasted_iota(jnp.int32, (GRID_W, GRID_W), 0)
    kc = lax.broadcasted_iota(jnp.int32, (GRID_W, GRID_W), 1)
    coff = kc - qc + (WIN_COLS - 1)
    cs = jnp.clip(qc - WIN_COLS // 2, 0, GRID_W - WIN_COLS)
    valid = (kc >= cs) & (kc < cs + WIN_COLS)
    neg = jnp.full((GRID_W, GRID_W), NEG_INF, f32)
    blocks = []
    for a in range(N_ROFF):
        t = jnp.zeros((GRID_W, GRID_W), f32)
        for b in range(N_COFF):
            t = jnp.where(coff == b, rpb_ref[base + a * N_COFF + b], t)
        blocks.append(jnp.where(valid, t, neg))
    for c in range(N_CHUNKS):
        for ql in range(CHUNK_ROWS):
            qr = c * CHUNK_ROWS + ql
            st = _win_start(qr)
            for kl in range(KWIN_ROWS):
                kr = KB0[c] + kl
                blk = blocks[kr - qr + WIN_ROWS - 1] if st <= kr < st + WIN_ROWS else neg
                o_ref[c, ql * GRID_W:(ql + 1) * GRID_W, kl * GRID_W:(kl + 1) * GRID_W] = blk


def _nbr_bias(rpb):
    return pl.pallas_call(
        _bias_kernel,
        grid=(N_EVEN, NA_HEADS),
        in_specs=[pl.BlockSpec(memory_space=pltpu.SMEM)],
        out_specs=pl.BlockSpec((None, None, N_CHUNKS, CHUNK_Q, KWIN), lambda i, h: (i, h, 0, 0, 0)),
        out_shape=jax.ShapeDtypeStruct((N_EVEN, NA_HEADS, N_CHUNKS, CHUNK_Q, KWIN), f32),
        compiler_params=_cp(("arbitrary", "arbitrary")),
    )(rpb.reshape(-1))


def _nbr_attn_kernel(q_ref, k_ref, v_ref, kc_ref, vc_ref, bias_ref, o_ref):
    ctx = KeyValues(kc_ref[...].reshape(LANES, PAST_LEN).astype(bf16),
                    vc_ref[...].reshape(LANES, PAST_LEN).astype(bf16), feature_major=True)
    for c0 in range(0, N_CHUNKS, CHUNKS_PER_GROUP):
        problems = []
        for c in range(c0, c0 + CHUNKS_PER_GROUP):
            rows = slice(c * CHUNK_Q, (c + 1) * CHUNK_Q)
            win = slice(KB0[c] * GRID_W, KB0[c] * GRID_W + KWIN)
            bias = [bias_ref[hh, c] for hh in range(HEADS_PER_STEP)]
            problems.append((q_ref[rows, :], [KeyValues(k_ref[win, :], v_ref[win, :], bias), ctx]))
        for c, o in zip(range(c0, c0 + CHUNKS_PER_GROUP), _pair_attention(problems)):
            o_ref[c * CHUNK_Q:(c + 1) * CHUNK_Q, :] = o.astype(bf16)


def _nbr_attention(proj, cache_k, cache_v, bias, li):
    qcol = 3 * CONV_CH // LANES
    ncol = NA_WIDTH // LANES
    hp_steps = NA_HEADS // HEADS_PER_STEP

    def col_spec(which):
        return pl.BlockSpec((DEC_SEQ, LANES), lambda hp, b: (NTP + b, qcol + which * ncol + hp))

    ctx_spec = pl.BlockSpec((None, None, HEADS_PER_STEP, HEAD_DIM, PAST_LEN),
                            lambda hp, b: (b, li, hp, 0, 0))
    return pl.pallas_call(
        _nbr_attn_kernel,
        grid=(hp_steps, DEC_BATCH),
        in_specs=[col_spec(0), col_spec(1), col_spec(2), ctx_spec, ctx_spec,
                  pl.BlockSpec((None, HEADS_PER_STEP, N_CHUNKS, CHUNK_Q, KWIN),
                               lambda hp, b: (li, hp, 0, 0, 0))],
        out_specs=pl.BlockSpec((DEC_SEQ, LANES), lambda hp, b: (b, hp)),
        out_shape=jax.ShapeDtypeStruct((T_S, NA_WIDTH), bf16),
        compiler_params=_cp(("arbitrary", "arbitrary")),
    )(proj, proj, proj, cache_k, cache_v, bias)


def _mixout_kernel(*refs, n_x):
    x_refs, (a_ref, ybp_ref, ybs_ref, g1_ref, cw_ref, w_ref, o_ref) = refs[:n_x], refs[n_x:]
    t = pl.program_id(0)
    a_b = a_ref[:, 0:CONV_CH].astype(f32)
    a_c = a_ref[:, CONV_CH:2 * CONV_CH].astype(f32)
    a_x = a_ref[:, 2 * CONV_CH:3 * CONV_CH].astype(f32)
    u = a_c * a_x
    r = lax.broadcasted_iota(jnp.int32, (TM, 1), 0)
    pos = jnp.where(t < NTP, r % SEQ, r)
    last = jnp.where(t < NTP, SEQ - 1, DEC_SEQ - 1)
    u_prev = jnp.where(pos == 0, 0.0, pltpu.roll(u, 1, axis=0))
    u_next = jnp.where(pos == last, 0.0, pltpu.roll(u, TM - 1, axis=0))
    y_a = a_b * (u_prev * cw_ref[0:1, :] + u * cw_ref[1:2, :] + u_next * cw_ref[2:3, :])
    y_b = jnp.where(t < NTP, ybp_ref[...], ybs_ref[...])
    y = (_dot(y_a.astype(bf16), w_ref[0:CONV_CH, :].astype(bf16))
         + _dot(y_b, w_ref[CONV_CH:, :].astype(bf16)))
    o_ref[...] = _load_x(t, x_refs) + g1_ref[0] * y


def _mixout(proj, yb_p, yb_s, x, mod, conv_w, w_out, layer):
    li = layer // 2
    xs = x if isinstance(x, tuple) else (x,)
    return pl.pallas_call(
        functools.partial(_mixout_kernel, n_x=len(xs)),
        grid=(NT,),
        in_specs=_x_specs(x) + [
            pl.BlockSpec((TM, 3 * CONV_CH), lambda t: (t, 0)),
            pl.BlockSpec((TM, NA_WIDTH), lambda t: (jnp.minimum(t, NTP - 1), 0)),
            pl.BlockSpec((TM, NA_WIDTH), lambda t: (jnp.maximum(t - NTP, 0), 0)),
            _mod_spec(layer, 2),
            pl.BlockSpec((None, 3, CONV_CH), lambda t: (li, 0, 0)),
            pl.BlockSpec((None, D, D), lambda t: (li, 0, 0))],
        out_specs=pl.BlockSpec((TM, D), lambda t: (t, 0)),
        out_shape=jax.ShapeDtypeStruct((T, D), f32),
        compiler_params=_cp(("arbitrary",)),
    )(*xs, proj, yb_p, yb_s, mod, conv_w, w_out)


def _ffn_kernel(x_ref, g_ref, sc_ref, sh_ref, g2_ref, w1_ref, w3_ref, w2_ref, o_ref, h_ref, acc_ref):
    j = pl.program_id(1)

    @pl.when(j == 0)
    def _():
        h_ref[...] = _norm_mod(x_ref[...], g_ref[0], sc_ref[0], sh_ref[0]).astype(bf16)
        acc_ref[...] = jnp.zeros_like(acc_ref)

    h = h_ref[...]
    gate = jax.nn.silu(_dot(h, w1_ref[...].astype(bf16))) * _dot(h, w3_ref[...].astype(bf16))
    acc_ref[...] += _dot(gate.astype(bf16), w2_ref[...].astype(bf16))

    @pl.when(j == pl.num_programs(1) - 1)
    def _():
        o_ref[...] = x_ref[...] + g2_ref[0] * acc_ref[...]


def _dense_ffn(x, mod, norm_g, w1, w3, w2, layer):
    li = layer // 2
    tf = TF_DENSE
    return pl.pallas_call(
        _ffn_kernel,
        grid=(NT, FFN_DENSE // tf),
        in_specs=[pl.BlockSpec((TM, D), lambda i, j: (i, 0)),
                  pl.BlockSpec((1, 1, D), lambda i, j: (layer, 0, 0)),
                  _mod_spec(layer, 4), _mod_spec(layer, 3), _mod_spec(layer, 5),
                  pl.BlockSpec((None, D, tf), lambda i, j: (li, 0, j)),
                  pl.BlockSpec((None, D, tf), lambda i, j: (li, 0, j)),
                  pl.BlockSpec((None, tf, D), lambda i, j: (li, j, 0))],
        out_specs=pl.BlockSpec((TM, D), lambda i, j: (i, 0)),
        out_shape=jax.ShapeDtypeStruct((T, D), f32),
        scratch_shapes=[pltpu.VMEM((TM, D), bf16), pltpu.VMEM((TM, D), f32)],
        compiler_params=_cp(("arbitrary", "arbitrary")),
    )(x, norm_g, mod, mod, mod, w1, w3, w2)


GROUP_CH = D // FOURIER_GROUPS


def _dft_mats(n):
    k = np.arange(n, dtype=np.int64)
    ang = 2.0 * np.pi * ((k[:, None] * k[None, :]) % n).astype(np.float64) / n
    return np.cos(ang).astype(np.float32), np.sin(ang).astype(np.float32)


def _fourier_kernel(x_ref, g_ref, sc_ref, sh_ref, g1_ref, cs_ref, ss_ref, cl_ref, sl_ref, wf_ref,
                    o_ref, f_ref):
    t = pl.program_id(0)
    h = _norm_mod(x_ref[...], g_ref[0], sc_ref[0], sh_ref[0]).astype(bf16)
    cs = cs_ref[...].astype(bf16)
    ss = ss_ref[...].astype(bf16)
    ys, zs = [], []
    for g in range(FOURIER_GROUPS):
        hg = h[:, g * GROUP_CH:(g + 1) * GROUP_CH]
        ys.append(_dot(hg, cs))
        zs.append(_dot(hg, ss))
    y = jnp.concatenate(ys, axis=-1).astype(bf16)
    z = jnp.concatenate(zs, axis=-1).astype(bf16)

    @pl.when(t < NTP)
    def _():
        for s in range(TM // SEQ):
            rows = slice(s * SEQ, (s + 1) * SEQ)
            f = _dot(cs, y[rows]) - _dot(ss, z[rows])
            f_ref[rows, :] = (f * ((SEQ * GROUP_CH) ** -0.5)).astype(bf16)

    @pl.when(t >= NTP)
    def _():
        f = _dot(cl_ref[...].astype(bf16), y) - _dot(sl_ref[...].astype(bf16), z)
        f_ref[...] = (f * ((DEC_SEQ * GROUP_CH) ** -0.5)).astype(bf16)

    o_ref[...] = x_ref[...] + g1_ref[0] * _dot(f_ref[...], wf_ref[...].astype(bf16))


def _fourier(x, mod, norm_g, fourier_w, layer):
    li = layer // 2
    c_s, s_s = _dft_mats(SEQ)
    c_l, s_l = _dft_mats(DEC_SEQ)
    const = lambda shape: pl.BlockSpec(shape, lambda t: (0, 0))
    return pl.pallas_call(
        _fourier_kernel,
        grid=(NT,),
        in_specs=[pl.BlockSpec((TM, D), lambda t: (t, 0)),
                  pl.BlockSpec((1, 1, D), lambda t: (layer, 0, 0)),
                  _mod_spec(layer, 1), _mod_spec(layer, 0), _mod_spec(layer, 2),
                  const((SEQ, SEQ)), const((SEQ, SEQ)),
                  const((DEC_SEQ, DEC_SEQ)), const((DEC_SEQ, DEC_SEQ)),
                  pl.BlockSpec((None, D, D), lambda t: (li, 0, 0))],
        out_specs=pl.BlockSpec((TM, D), lambda t: (t, 0)),
        out_shape=jax.ShapeDtypeStruct((T, D), f32),
        scratch_shapes=[pltpu.VMEM((TM, D), bf16)],
        compiler_params=_cp(("arbitrary",)),
    )(x, norm_g, mod, mod, mod, jnp.asarray(c_s), jnp.asarray(s_s), jnp.asarray(c_l), jnp.asarray(s_l),
      fourier_w)


M_E0, M_E1, M_R0, M_R1, M_W0, M_W1 = range(6)
ROW_UNROLL = 8


META_ROWS = 8


def _router_kernel(x_ref, g_ref, sc_ref, sh_ref, rw_ref, rb_ref, h_ref, meta_ref, meta_t_ref, cnt_ref,
                   carry_ref):
    t = pl.program_id(0)

    @pl.when(t == 0)
    def _():
        carry_ref[...] = jnp.zeros_like(carry_ref)

    h = _norm_mod(x_ref[...], g_ref[0], sc_ref[0], sh_ref[0])
    h_ref[...] = h
    logits = jnp.dot(h, rw_ref[...], preferred_element_type=f32,
                     precision=lax.Precision.HIGHEST) + rb_ref[...]
    lane = lax.broadcasted_iota(jnp.int32, (TM, LANES), 1)
    m1 = jnp.max(logits, axis=-1, keepdims=True)
    i1 = jnp.min(jnp.where(logits == m1, lane, LANES), axis=-1, keepdims=True)
    rest = jnp.where(lane == i1, -jnp.inf, logits)
    m2 = jnp.max(rest, axis=-1, keepdims=True)
    i2 = jnp.min(jnp.where(rest == m2, lane, LANES), axis=-1, keepdims=True)
    e = jnp.exp(m2 - m1)
    w0 = 1.0 / (1.0 + e)
    w1 = e / (1.0 + e)
    oh0 = (lane == i1).astype(f32)
    oh1 = (lane == i2).astype(f32)
    oh = oh0 + oh1
    row = lax.broadcasted_iota(jnp.int32, (TM, TM), 0)
    col = lax.broadcasted_iota(jnp.int32, (TM, TM), 1)
    before = jnp.where(col < row, 1.0, 0.0).astype(bf16)
    base = carry_ref[...] + _dot(before, oh.astype(bf16))
    r0 = jnp.sum(oh0 * base, axis=-1, keepdims=True)
    r1 = jnp.sum(oh1 * base, axis=-1, keepdims=True)
    carry_ref[...] += jnp.sum(oh, axis=0, keepdims=True)
    rec = jnp.zeros((TM, LANES), f32)
    for idx, val in ((M_E0, i1.astype(f32)), (M_E1, i2.astype(f32)), (M_R0, r0), (M_R1, r1),
                     (M_W0, w0), (M_W1, w1)):
        rec = jnp.where(lane == idx, val, rec)
    meta_ref[...] = rec
    meta_t_ref[...] = rec.T[:META_ROWS, :]
    cnt_ref[...] = carry_ref[...]


def _router(x, mod, norm_g, router_w, router_b, layer):
    li = layer // 2
    rw = jnp.pad(router_w[li], ((0, 0), (0, LANES - N_EXPERTS)))
    rb = jnp.pad(router_b[li], (0, LANES - N_EXPERTS), constant_values=NEG_INF).reshape(1, LANES)
    return pl.pallas_call(
        _router_kernel,
        grid=(NT,),
        in_specs=[pl.BlockSpec((TM, D), lambda t: (t, 0)),
                  pl.BlockSpec((1, 1, D), lambda t: (layer, 0, 0)),
                  _mod_spec(layer, 4), _mod_spec(layer, 3),
                  pl.BlockSpec((D, LANES), lambda t: (0, 0)),
                  pl.BlockSpec((1, LANES), lambda t: (0, 0))],
        out_specs=[pl.BlockSpec((TM, D), lambda t: (t, 0)),
                   pl.BlockSpec((TM, LANES), lambda t: (t, 0)),
                   pl.BlockSpec((META_ROWS, TM), lambda t: (0, t)),
                   pl.BlockSpec((1, LANES), lambda t: (0, 0))],
        out_shape=[jax.ShapeDtypeStruct((T, D), f32),
                   jax.ShapeDtypeStruct((T, LANES), f32),
                   jax.ShapeDtypeStruct((META_ROWS, T), f32),
                   jax.ShapeDtypeStruct((1, LANES), f32)],
        scratch_shapes=[pltpu.VMEM((1, LANES), f32)],
        compiler_params=_cp(("arbitrary",)),
    )(x, norm_g, mod, mod, rw, rb)


def _row_copy(src_ref, src_row, dst_ref, dst_row, sem):
    return pltpu.make_async_copy(src_ref.at[pl.ds(src_row, 1)], dst_ref.at[pl.ds(dst_row, 1)], sem)


def _dispatch_kernel(dest_ref, h_ref, xg_in_ref, xg_ref, sem):
    del xg_in_ref

    def issue(g, c):
        for u in range(ROW_UNROLL):
            r = g * ROW_UNROLL + u
            for k in range(2):
                _row_copy(h_ref, r, xg_ref, dest_ref[0, 0, k * TM + r], sem).start()
        return c

    lax.fori_loop(0, TM // ROW_UNROLL, issue, 0)
    for k in range(2):
        pltpu.make_async_copy(h_ref, xg_ref.at[pl.ds(0, TM)], sem).wait()


def _dispatch(dest, h):
    xg0 = jnp.zeros((R_E, D), f32)
    return pl.pallas_call(
        _dispatch_kernel,
        grid=(NT,),
        in_specs=[pl.BlockSpec((1, 1, 2 * TM), lambda t: (t, 0, 0), memory_space=pltpu.SMEM),
                  pl.BlockSpec((TM, D), lambda t: (t, 0)),
                  pl.BlockSpec(memory_space=pl.ANY)],
        out_specs=pl.BlockSpec(memory_space=pl.ANY),
        out_shape=jax.ShapeDtypeStruct((R_E, D), f32),
        scratch_shapes=[pltpu.SemaphoreType.DMA(())],
        input_output_aliases={2: 0},
        compiler_params=_cp(("arbitrary",)),
    )(dest, h, xg0)


NJ_EXP = FFN_EXPERT // TF_EXP


def _expert_kernel(te_ref, nu_ref, xg_ref, w1_ref, w3_ref, w2_ref, y_ref, h_ref):
    t = pl.program_id(0)
    j = pl.program_id(1)
    used = t < nu_ref[0]

    @pl.when(j == 0)
    def _():
        h_ref[...] = xg_ref[...].astype(bf16)
        y_ref[...] = jnp.zeros_like(y_ref)

    @pl.when(used)
    def _():
        h = h_ref[...]
        gate = jax.nn.silu(_dot(h, w1_ref[...].astype(bf16))) * _dot(h, w3_ref[...].astype(bf16))
        y_ref[...] += _dot(gate.astype(bf16), w2_ref[...].astype(bf16))


def _experts(tile_expert, n_used, xg, w1, w3, w2, layer):
    li = layer // 2
    tf = TF_EXP
    nj = NJ_EXP

    def jj(t, j, nu):
        return jnp.where(t < nu[0], j, nj - 1)

    def tt(t, nu):
        return jnp.minimum(t, jnp.maximum(nu[0] - 1, 0))

    grid_spec = pltpu.PrefetchScalarGridSpec(
        num_scalar_prefetch=2,
        grid=(NT_E, nj),
        in_specs=[pl.BlockSpec((TM_E, D), lambda t, j, te, nu: (tt(t, nu), 0)),
                  pl.BlockSpec((None, None, D, tf), lambda t, j, te, nu: (li, te[t], 0, jj(t, j, nu))),
                  pl.BlockSpec((None, None, D, tf), lambda t, j, te, nu: (li, te[t], 0, jj(t, j, nu))),
                  pl.BlockSpec((None, None, tf, D), lambda t, j, te, nu: (li, te[t], jj(t, j, nu), 0))],
        out_specs=pl.BlockSpec((TM_E, D), lambda t, j, te, nu: (t, 0)),
        scratch_shapes=[pltpu.VMEM((TM_E, D), bf16)],
    )
    return pl.pallas_call(
        _expert_kernel,
        grid_spec=grid_spec,
        out_shape=jax.ShapeDtypeStruct((R_E, D), f32),
        compiler_params=_cp(("arbitrary", "arbitrary")),
    )(tile_expert, n_used, xg, w1, w3, w2)


def _combine_kernel(dest_ref, x_ref, g2_ref, meta_ref, y_ref, o_ref, buf_ref, sem):
    def issue(g, c):
        for u in range(ROW_UNROLL):
            r = g * ROW_UNROLL + u
            for k in range(2):
                _row_copy(y_ref, dest_ref[0, 0, k * TM + r], buf_ref.at[k], r, sem).start()
        return c

    lax.fori_loop(0, TM // ROW_UNROLL, issue, 0)
    for k in range(2):
        pltpu.make_async_copy(y_ref.at[pl.ds(0, TM)], buf_ref.at[k], sem).wait()
    w0 = meta_ref[:, M_W0:M_W0 + 1]
    w1 = meta_ref[:, M_W1:M_W1 + 1]
    o_ref[...] = x_ref[...] + g2_ref[0] * (w0 * buf_ref[0] + w1 * buf_ref[1])


def _combine(dest, x, mod, meta, y, layer):
    return pl.pallas_call(
        _combine_kernel,
        grid=(NT,),
        in_specs=[pl.BlockSpec((1, 1, 2 * TM), lambda t: (t, 0, 0), memory_space=pltpu.SMEM),
                  pl.BlockSpec((TM, D), lambda t: (t, 0)),
                  _mod_spec(layer, 5),
                  pl.BlockSpec((TM, LANES), lambda t: (t, 0)),
                  pl.BlockSpec(memory_space=pl.ANY)],
        out_specs=pl.BlockSpec((TM, D), lambda t: (t, 0)),
        out_shape=jax.ShapeDtypeStruct((T, D), f32),
        scratch_shapes=[pltpu.VMEM((2, TM, D), f32), pltpu.SemaphoreType.DMA(())],
        compiler_params=_cp(("arbitrary",)),
    )(dest, x, mod, meta, y)


def _moe(x, mod, norm_g, router_w, router_b, w1, w3, w2, layer):
    h_rows, meta, meta_t, counts = _router(x, mod, norm_g, router_w, router_b, layer)
    cnt = counts[0, :N_EXPERTS].astype(jnp.int32)
    padded = ((cnt + TM_E - 1) // TM_E) * TM_E
    ends = jnp.cumsum(padded)
    starts = ends - padded
    experts = meta_t[M_E0:M_E1 + 1].astype(jnp.int32)
    ranks = meta_t[M_R0:M_R1 + 1].astype(jnp.int32)
    start_of = functools.reduce(lambda acc, e: jnp.where(experts == e, starts[e], acc), range(N_EXPERTS), 0)
    dest = start_of + ranks
    dest = dest.reshape(2, NT, TM).transpose(1, 0, 2).reshape(NT, 1, 2 * TM)
    n_used = (ends[-1] // TM_E).astype(jnp.int32).reshape(1)
    tile_start = jnp.minimum(jnp.arange(NT_E, dtype=jnp.int32), n_used[0] - 1) * TM_E
    tile_expert = jnp.sum((tile_start[:, None] >= ends[None, :]).astype(jnp.int32), axis=1)
    tile_expert = jnp.minimum(tile_expert, N_EXPERTS - 1).astype(jnp.int32)
    xg = _dispatch(dest, h_rows)
    y = _experts(tile_expert, n_used, xg, w1, w3, w2, layer)
    return _combine(dest, x, mod, meta, y, layer)


def _final_kernel(x_ref, g_ref, o_ref):
    x = x_ref[...]
    o_ref[...] = (x * lax.rsqrt(jnp.mean(x * x, axis=-1, keepdims=True) + RMS_EPS)) * g_ref[...]


def _final_norm(x, final_g, first_tile, n_tiles):
    return pl.pallas_call(
        _final_kernel,
        grid=(n_tiles,),
        in_specs=[pl.BlockSpec((TM, D), lambda t: (first_tile + t, 0)),
                  pl.BlockSpec((1, D), lambda t: (0, 0))],
        out_specs=pl.BlockSpec((TM, D), lambda t: (t, 0)),
        out_shape=jax.ShapeDtypeStruct((n_tiles * TM, D), f32),
        compiler_params=_cp(("arbitrary",)),
    )(x, final_g.reshape(1, D))


def kernel(x_prompt, x_sample, c, cache_k, cache_v, c_ctx, ada_w, ada_b, norm1_g, norm2_g, w_in, conv_w, rpb,
           w_out, ffn_w1, ffn_w3, ffn_w2, fourier_w, router_w, router_b, moe_w1, moe_w3, moe_w2, final_g):
    cvecs = jnp.concatenate([c_ctx[None, :], c, jnp.zeros((MOD_ROWS - 1 - DEC_BATCH, D), f32)], axis=0)
    mod = _modulation(cvecs, ada_w, ada_b).reshape(DEPTH * MOD_ROWS * 6, 1, D)
    n1 = norm1_g.reshape(DEPTH, 1, D)
    n2 = norm2_g.reshape(DEPTH, 1, D)
    bias = _nbr_bias(rpb)
    cache_kt = jnp.swapaxes(cache_k, -1, -2)
    cache_vt = jnp.swapaxes(cache_v, -1, -2)
    x = (x_prompt.reshape(T_P, D), x_sample.reshape(T_S, D))
    caches = [jnp.zeros((BATCH, N_EVEN, NA_HEADS, HEAD_DIM, SEQ), f32) for _ in range(2)]
    for layer in range(DEPTH):
        li = layer // 2
        if layer % 2 == 0:
            proj, *caches = _inproj(x, mod, n1, w_in, layer, caches)
            yb_p = _ctx_attention(proj)
            yb_s = _nbr_attention(proj, cache_kt, cache_vt, bias, li)
            x = _mixout(proj, yb_p, yb_s, x, mod, conv_w, w_out, layer)
            x = _dense_ffn(x, mod, n2, ffn_w1, ffn_w3, ffn_w2, layer)
        else:
            x = _fourier(x, mod, n1, fourier_w, layer)
            x = _moe(x, mod, n2, router_w, router_b, moe_w1, moe_w3, moe_w2, layer)
    y_prompt = _final_norm(x, final_g, 0, NTP).reshape(BATCH, SEQ, D)
    y_sample = _final_norm(x, final_g, NTP, NT - NTP).reshape(DEC_BATCH, DEC_SEQ, D)
    new_kt, new_vt = caches
    return (y_prompt, y_sample, jnp.swapaxes(new_kt, -1, -2), jnp.swapaxes(new_vt, -1, -2))
```

```python
import functools
from typing import Any, NamedTuple

import numpy as np
import jax
import jax.numpy as jnp
from jax import lax
from jax.experimental import pallas as pl
from jax.experimental.pallas import tpu as pltpu

f32 = jnp.float32
bf16 = jnp.bfloat16

D = 1024
BATCH = 16
SEQ = 256
DEPTH = 4
DEC_BATCH = 8
DEC_SEQ = 1024
PAST_LEN = 512
GRID_W = 64
CONV_CH = 512
NA_HEADS = 8
HEAD_DIM = 64
NA_WIDTH = 512
WIN_ROWS = 8
WIN_COLS = 16
IN_WIDTH = 3072
FOURIER_GROUPS = 4
FFN_DENSE = 2816
N_EXPERTS = 8
FFN_EXPERT = 3584
N_EVEN = 2
RMS_EPS = 1e-6
NEG_INF = -1e30
SCALE = HEAD_DIM ** -0.5

LANES = 128
TM = 1024
T_P = BATCH * SEQ
T_S = DEC_BATCH * DEC_SEQ
T = T_P + T_S
NT = T // TM
NTP = T_P // TM
MOD_ROWS = 16
GRID_ROWS = DEC_SEQ // GRID_W
CHUNK_ROWS = 4
N_CHUNKS = GRID_ROWS // CHUNK_ROWS
CHUNK_Q = CHUNK_ROWS * GRID_W
KWIN_ROWS = 12
KWIN = KWIN_ROWS * GRID_W
KB0 = (0, 0, 4, 4)
CHUNKS_PER_GROUP = 4
TF_DENSE = 256
TF_EXP = 512
TM_E = 1024
N_ASSIGN = 2 * T
NT_E = N_ASSIGN // TM_E + N_EXPERTS
R_E = NT_E * TM_E
VMEM_LIMIT = 56 * 1024 * 1024


def _cp(sem, vmem=VMEM_LIMIT):
    return pltpu.CompilerParams(dimension_semantics=sem, vmem_limit_bytes=vmem)


def _mod_index(t):
    return jnp.where(t < NTP, 0, t - (NTP - 1))


def _mod_spec(layer, part):
    def index(t, *_):
        return ((layer * MOD_ROWS + _mod_index(t)) * 6 + part, 0, 0)
    return pl.BlockSpec((1, 1, D), index)


def _norm_mod(x, g, sc, sh):
    y = x * lax.rsqrt(jnp.mean(x * x, axis=-1, keepdims=True) + RMS_EPS)
    return (y * g) * (1 + sc) + sh


def _dot(a, b):
    return jnp.dot(a, b, preferred_element_type=f32)


def _dot_nt(a, b):
    return lax.dot_general(a, b, (((1,), (1,)), ((), ())), preferred_element_type=f32)


def _mod_kernel(cv_ref, w_ref, b_ref, o_ref):
    s = jax.nn.silu(cv_ref[...]).astype(bf16)
    o_ref[0] = _dot(s, w_ref[0].astype(bf16)) + b_ref[0]


def _modulation(cvecs, ada_w, ada_b):
    tn = 1536
    return pl.pallas_call(
        _mod_kernel,
        grid=(DEPTH, 6 * D // tn),
        in_specs=[pl.BlockSpec((MOD_ROWS, D), lambda l, j: (0, 0)),
                  pl.BlockSpec((1, D, tn), lambda l, j: (l, 0, j)),
                  pl.BlockSpec((1, 1, tn), lambda l, j: (l, 0, j))],
        out_specs=pl.BlockSpec((1, MOD_ROWS, tn), lambda l, j: (l, 0, j)),
        out_shape=jax.ShapeDtypeStruct((DEPTH, MOD_ROWS, 6 * D), f32),
        compiler_params=_cp(("arbitrary", "arbitrary")),
    )(cvecs, ada_w, ada_b.reshape(DEPTH, 1, 6 * D))


TN_IN = IN_WIDTH // 2


def _x_specs(x):
    if isinstance(x, tuple):
        return [pl.BlockSpec((TM, D), lambda i, *_: (jnp.minimum(i, NTP - 1), 0)),
                pl.BlockSpec((TM, D), lambda i, *_: (jnp.maximum(i - NTP, 0), 0))]
    return [pl.BlockSpec((TM, D), lambda i, *_: (i, 0))]


def _load_x(i, x_refs):
    if len(x_refs) == 2:
        return jnp.where(i < NTP, x_refs[0][...], x_refs[1][...])
    return x_refs[0][...]


def _inproj_kernel(*refs, n_x, n_alias):
    x_refs, refs = refs[:n_x], refs[n_x + n_alias:]
    g_ref, sc_ref, sh_ref, w_ref, proj_ref, kt_ref, vt_ref, h_ref = refs
    i = pl.program_id(0)
    j = pl.program_id(1)

    @pl.when(j == 0)
    def _():
        h_ref[...] = _norm_mod(_load_x(i, x_refs), g_ref[0], sc_ref[0], sh_ref[0]).astype(bf16)

    acc = _dot(h_ref[...], w_ref[...].astype(bf16))
    proj_ref[...] = acc.astype(bf16)

    @pl.when((j == 1) & (i < NTP))
    def _():
        for ref, col0 in ((kt_ref, NA_WIDTH), (vt_ref, 2 * NA_WIDTH)):
            for s in range(TM // SEQ):
                t = acc[s * SEQ:(s + 1) * SEQ, col0:col0 + NA_WIDTH].T
                for hd in range(NA_HEADS):
                    ref[s, hd] = t[hd * HEAD_DIM:(hd + 1) * HEAD_DIM, :]


def _inproj(x, mod, norm_g, w_in, layer, caches):
    li = layer // 2
    tn = TN_IN
    seqs = TM // SEQ
    xs = x if isinstance(x, tuple) else (x,)
    cache_spec = pl.BlockSpec((seqs, None, NA_HEADS, HEAD_DIM, SEQ),
                              lambda i, j: (jnp.minimum(i, NTP - 1), li, 0, 0, 0))
    cache_shape = jax.ShapeDtypeStruct((BATCH, N_EVEN, NA_HEADS, HEAD_DIM, SEQ), f32)
    aliased = tuple(caches)
    return pl.pallas_call(
        functools.partial(_inproj_kernel, n_x=len(xs), n_alias=len(aliased)),
        grid=(NT, IN_WIDTH // tn),
        in_specs=_x_specs(x) + [pl.BlockSpec(memory_space=pl.ANY)] * len(aliased) + [
            pl.BlockSpec((1, 1, D), lambda i, j: (layer, 0, 0)),
            _mod_spec(layer, 1), _mod_spec(layer, 0),
            pl.BlockSpec((None, D, tn), lambda i, j: (li, 0, j))],
        out_specs=[pl.BlockSpec((TM, tn), lambda i, j: (i, j)), cache_spec, cache_spec],
        out_shape=[jax.ShapeDtypeStruct((T, IN_WIDTH), bf16), cache_shape, cache_shape],
        scratch_shapes=[pltpu.VMEM((TM, D), bf16)],
        input_output_aliases={len(xs) + n: 1 + n for n in range(len(aliased))},
        compiler_params=_cp(("arbitrary", "arbitrary")),
    )(*xs, *aliased, norm_g, mod, mod, w_in)


HEADS_PER_STEP = LANES // HEAD_DIM


class KeyValues(NamedTuple):
    k: jax.Array
    v: jax.Array
    bias: Any = None
    feature_major: bool = False


def _pair_attention(problems):
    first = lax.broadcasted_iota(jnp.int32, (1, LANES), 1) < HEAD_DIM
    mine = (first, jnp.logical_not(first))
    first_t = lax.broadcasted_iota(jnp.int32, (LANES, 1), 0) < HEAD_DIM
    mine_t = (first_t, jnp.logical_not(first_t))
    scores = []
    for q, parts in problems:
        q = q * SCALE
        for hh in range(HEADS_PER_STEP):
            qh = jnp.where(mine[hh], q, 0)
            ss = []
            for part in parts:
                s = _dot(qh, part.k) if part.feature_major else _dot_nt(qh, part.k)
                ss.append(s if part.bias is None else s + part.bias[hh])
            scores.append(ss)
    probs = []
    for ss in scores:
        m = functools.reduce(jnp.maximum, [jnp.max(s, axis=-1, keepdims=True) for s in ss])
        probs.append([jnp.exp((s - m).astype(bf16)) for s in ss])

    def p_times_v(p, part, hh):
        if part.feature_major:
            return _dot_nt(p, jnp.where(mine_t[hh], part.v, 1))
        return _dot(p, jnp.where(mine[hh], part.v, 1))

    results = []
    for n, (_, parts) in enumerate(problems):
        outs = []
        for hh in range(HEADS_PER_STEP):
            pv = functools.reduce(jnp.add, [p_times_v(p, part, hh)
                                            for p, part in zip(probs[n * HEADS_PER_STEP + hh], parts)])
            denom = pv[:, HEAD_DIM:HEAD_DIM + 1] if hh == 0 else pv[:, 0:1]
            outs.append(pv / denom)
        results.append(jnp.where(first, outs[0], outs[1]))
    return results


def _ctx_attn_kernel(q_ref, k_ref, v_ref, o_ref):
    cols = [slice(hp * LANES, (hp + 1) * LANES) for hp in range(NA_HEADS // HEADS_PER_STEP)]
    outs = _pair_attention([(q_ref[:, sl], [KeyValues(k_ref[:, sl], v_ref[:, sl])]) for sl in cols])
    for sl, o in zip(cols, outs):
        o_ref[:, sl] = o.astype(bf16)


def _ctx_attention(proj):
    col = IN_WIDTH // NA_WIDTH - 3
    return pl.pallas_call(
        _ctx_attn_kernel,
        grid=(BATCH,),
        in_specs=[pl.BlockSpec((SEQ, NA_WIDTH), lambda b: (b, col)),
                  pl.BlockSpec((SEQ, NA_WIDTH), lambda b: (b, col + 1)),
                  pl.BlockSpec((SEQ, NA_WIDTH), lambda b: (b, col + 2))],
        out_specs=pl.BlockSpec((SEQ, NA_WIDTH), lambda b: (b, 0)),
        out_shape=jax.ShapeDtypeStruct((T_P, NA_WIDTH), bf16),
        compiler_params=_cp(("arbitrary",)),
    )(proj, proj, proj)


def _win_start(qr):
    return min(max(qr - WIN_ROWS // 2, 0), GRID_ROWS - WIN_ROWS)


N_ROFF = 2 * WIN_ROWS - 1
N_COFF = 2 * WIN_COLS - 1


def _bias_kernel(rpb_ref, o_ref):
    base = (pl.program_id(0) * NA_HEADS + pl.program_id(1)) * (N_ROFF * N_COFF)
    qc = lax.broadcasted_iota(jnp.int32, (GRID_W, GRID_W), 0)
    kc = lax.broadcasted_iota(jnp.int32, (GRID_W, GRID_W), 1)
    coff = kc - qc + (WIN_COLS - 1)
    cs = jnp.clip(qc - WIN_COLS // 2, 0, GRID_W - WIN_COLS)
    valid = (kc >= cs) & (kc < cs + WIN_COLS)
    neg = jnp.full((GRID_W, GRID_W), NEG_INF, f32)
    blocks = []
    for a in range(N_ROFF):
        t = jnp.zeros((GRID_W, GRID_W), f32)
        for b in range(N_COFF):
            t = jnp.where(coff == b, rpb_ref[base + a * N_COFF + b], t)
        blocks.append(jnp.where(valid, t, neg))
    for c in range(N_CHUNKS):
        for ql in range(CHUNK_ROWS):
            qr = c * CHUNK_ROWS + ql
            st = _win_start(qr)
            for kl in range(KWIN_ROWS):
                kr = KB0[c] + kl
                blk = blocks[kr - qr + WIN_ROWS - 1] if st <= kr < st + WIN_ROWS else neg
                o_ref[c, ql * GRID_W:(ql + 1) * GRID_W, kl * GRID_W:(kl + 1) * GRID_W] = blk


def _nbr_bias(rpb):
    return pl.pallas_call(
        _bias_kernel,
        grid=(N_EVEN, NA_HEADS),
        in_specs=[pl.BlockSpec(memory_space=pltpu.SMEM)],
        out_specs=pl.BlockSpec((None, None, N_CHUNKS, CHUNK_Q, KWIN), lambda i, h: (i, h, 0, 0, 0)),
        out_shape=jax.ShapeDtypeStruct((N_EVEN, NA_HEADS, N_CHUNKS, CHUNK_Q, KWIN), f32),
        compiler_params=_cp(("arbitrary", "arbitrary")),
    )(rpb.reshape(-1))


def _nbr_attn_kernel(q_ref, k_ref, v_ref, kc_ref, vc_ref, bias_ref, o_ref):
    ctx = KeyValues(kc_ref[...].reshape(LANES, PAST_LEN).astype(bf16),
                    vc_ref[...].reshape(LANES, PAST_LEN).astype(bf16), feature_major=True)
    for c0 in range(0, N_CHUNKS, CHUNKS_PER_GROUP):
        problems = []
        for c in range(c0, c0 + CHUNKS_PER_GROUP):
            rows = slice(c * CHUNK_Q, (c + 1) * CHUNK_Q)
            win = slice(KB0[c] * GRID_W, KB0[c] * GRID_W + KWIN)
            bias = [bias_ref[hh, c] for hh in range(HEADS_PER_STEP)]
            problems.append((q_ref[rows, :], [KeyValues(k_ref[win, :], v_ref[win, :], bias), ctx]))
        for c, o in zip(range(c0, c0 + CHUNKS_PER_GROUP), _pair_attention(problems)):
            o_ref[c * CHUNK_Q:(c + 1) * CHUNK_Q, :] = o.astype(bf16)


def _nbr_attention(proj, cache_k, cache_v, bias, li):
    qcol = 3 * CONV_CH // LANES
    ncol = NA_WIDTH // LANES
    hp_steps = NA_HEADS // HEADS_PER_STEP

    def col_spec(which):
        return pl.BlockSpec((DEC_SEQ, LANES), lambda hp, b: (NTP + b, qcol + which * ncol + hp))

    ctx_spec = pl.BlockSpec((None, None, HEADS_PER_STEP, HEAD_DIM, PAST_LEN),
                            lambda hp, b: (b, li, hp, 0, 0))
    return pl.pallas_call(
        _nbr_attn_kernel,
        grid=(hp_steps, DEC_BATCH),
        in_specs=[col_spec(0), col_spec(1), col_spec(2), ctx_spec, ctx_spec,
                  pl.BlockSpec((None, HEADS_PER_STEP, N_CHUNKS, CHUNK_Q, KWIN),
                               lambda hp, b: (li, hp, 0, 0, 0))],
        out_specs=pl.BlockSpec((DEC_SEQ, LANES), lambda hp, b: (b, hp)),
        out_shape=jax.ShapeDtypeStruct((T_S, NA_WIDTH), bf16),
        compiler_params=_cp(("arbitrary", "arbitrary")),
    )(proj, proj, proj, cache_k, cache_v, bias)


def _mixout_kernel(*refs, n_x):
    x_refs, (a_ref, ybp_ref, ybs_ref, g1_ref, cw_ref, w_ref, o_ref) = refs[:n_x], refs[n_x:]
    t = pl.program_id(0)
    a_b = a_ref[:, 0:CONV_CH].astype(f32)
    a_c = a_ref[:, CONV_CH:2 * CONV_CH].astype(f32)
    a_x = a_ref[:, 2 * CONV_CH:3 * CONV_CH].astype(f32)
    u = a_c * a_x
    r = lax.broadcasted_iota(jnp.int32, (TM, 1), 0)
    pos = jnp.where(t < NTP, r % SEQ, r)
    last = jnp.where(t < NTP, SEQ - 1, DEC_SEQ - 1)
    u_prev = jnp.where(pos == 0, 0.0, pltpu.roll(u, 1, axis=0))
    u_next = jnp.where(pos == last, 0.0, pltpu.roll(u, TM - 1, axis=0))
    y_a = a_b * (u_prev * cw_ref[0:1, :] + u * cw_ref[1:2, :] + u_next * cw_ref[2:3, :])
    y_b = jnp.where(t < NTP, ybp_ref[...], ybs_ref[...])
    y = (_dot(y_a.astype(bf16), w_ref[0:CONV_CH, :].astype(bf16))
         + _dot(y_b, w_ref[CONV_CH:, :].astype(bf16)))
    o_ref[...] = _load_x(t, x_refs) + g1_ref[0] * y


def _mixout(proj, yb_p, yb_s, x, mod, conv_w, w_out, layer):
    li = layer // 2
    xs = x if isinstance(x, tuple) else (x,)
    return pl.pallas_call(
        functools.partial(_mixout_kernel, n_x=len(xs)),
        grid=(NT,),
        in_specs=_x_specs(x) + [
            pl.BlockSpec((TM, 3 * CONV_CH), lambda t: (t, 0)),
            pl.BlockSpec((TM, NA_WIDTH), lambda t: (jnp.minimum(t, NTP - 1), 0)),
            pl.BlockSpec((TM, NA_WIDTH), lambda t: (jnp.maximum(t - NTP, 0), 0)),
            _mod_spec(layer, 2),
            pl.BlockSpec((None, 3, CONV_CH), lambda t: (li, 0, 0)),
            pl.BlockSpec((None, D, D), lambda t: (li, 0, 0))],
        out_specs=pl.BlockSpec((TM, D), lambda t: (t, 0)),
        out_shape=jax.ShapeDtypeStruct((T, D), f32),
        compiler_params=_cp(("arbitrary",)),
    )(*xs, proj, yb_p, yb_s, mod, conv_w, w_out)


def _ffn_kernel(x_ref, g_ref, sc_ref, sh_ref, g2_ref, w1_ref, w3_ref, w2_ref, o_ref, h_ref, acc_ref):
    j = pl.program_id(1)

    @pl.when(j == 0)
    def _():
        h_ref[...] = _norm_mod(x_ref[...], g_ref[0], sc_ref[0], sh_ref[0]).astype(bf16)
        acc_ref[...] = jnp.zeros_like(acc_ref)

    h = h_ref[...]
    gate = jax.nn.silu(_dot(h, w1_ref[...].astype(bf16))) * _dot(h, w3_ref[...].astype(bf16))
    acc_ref[...] += _dot(gate.astype(bf16), w2_ref[...].astype(bf16))

    @pl.when(j == pl.num_programs(1) - 1)
    def _():
        o_ref[...] = x_ref[...] + g2_ref[0] * acc_ref[...]


def _dense_ffn(x, mod, norm_g, w1, w3, w2, layer):
    li = layer // 2
    tf = TF_DENSE
    return pl.pallas_call(
        _ffn_kernel,
        grid=(NT, FFN_DENSE // tf),
        in_specs=[pl.BlockSpec((TM, D), lambda i, j: (i, 0)),
                  pl.BlockSpec((1, 1, D), lambda i, j: (layer, 0, 0)),
                  _mod_spec(layer, 4), _mod_spec(layer, 3), _mod_spec(layer, 5),
                  pl.BlockSpec((None, D, tf), lambda i, j: (li, 0, j)),
                  pl.BlockSpec((None, D, tf), lambda i, j: (li, 0, j)),
                  pl.BlockSpec((None, tf, D), lambda i, j: (li, j, 0))],
        out_specs=pl.BlockSpec((TM, D), lambda i, j: (i, 0)),
        out_shape=jax.ShapeDtypeStruct((T, D), f32),
        scratch_shapes=[pltpu.VMEM((TM, D), bf16), pltpu.VMEM((TM, D), f32)],
        compiler_params=_cp(("arbitrary", "arbitrary")),
    )(x, norm_g, mod, mod, mod, w1, w3, w2)


GROUP_CH = D // FOURIER_GROUPS


def _dft_mats(n):
    k = np.arange(n, dtype=np.int64)
    ang = 2.0 * np.pi * ((k[:, None] * k[None, :]) % n).astype(np.float64) / n
    return np.cos(ang).astype(np.float32), np.sin(ang).astype(np.float32)


def _fourier_kernel(x_ref, g_ref, sc_ref, sh_ref, g1_ref, cs_ref, ss_ref, cl_ref, sl_ref, wf_ref,
                    o_ref, f_ref):
    t = pl.program_id(0)
    h = _norm_mod(x_ref[...], g_ref[0], sc_ref[0], sh_ref[0]).astype(bf16)
    cs = cs_ref[...].astype(bf16)
    ss = ss_ref[...].astype(bf16)
    ys, zs = [], []
    for g in range(FOURIER_GROUPS):
        hg = h[:, g * GROUP_CH:(g + 1) * GROUP_CH]
        ys.append(_dot(hg, cs))
        zs.append(_dot(hg, ss))
    y = jnp.concatenate(ys, axis=-1).astype(bf16)
    z = jnp.concatenate(zs, axis=-1).astype(bf16)

    @pl.when(t < NTP)
    def _():
        for s in range(TM // SEQ):
            rows = slice(s * SEQ, (s + 1) * SEQ)
            f = _dot(cs, y[rows]) - _dot(ss, z[rows])
            f_ref[rows, :] = (f * ((SEQ * GROUP_CH) ** -0.5)).astype(bf16)

    @pl.when(t >= NTP)
    def _():
        f = _dot(cl_ref[...].astype(bf16), y) - _dot(sl_ref[...].astype(bf16), z)
        f_ref[...] = (f * ((DEC_SEQ * GROUP_CH) ** -0.5)).astype(bf16)

    o_ref[...] = x_ref[...] + g1_ref[0] * _dot(f_ref[...], wf_ref[...].astype(bf16))


def _fourier(x, mod, norm_g, fourier_w, layer):
    li = layer // 2
    c_s, s_s = _dft_mats(SEQ)
    c_l, s_l = _dft_mats(DEC_SEQ)
    const = lambda shape: pl.BlockSpec(shape, lambda t: (0, 0))
    return pl.pallas_call(
        _fourier_kernel,
        grid=(NT,),
        in_specs=[pl.BlockSpec((TM, D), lambda t: (t, 0)),
                  pl.BlockSpec((1, 1, D), lambda t: (layer, 0, 0)),
                  _mod_spec(layer, 1), _mod_spec(layer, 0), _mod_spec(layer, 2),
                  const((SEQ, SEQ)), const((SEQ, SEQ)),
                  const((DEC_SEQ, DEC_SEQ)), const((DEC_SEQ, DEC_SEQ)),
                  pl.BlockSpec((None, D, D), lambda t: (li, 0, 0))],
        out_specs=pl.BlockSpec((TM, D), lambda t: (t, 0)),
        out_shape=jax.ShapeDtypeStruct((T, D), f32),
        scratch_shapes=[pltpu.VMEM((TM, D), bf16)],
        compiler_params=_cp(("arbitrary",)),
    )(x, norm_g, mod, mod, mod, jnp.asarray(c_s), jnp.asarray(s_s), jnp.asarray(c_l), jnp.asarray(s_l),
      fourier_w)


M_E0, M_E1, M_R0, M_R1, M_W0, M_W1 = range(6)
ROW_UNROLL = 8


META_ROWS = 8


def _router_kernel(x_ref, g_ref, sc_ref, sh_ref, rw_ref, rb_ref, h_ref, meta_ref, meta_t_ref, cnt_ref,
                   carry_ref):
    t = pl.program_id(0)

    @pl.when(t == 0)
    def _():
        carry_ref[...] = jnp.zeros_like(carry_ref)

    h = _norm_mod(x_ref[...], g_ref[0], sc_ref[0], sh_ref[0])
    h_ref[...] = h
    h_hi = h.astype(bf16)
    h_lo = (h - h_hi.astype(f32)).astype(bf16)
    hi_terms = _dot(h_hi, rw_ref[...])
    lo_term = _dot(h_lo, rw_ref[:, :LANES])
    logits = ((hi_terms[:, :LANES] + lo_term) + hi_terms[:, LANES:]) + rb_ref[...]
    lane = lax.broadcasted_iota(jnp.int32, (TM, LANES), 1)
    m1 = jnp.max(logits, axis=-1, keepdims=True)
    i1 = jnp.min(jnp.where(logits == m1, lane, LANES), axis=-1, keepdims=True)
    rest = jnp.where(lane == i1, -jnp.inf, logits)
    m2 = jnp.max(rest, axis=-1, keepdims=True)
    i2 = jnp.min(jnp.where(rest == m2, lane, LANES), axis=-1, keepdims=True)
    e = jnp.exp(m2 - m1)
    w0 = 1.0 / (1.0 + e)
    w1 = e / (1.0 + e)
    oh0 = (lane == i1).astype(f32)
    oh1 = (lane == i2).astype(f32)
    oh = oh0 + oh1
    row = lax.broadcasted_iota(jnp.int32, (TM, TM), 0)
    col = lax.broadcasted_iota(jnp.int32, (TM, TM), 1)
    before = jnp.where(col < row, 1.0, 0.0).astype(bf16)
    base = carry_ref[...] + _dot(before, oh.astype(bf16))
    r0 = jnp.sum(oh0 * base, axis=-1, keepdims=True)
    r1 = jnp.sum(oh1 * base, axis=-1, keepdims=True)
    carry_ref[...] += jnp.sum(oh, axis=0, keepdims=True)
    rec = jnp.zeros((TM, LANES), f32)
    for idx, val in ((M_E0, i1.astype(f32)), (M_E1, i2.astype(f32)), (M_R0, r0), (M_R1, r1),
                     (M_W0, w0), (M_W1, w1)):
        rec = jnp.where(lane == idx, val, rec)
    meta_ref[...] = rec
    meta_t_ref[...] = rec.T[:META_ROWS, :]
    cnt_ref[...] = carry_ref[...]


def _router(x, mod, norm_g, router_w, router_b, layer):
    li = layer // 2
    rw = jnp.pad(router_w[li], ((0, 0), (0, LANES - N_EXPERTS)))
    rw_hi = rw.astype(bf16)
    rw_lo = (rw - rw_hi.astype(f32)).astype(bf16)
    rw = jnp.concatenate([rw_hi, rw_lo], axis=1)
    rb = jnp.pad(router_b[li], (0, LANES - N_EXPERTS), constant_values=NEG_INF).reshape(1, LANES)
    return pl.pallas_call(
        _router_kernel,
        grid=(NT,),
        in_specs=[pl.BlockSpec((TM, D), lambda t: (t, 0)),
                  pl.BlockSpec((1, 1, D), lambda t: (layer, 0, 0)),
                  _mod_spec(layer, 4), _mod_spec(layer, 3),
                  pl.BlockSpec((D, 2 * LANES), lambda t: (0, 0)),
                  pl.BlockSpec((1, LANES), lambda t: (0, 0))],
        out_specs=[pl.BlockSpec((TM, D), lambda t: (t, 0)),
                   pl.BlockSpec((TM, LANES), lambda t: (t, 0)),
                   pl.BlockSpec((META_ROWS, TM), lambda t: (0, t)),
                   pl.BlockSpec((1, LANES), lambda t: (0, 0))],
        out_shape=[jax.ShapeDtypeStruct((T, D), f32),
                   jax.ShapeDtypeStruct((T, LANES), f32),
                   jax.ShapeDtypeStruct((META_ROWS, T), f32),
                   jax.ShapeDtypeStruct((1, LANES), f32)],
        scratch_shapes=[pltpu.VMEM((1, LANES), f32)],
        compiler_params=_cp(("arbitrary",)),
    )(x, norm_g, mod, mod, rw, rb)


def _row_copy(src_ref, src_row, dst_ref, dst_row, sem):
    return pltpu.make_async_copy(src_ref.at[pl.ds(src_row, 1)], dst_ref.at[pl.ds(dst_row, 1)], sem)


def _dispatch_kernel(dest_ref, h_ref, xg_in_ref, xg_ref, sem):
    del xg_in_ref

    def issue(g, c):
        for u in range(ROW_UNROLL):
            r = g * ROW_UNROLL + u
            for k in range(2):
                _row_copy(h_ref, r, xg_ref, dest_ref[0, 0, k * TM + r], sem).start()
        return c

    lax.fori_loop(0, TM // ROW_UNROLL, issue, 0)
    for k in range(2):
        pltpu.make_async_copy(h_ref, xg_ref.at[pl.ds(0, TM)], sem).wait()


def _dispatch(dest, h):
    xg0 = jnp.zeros((R_E, D), f32)
    return pl.pallas_call(
        _dispatch_kernel,
        grid=(NT,),
        in_specs=[pl.BlockSpec((1, 1, 2 * TM), lambda t: (t, 0, 0), memory_space=pltpu.SMEM),
                  pl.BlockSpec((TM, D), lambda t: (t, 0)),
                  pl.BlockSpec(memory_space=pl.ANY)],
        out_specs=pl.BlockSpec(memory_space=pl.ANY),
        out_shape=jax.ShapeDtypeStruct((R_E, D), f32),
        scratch_shapes=[pltpu.SemaphoreType.DMA(())],
        input_output_aliases={2: 0},
        compiler_params=_cp(("arbitrary",)),
    )(dest, h, xg0)


NJ_EXP = FFN_EXPERT // TF_EXP


SUB_ROWS = 256
N_SUB = TM_E // SUB_ROWS


def _expert_kernel(te_ref, nu_ref, nsub_ref, xg_ref, w1_ref, w3_ref, w2_ref, y_ref, h_ref,
                   w1b_ref, w3b_ref, w2b_ref):
    t = pl.program_id(0)
    j = pl.program_id(1)
    n_sub = nsub_ref[t]

    @pl.when(j == 0)
    def _():
        h_ref[...] = xg_ref[...].astype(bf16)
        y_ref[...] = jnp.zeros_like(y_ref)

    for b in range(N_SUB):
        @pl.when(b < n_sub)
        def _():
            if b == 0:
                w1, w3, w2 = (r[...].astype(bf16) for r in (w1_ref, w3_ref, w2_ref))
                w1b_ref[...] = w1
                w3b_ref[...] = w3
                w2b_ref[...] = w2
            else:
                w1, w3, w2 = w1b_ref[...], w3b_ref[...], w2b_ref[...]
            rows = slice(b * SUB_ROWS, (b + 1) * SUB_ROWS)
            h = h_ref[rows, :]
            gate = jax.nn.silu(_dot(h, w1)) * _dot(h, w3)
            y_ref[rows, :] += _dot(gate.astype(bf16), w2)


def _experts(tile_expert, n_used, n_sub, xg, w1, w3, w2, layer):
    li = layer // 2
    tf = TF_EXP
    nj = NJ_EXP

    def jj(t, j, nu):
        return jnp.where(t < nu[0], j, nj - 1)

    def tt(t, nu):
        return jnp.minimum(t, jnp.maximum(nu[0] - 1, 0))

    grid_spec = pltpu.PrefetchScalarGridSpec(
        num_scalar_prefetch=3,
        grid=(NT_E, nj),
        in_specs=[pl.BlockSpec((TM_E, D), lambda t, j, te, nu, ns: (tt(t, nu), 0)),
                  pl.BlockSpec((None, None, D, tf), lambda t, j, te, nu, ns: (li, te[t], 0, jj(t, j, nu))),
                  pl.BlockSpec((None, None, D, tf), lambda t, j, te, nu, ns: (li, te[t], 0, jj(t, j, nu))),
                  pl.BlockSpec((None, None, tf, D), lambda t, j, te, nu, ns: (li, te[t], jj(t, j, nu), 0))],
        out_specs=pl.BlockSpec((TM_E, D), lambda t, j, te, nu, ns: (t, 0)),
        scratch_shapes=[pltpu.VMEM((TM_E, D), bf16), pltpu.VMEM((D, tf), bf16), pltpu.VMEM((D, tf), bf16),
                        pltpu.VMEM((tf, D), bf16)],
    )
    return pl.pallas_call(
        _expert_kernel,
        grid_spec=grid_spec,
        out_shape=jax.ShapeDtypeStruct((R_E, D), f32),
        compiler_params=_cp(("arbitrary", "arbitrary")),
    )(tile_expert, n_used, n_sub, xg, w1, w3, w2)


def _combine_kernel(dest_ref, x_ref, g2_ref, meta_ref, y_ref, o_ref, buf_ref, sem):
    def issue(g, c):
        for u in range(ROW_UNROLL):
            r = g * ROW_UNROLL + u
            for k in range(2):
                _row_copy(y_ref, dest_ref[0, 0, k * TM + r], buf_ref.at[k], r, sem).start()
        return c

    lax.fori_loop(0, TM // ROW_UNROLL, issue, 0)
    for k in range(2):
        pltpu.make_async_copy(y_ref.at[pl.ds(0, TM)], buf_ref.at[k], sem).wait()
    w0 = meta_ref[:, M_W0:M_W0 + 1]
    w1 = meta_ref[:, M_W1:M_W1 + 1]
    o_ref[...] = x_ref[...] + g2_ref[0] * (w0 * buf_ref[0] + w1 * buf_ref[1])


def _combine(dest, x, mod, meta, y, layer):
    return pl.pallas_call(
        _combine_kernel,
        grid=(NT,),
        in_specs=[pl.BlockSpec((1, 1, 2 * TM), lambda t: (t, 0, 0), memory_space=pltpu.SMEM),
                  pl.BlockSpec((TM, D), lambda t: (t, 0)),
                  _mod_spec(layer, 5),
                  pl.BlockSpec((TM, LANES), lambda t: (t, 0)),
                  pl.BlockSpec(memory_space=pl.ANY)],
        out_specs=pl.BlockSpec((TM, D), lambda t: (t, 0)),
        out_shape=jax.ShapeDtypeStruct((T, D), f32),
        scratch_shapes=[pltpu.VMEM((2, TM, D), f32), pltpu.SemaphoreType.DMA(())],
        compiler_params=_cp(("arbitrary",)),
    )(dest, x, mod, meta, y)


def _moe(x, mod, norm_g, router_w, router_b, w1, w3, w2, layer):
    h_rows, meta, meta_t, counts = _router(x, mod, norm_g, router_w, router_b, layer)
    cnt = counts[0, :N_EXPERTS].astype(jnp.int32)
    padded = ((cnt + TM_E - 1) // TM_E) * TM_E
    ends = jnp.cumsum(padded)
    starts = ends - padded
    experts = meta_t[M_E0:M_E1 + 1].astype(jnp.int32)
    ranks = meta_t[M_R0:M_R1 + 1].astype(jnp.int32)
    start_of = functools.reduce(lambda acc, e: jnp.where(experts == e, starts[e], acc), range(N_EXPERTS), 0)
    dest = start_of + ranks
    dest = dest.reshape(2, NT, TM).transpose(1, 0, 2).reshape(NT, 1, 2 * TM)
    n_used = (ends[-1] // TM_E).astype(jnp.int32).reshape(1)
    tile_start = jnp.minimum(jnp.arange(NT_E, dtype=jnp.int32), n_used[0] - 1) * TM_E
    tile_expert = jnp.sum((tile_start[:, None] >= ends[None, :]).astype(jnp.int32), axis=1)
    tile_expert = jnp.minimum(tile_expert, N_EXPERTS - 1).astype(jnp.int32)
    tile_ids = jnp.arange(NT_E, dtype=jnp.int32)
    real_rows = jnp.clip((starts + cnt)[tile_expert] - tile_ids * TM_E, 0, TM_E)
    n_sub = jnp.where(tile_ids < n_used[0], (real_rows + SUB_ROWS - 1) // SUB_ROWS, 0).astype(jnp.int32)
    xg = _dispatch(dest, h_rows)
    y = _experts(tile_expert, n_used, n_sub, xg, w1, w3, w2, layer)
    return _combine(dest, x, mod, meta, y, layer)


def _final_kernel(x_ref, g_ref, o_ref):
    x = x_ref[...]
    o_ref[...] = (x * lax.rsqrt(jnp.mean(x * x, axis=-1, keepdims=True) + RMS_EPS)) * g_ref[...]


def _final_norm(x, final_g, first_tile, n_tiles):
    return pl.pallas_call(
        _final_kernel,
        grid=(n_tiles,),
        in_specs=[pl.BlockSpec((TM, D), lambda t: (first_tile + t, 0)),
                  pl.BlockSpec((1, D), lambda t: (0, 0))],
        out_specs=pl.BlockSpec((TM, D), lambda t: (t, 0)),
        out_shape=jax.ShapeDtypeStruct((n_tiles * TM, D), f32),
        compiler_params=_cp(("arbitrary",)),
    )(x, final_g.reshape(1, D))


def kernel(x_prompt, x_sample, c, cache_k, cache_v, c_ctx, ada_w, ada_b, norm1_g, norm2_g, w_in, conv_w, rpb,
           w_out, ffn_w1, ffn_w3, ffn_w2, fourier_w, router_w, router_b, moe_w1, moe_w3, moe_w2, final_g):
    cvecs = jnp.concatenate([c_ctx[None, :], c, jnp.zeros((MOD_ROWS - 1 - DEC_BATCH, D), f32)], axis=0)
    mod = _modulation(cvecs, ada_w, ada_b).reshape(DEPTH * MOD_ROWS * 6, 1, D)
    n1 = norm1_g.reshape(DEPTH, 1, D)
    n2 = norm2_g.reshape(DEPTH, 1, D)
    bias = _nbr_bias(rpb)
    cache_kt = jnp.swapaxes(cache_k, -1, -2)
    cache_vt = jnp.swapaxes(cache_v, -1, -2)
    x = (x_prompt.reshape(T_P, D), x_sample.reshape(T_S, D))
    caches = [jnp.zeros((BATCH, N_EVEN, NA_HEADS, HEAD_DIM, SEQ), f32) for _ in range(2)]
    for layer in range(DEPTH):
        li = layer // 2
        if layer % 2 == 0:
            proj, *caches = _inproj(x, mod, n1, w_in, layer, caches)
            yb_p = _ctx_attention(proj)
            yb_s = _nbr_attention(proj, cache_kt, cache_vt, bias, li)
            x = _mixout(proj, yb_p, yb_s, x, mod, conv_w, w_out, layer)
            x = _dense_ffn(x, mod, n2, ffn_w1, ffn_w3, ffn_w2, layer)
        else:
            x = _fourier(x, mod, n1, fourier_w, layer)
            x = _moe(x, mod, n2, router_w, router_b, moe_w1, moe_w3, moe_w2, layer)
    y_prompt = _final_norm(x, final_g, 0, NTP).reshape(BATCH, SEQ, D)
    y_sample = _final_norm(x, final_g, NTP, NT - NTP).reshape(DEC_BATCH, DEC_SEQ, D)
    new_kt, new_vt = caches
    return (y_prompt, y_sample, jnp.swapaxes(new_kt, -1, -2), jnp.swapaxes(new_vt, -1, -2))
```

```python
import functools
from typing import Any, NamedTuple

import numpy as np
import jax
import jax.numpy as jnp
from jax import lax
from jax.experimental import pallas as pl
from jax.experimental.pallas import tpu as pltpu

f32 = jnp.float32
bf16 = jnp.bfloat16

D = 1024
BATCH = 16
SEQ = 256
DEPTH = 4
DEC_BATCH = 8
DEC_SEQ = 1024
PAST_LEN = 512
GRID_W = 64
CONV_CH = 512
NA_HEADS = 8
HEAD_DIM = 64
NA_WIDTH = 512
WIN_ROWS = 8
WIN_COLS = 16
IN_WIDTH = 3072
FOURIER_GROUPS = 4
FFN_DENSE = 2816
N_EXPERTS = 8
FFN_EXPERT = 3584
N_EVEN = 2
RMS_EPS = 1e-6
NEG_INF = -1e30
SCALE = HEAD_DIM ** -0.5

LANES = 128
TM = 1024
T_P = BATCH * SEQ
T_S = DEC_BATCH * DEC_SEQ
T = T_P + T_S
NT = T // TM
NTP = T_P // TM
MOD_ROWS = 16
GRID_ROWS = DEC_SEQ // GRID_W
CHUNK_ROWS = 4
N_CHUNKS = GRID_ROWS // CHUNK_ROWS
CHUNK_Q = CHUNK_ROWS * GRID_W
KWIN_ROWS = 12
KWIN = KWIN_ROWS * GRID_W
KB0 = (0, 0, 4, 4)
CHUNKS_PER_GROUP = 4
TF_DENSE = 256
TF_EXP = 512
TM_E = 1024
N_ASSIGN = 2 * T
NT_E = N_ASSIGN // TM_E + N_EXPERTS
R_E = NT_E * TM_E
VMEM_LIMIT = 56 * 1024 * 1024


def _cp(sem, vmem=VMEM_LIMIT):
    return pltpu.CompilerParams(dimension_semantics=sem, vmem_limit_bytes=vmem)


def _mod_index(t):
    return jnp.where(t < NTP, 0, t - (NTP - 1))


def _mod_spec(layer, part):
    def index(t, *_):
        return ((layer * MOD_ROWS + _mod_index(t)) * 6 + part, 0, 0)
    return pl.BlockSpec((1, 1, D), index)


def _norm_mod(x, g, sc, sh):
    y = x * lax.rsqrt(jnp.mean(x * x, axis=-1, keepdims=True) + RMS_EPS)
    return (y * g) * (1 + sc) + sh


def _dot(a, b):
    return jnp.dot(a, b, preferred_element_type=f32)


def _dot_nt(a, b):
    return lax.dot_general(a, b, (((1,), (1,)), ((), ())), preferred_element_type=f32)


def _mod_kernel(cv_ref, w_ref, b_ref, o_ref):
    s = jax.nn.silu(cv_ref[...]).astype(bf16)
    o_ref[0] = _dot(s, w_ref[0].astype(bf16)) + b_ref[0]


def _modulation(cvecs, ada_w, ada_b):
    tn = 1536
    return pl.pallas_call(
        _mod_kernel,
        grid=(DEPTH, 6 * D // tn),
        in_specs=[pl.BlockSpec((MOD_ROWS, D), lambda l, j: (0, 0)),
                  pl.BlockSpec((1, D, tn), lambda l, j: (l, 0, j)),
                  pl.BlockSpec((1, 1, tn), lambda l, j: (l, 0, j))],
        out_specs=pl.BlockSpec((1, MOD_ROWS, tn), lambda l, j: (l, 0, j)),
        out_shape=jax.ShapeDtypeStruct((DEPTH, MOD_ROWS, 6 * D), f32),
        compiler_params=_cp(("arbitrary", "arbitrary")),
    )(cvecs, ada_w, ada_b.reshape(DEPTH, 1, 6 * D))


TN_IN = IN_WIDTH // 2


def _x_specs(x):
    if isinstance(x, tuple):
        return [pl.BlockSpec((TM, D), lambda i, *_: (jnp.minimum(i, NTP - 1), 0)),
                pl.BlockSpec((TM, D), lambda i, *_: (jnp.maximum(i - NTP, 0), 0))]
    return [pl.BlockSpec((TM, D), lambda i, *_: (i, 0))]


def _load_x(i, x_refs):
    if len(x_refs) == 2:
        return jnp.where(i < NTP, x_refs[0][...], x_refs[1][...])
    return x_refs[0][...]


def _inproj_kernel(*refs, n_x, n_alias):
    x_refs, refs = refs[:n_x], refs[n_x + n_alias:]
    g_ref, sc_ref, sh_ref, w_ref, proj_ref, kt_ref, vt_ref, h_ref = refs
    i = pl.program_id(0)
    j = pl.program_id(1)

    @pl.when(j == 0)
    def _():
        h_ref[...] = _norm_mod(_load_x(i, x_refs), g_ref[0], sc_ref[0], sh_ref[0]).astype(bf16)

    acc = _dot(h_ref[...], w_ref[...].astype(bf16))
    proj_ref[...] = acc.astype(bf16)

    @pl.when((j == 1) & (i < NTP))
    def _():
        for ref, col0 in ((kt_ref, NA_WIDTH), (vt_ref, 2 * NA_WIDTH)):
            for s in range(TM // SEQ):
                t = acc[s * SEQ:(s + 1) * SEQ, col0:col0 + NA_WIDTH].T
                for hd in range(NA_HEADS):
                    ref[s, hd] = t[hd * HEAD_DIM:(hd + 1) * HEAD_DIM, :]


def _inproj(x, mod, norm_g, w_in, layer, caches):
    li = layer // 2
    tn = TN_IN
    seqs = TM // SEQ
    xs = x if isinstance(x, tuple) else (x,)
    cache_spec = pl.BlockSpec((seqs, None, NA_HEADS, HEAD_DIM, SEQ),
                              lambda i, j: (jnp.minimum(i, NTP - 1), li, 0, 0, 0))
    cache_shape = jax.ShapeDtypeStruct((BATCH, N_EVEN, NA_HEADS, HEAD_DIM, SEQ), f32)
    aliased = tuple(caches)
    return pl.pallas_call(
        functools.partial(_inproj_kernel, n_x=len(xs), n_alias=len(aliased)),
        grid=(NT, IN_WIDTH // tn),
        in_specs=_x_specs(x) + [pl.BlockSpec(memory_space=pl.ANY)] * len(aliased) + [
            pl.BlockSpec((1, 1, D), lambda i, j: (layer, 0, 0)),
            _mod_spec(layer, 1), _mod_spec(layer, 0),
            pl.BlockSpec((None, D, tn), lambda i, j: (li, 0, j))],
        out_specs=[pl.BlockSpec((TM, tn), lambda i, j: (i, j)), cache_spec, cache_spec],
        out_shape=[jax.ShapeDtypeStruct((T, IN_WIDTH), bf16), cache_shape, cache_shape],
        scratch_shapes=[pltpu.VMEM((TM, D), bf16)],
        input_output_aliases={len(xs) + n: 1 + n for n in range(len(aliased))},
        compiler_params=_cp(("arbitrary", "arbitrary")),
    )(*xs, *aliased, norm_g, mod, mod, w_in)


HEADS_PER_STEP = LANES // HEAD_DIM


class KeyValues(NamedTuple):
    k: jax.Array
    v: jax.Array
    bias: Any = None
    feature_major: bool = False


def _pair_attention(problems):
    first = lax.broadcasted_iota(jnp.int32, (1, LANES), 1) < HEAD_DIM
    mine = (first, jnp.logical_not(first))
    first_t = lax.broadcasted_iota(jnp.int32, (LANES, 1), 0) < HEAD_DIM
    mine_t = (first_t, jnp.logical_not(first_t))
    scores = []
    for q, parts in problems:
        q = q * SCALE
        for hh in range(HEADS_PER_STEP):
            qh = jnp.where(mine[hh], q, 0)
            ss = []
            for part in parts:
                s = _dot(qh, part.k) if part.feature_major else _dot_nt(qh, part.k)
                ss.append(s if part.bias is None else s + part.bias[hh])
            scores.append(ss)
    probs = []
    for ss in scores:
        m = functools.reduce(jnp.maximum, [jnp.max(s, axis=-1, keepdims=True) for s in ss])
        probs.append([jnp.exp((s - m).astype(bf16)) for s in ss])

    def p_times_v(p, part, hh):
        if part.feature_major:
            return _dot_nt(p, jnp.where(mine_t[hh], part.v, 1))
        return _dot(p, jnp.where(mine[hh], part.v, 1))

    results = []
    for n, (_, parts) in enumerate(problems):
        outs = []
        for hh in range(HEADS_PER_STEP):
            pv = functools.reduce(jnp.add, [p_times_v(p, part, hh)
                                            for p, part in zip(probs[n * HEADS_PER_STEP + hh], parts)])
            denom = pv[:, HEAD_DIM:HEAD_DIM + 1] if hh == 0 else pv[:, 0:1]
            outs.append(pv / denom)
        results.append(jnp.where(first, outs[0], outs[1]))
    return results


def _ctx_attn_kernel(q_ref, k_ref, v_ref, o_ref):
    cols = [slice(hp * LANES, (hp + 1) * LANES) for hp in range(NA_HEADS // HEADS_PER_STEP)]
    outs = _pair_attention([(q_ref[:, sl], [KeyValues(k_ref[:, sl], v_ref[:, sl])]) for sl in cols])
    for sl, o in zip(cols, outs):
        o_ref[:, sl] = o.astype(bf16)


def _ctx_attention(proj):
    col = IN_WIDTH // NA_WIDTH - 3
    return pl.pallas_call(
        _ctx_attn_kernel,
        grid=(BATCH,),
        in_specs=[pl.BlockSpec((SEQ, NA_WIDTH), lambda b: (b, col)),
                  pl.BlockSpec((SEQ, NA_WIDTH), lambda b: (b, col + 1)),
                  pl.BlockSpec((SEQ, NA_WIDTH), lambda b: (b, col + 2))],
        out_specs=pl.BlockSpec((SEQ, NA_WIDTH), lambda b: (b, 0)),
        out_shape=jax.ShapeDtypeStruct((T_P, NA_WIDTH), bf16),
        compiler_params=_cp(("arbitrary",)),
    )(proj, proj, proj)


def _win_start(qr):
    return min(max(qr - WIN_ROWS // 2, 0), GRID_ROWS - WIN_ROWS)


N_ROFF = 2 * WIN_ROWS - 1
N_COFF = 2 * WIN_COLS - 1


def _bias_kernel(rpb_ref, o_ref):
    base = (pl.program_id(0) * NA_HEADS + pl.program_id(1)) * (N_ROFF * N_COFF)
    qc = lax.broadcasted_iota(jnp.int32, (GRID_W, GRID_W), 0)
    kc = lax.broadcasted_iota(jnp.int32, (GRID_W, GRID_W), 1)
    coff = kc - qc + (WIN_COLS - 1)
    cs = jnp.clip(qc - WIN_COLS // 2, 0, GRID_W - WIN_COLS)
    valid = (kc >= cs) & (kc < cs + WIN_COLS)
    neg = jnp.full((GRID_W, GRID_W), NEG_INF, f32)
    blocks = []
    for a in range(N_ROFF):
        t = jnp.zeros((GRID_W, GRID_W), f32)
        for b in range(N_COFF):
            t = jnp.where(coff == b, rpb_ref[base + a * N_COFF + b], t)
        blocks.append(jnp.where(valid, t, neg))
    for c in range(N_CHUNKS):
        for ql in range(CHUNK_ROWS):
            qr = c * CHUNK_ROWS + ql
            st = _win_start(qr)
            for kl in range(KWIN_ROWS):
                kr = KB0[c] + kl
                blk = blocks[kr - qr + WIN_ROWS - 1] if st <= kr < st + WIN_ROWS else neg
                o_ref[c, ql * GRID_W:(ql + 1) * GRID_W, kl * GRID_W:(kl + 1) * GRID_W] = blk


def _nbr_bias(rpb):
    return pl.pallas_call(
        _bias_kernel,
        grid=(N_EVEN, NA_HEADS),
        in_specs=[pl.BlockSpec(memory_space=pltpu.SMEM)],
        out_specs=pl.BlockSpec((None, None, N_CHUNKS, CHUNK_Q, KWIN), lambda i, h: (i, h, 0, 0, 0)),
        out_shape=jax.ShapeDtypeStruct((N_EVEN, NA_HEADS, N_CHUNKS, CHUNK_Q, KWIN), f32),
        compiler_params=_cp(("arbitrary", "arbitrary")),
    )(rpb.reshape(-1))


def _nbr_attn_kernel(q_ref, k_ref, v_ref, kc_ref, vc_ref, bias_ref, o_ref):
    ctx = KeyValues(kc_ref[...].reshape(LANES, PAST_LEN).astype(bf16),
                    vc_ref[...].reshape(LANES, PAST_LEN).astype(bf16), feature_major=True)
    for c0 in range(0, N_CHUNKS, CHUNKS_PER_GROUP):
        problems = []
        for c in range(c0, c0 + CHUNKS_PER_GROUP):
            rows = slice(c * CHUNK_Q, (c + 1) * CHUNK_Q)
            win = slice(KB0[c] * GRID_W, KB0[c] * GRID_W + KWIN)
            bias = [bias_ref[hh, c] for hh in range(HEADS_PER_STEP)]
            problems.append((q_ref[rows, :], [KeyValues(k_ref[win, :], v_ref[win, :], bias), ctx]))
        for c, o in zip(range(c0, c0 + CHUNKS_PER_GROUP), _pair_attention(problems)):
            o_ref[c * CHUNK_Q:(c + 1) * CHUNK_Q, :] = o.astype(bf16)


def _nbr_attention(proj, cache_k, cache_v, bias, li):
    qcol = 3 * CONV_CH // LANES
    ncol = NA_WIDTH // LANES
    hp_steps = NA_HEADS // HEADS_PER_STEP

    def col_spec(which):
        return pl.BlockSpec((DEC_SEQ, LANES), lambda hp, b: (NTP + b, qcol + which * ncol + hp))

    ctx_spec = pl.BlockSpec((None, None, HEADS_PER_STEP, HEAD_DIM, PAST_LEN),
                            lambda hp, b: (b, li, hp, 0, 0))
    return pl.pallas_call(
        _nbr_attn_kernel,
        grid=(hp_steps, DEC_BATCH),
        in_specs=[col_spec(0), col_spec(1), col_spec(2), ctx_spec, ctx_spec,
                  pl.BlockSpec((None, HEADS_PER_STEP, N_CHUNKS, CHUNK_Q, KWIN),
                               lambda hp, b: (li, hp, 0, 0, 0))],
        out_specs=pl.BlockSpec((DEC_SEQ, LANES), lambda hp, b: (b, hp)),
        out_shape=jax.ShapeDtypeStruct((T_S, NA_WIDTH), bf16),
        compiler_params=_cp(("arbitrary", "arbitrary")),
    )(proj, proj, proj, cache_k, cache_v, bias)


def _mixout_kernel(*refs, n_x):
    x_refs, (a_ref, ybp_ref, ybs_ref, g1_ref, cw_ref, w_ref, o_ref) = refs[:n_x], refs[n_x:]
    t = pl.program_id(0)
    a_b = a_ref[:, 0:CONV_CH].astype(f32)
    a_c = a_ref[:, CONV_CH:2 * CONV_CH].astype(f32)
    a_x = a_ref[:, 2 * CONV_CH:3 * CONV_CH].astype(f32)
    u = a_c * a_x
    r = lax.broadcasted_iota(jnp.int32, (TM, 1), 0)
    pos = jnp.where(t < NTP, r % SEQ, r)
    last = jnp.where(t < NTP, SEQ - 1, DEC_SEQ - 1)
    u_prev = jnp.where(pos == 0, 0.0, pltpu.roll(u, 1, axis=0))
    u_next = jnp.where(pos == last, 0.0, pltpu.roll(u, TM - 1, axis=0))
    y_a = a_b * (u_prev * cw_ref[0:1, :] + u * cw_ref[1:2, :] + u_next * cw_ref[2:3, :])
    y_b = jnp.where(t < NTP, ybp_ref[...], ybs_ref[...])
    y = (_dot(y_a.astype(bf16), w_ref[0:CONV_CH, :].astype(bf16))
         + _dot(y_b, w_ref[CONV_CH:, :].astype(bf16)))
    o_ref[...] = _load_x(t, x_refs) + g1_ref[0] * y


def _mixout(proj, yb_p, yb_s, x, mod, conv_w, w_out, layer):
    li = layer // 2
    xs = x if isinstance(x, tuple) else (x,)
    return pl.pallas_call(
        functools.partial(_mixout_kernel, n_x=len(xs)),
        grid=(NT,),
        in_specs=_x_specs(x) + [
            pl.BlockSpec((TM, 3 * CONV_CH), lambda t: (t, 0)),
            pl.BlockSpec((TM, NA_WIDTH), lambda t: (jnp.minimum(t, NTP - 1), 0)),
            pl.BlockSpec((TM, NA_WIDTH), lambda t: (jnp.maximum(t - NTP, 0), 0)),
            _mod_spec(layer, 2),
            pl.BlockSpec((None, 3, CONV_CH), lambda t: (li, 0, 0)),
            pl.BlockSpec((None, D, D), lambda t: (li, 0, 0))],
        out_specs=pl.BlockSpec((TM, D), lambda t: (t, 0)),
        out_shape=jax.ShapeDtypeStruct((T, D), f32),
        compiler_params=_cp(("arbitrary",)),
    )(*xs, proj, yb_p, yb_s, mod, conv_w, w_out)


def _ffn_kernel(x_ref, g_ref, sc_ref, sh_ref, g2_ref, w1_ref, w3_ref, w2_ref, o_ref, h_ref, acc_ref):
    j = pl.program_id(1)

    @pl.when(j == 0)
    def _():
        h_ref[...] = _norm_mod(x_ref[...], g_ref[0], sc_ref[0], sh_ref[0]).astype(bf16)
        acc_ref[...] = jnp.zeros_like(acc_ref)

    h = h_ref[...]
    gate = jax.nn.silu(_dot(h, w1_ref[...].astype(bf16))) * _dot(h, w3_ref[...].astype(bf16))
    acc_ref[...] += _dot(gate.astype(bf16), w2_ref[...].astype(bf16))

    @pl.when(j == pl.num_programs(1) - 1)
    def _():
        o_ref[...] = x_ref[...] + g2_ref[0] * acc_ref[...]


def _dense_ffn(x, mod, norm_g, w1, w3, w2, layer):
    li = layer // 2
    tf = TF_DENSE
    return pl.pallas_call(
        _ffn_kernel,
        grid=(NT, FFN_DENSE // tf),
        in_specs=[pl.BlockSpec((TM, D), lambda i, j: (i, 0)),
                  pl.BlockSpec((1, 1, D), lambda i, j: (layer, 0, 0)),
                  _mod_spec(layer, 4), _mod_spec(layer, 3), _mod_spec(layer, 5),
                  pl.BlockSpec((None, D, tf), lambda i, j: (li, 0, j)),
                  pl.BlockSpec((None, D, tf), lambda i, j: (li, 0, j)),
                  pl.BlockSpec((None, tf, D), lambda i, j: (li, j, 0))],
        out_specs=pl.BlockSpec((TM, D), lambda i, j: (i, 0)),
        out_shape=jax.ShapeDtypeStruct((T, D), f32),
        scratch_shapes=[pltpu.VMEM((TM, D), bf16), pltpu.VMEM((TM, D), f32)],
        compiler_params=_cp(("arbitrary", "arbitrary")),
    )(x, norm_g, mod, mod, mod, w1, w3, w2)


GROUP_CH = D // FOURIER_GROUPS


def _dft_mats(n):
    k = np.arange(n, dtype=np.int64)
    ang = 2.0 * np.pi * ((k[:, None] * k[None, :]) % n).astype(np.float64) / n
    return np.cos(ang).astype(np.float32), np.sin(ang).astype(np.float32)


def _fourier_kernel(x_ref, g_ref, sc_ref, sh_ref, g1_ref, cs_ref, ss_ref, cl_ref, sl_ref, wf_ref,
                    o_ref, f_ref):
    t = pl.program_id(0)
    h = _norm_mod(x_ref[...], g_ref[0], sc_ref[0], sh_ref[0]).astype(bf16)
    cs = cs_ref[...].astype(bf16)
    ss = ss_ref[...].astype(bf16)
    ys, zs = [], []
    for g in range(FOURIER_GROUPS):
        hg = h[:, g * GROUP_CH:(g + 1) * GROUP_CH]
        ys.append(_dot(hg, cs))
        zs.append(_dot(hg, ss))
    y = jnp.concatenate(ys, axis=-1).astype(bf16)
    z = jnp.concatenate(zs, axis=-1).astype(bf16)

    @pl.when(t < NTP)
    def _():
        for s in range(TM // SEQ):
            rows = slice(s * SEQ, (s + 1) * SEQ)
            f = _dot(cs, y[rows]) - _dot(ss, z[rows])
            f_ref[rows, :] = (f * ((SEQ * GROUP_CH) ** -0.5)).astype(bf16)

    @pl.when(t >= NTP)
    def _():
        f = _dot(cl_ref[...].astype(bf16), y) - _dot(sl_ref[...].astype(bf16), z)
        f_ref[...] = (f * ((DEC_SEQ * GROUP_CH) ** -0.5)).astype(bf16)

    o_ref[...] = x_ref[...] + g1_ref[0] * _dot(f_ref[...], wf_ref[...].astype(bf16))


def _fourier(x, mod, norm_g, fourier_w, layer):
    li = layer // 2
    c_s, s_s = _dft_mats(SEQ)
    c_l, s_l = _dft_mats(DEC_SEQ)
    const = lambda shape: pl.BlockSpec(shape, lambda t: (0, 0))
    return pl.pallas_call(
        _fourier_kernel,
        grid=(NT,),
        in_specs=[pl.BlockSpec((TM, D), lambda t: (t, 0)),
                  pl.BlockSpec((1, 1, D), lambda t: (layer, 0, 0)),
                  _mod_spec(layer, 1), _mod_spec(layer, 0), _mod_spec(layer, 2),
                  const((SEQ, SEQ)), const((SEQ, SEQ)),
                  const((DEC_SEQ, DEC_SEQ)), const((DEC_SEQ, DEC_SEQ)),
                  pl.BlockSpec((None, D, D), lambda t: (li, 0, 0))],
        out_specs=pl.BlockSpec((TM, D), lambda t: (t, 0)),
        out_shape=jax.ShapeDtypeStruct((T, D), f32),
        scratch_shapes=[pltpu.VMEM((TM, D), bf16)],
        compiler_params=_cp(("arbitrary",)),
    )(x, norm_g, mod, mod, mod, jnp.asarray(c_s), jnp.asarray(s_s), jnp.asarray(c_l), jnp.asarray(s_l),
      fourier_w)


M_E0, M_E1, M_R0, M_R1, M_W0, M_W1 = range(6)
ROW_UNROLL = 8


META_ROWS = 8


def _router_kernel(x_ref, g_ref, sc_ref, sh_ref, rw_ref, rb_ref, h_ref, meta_ref, meta_t_ref, cnt_ref,
                   carry_ref):
    t = pl.program_id(0)

    @pl.when(t == 0)
    def _():
        carry_ref[...] = jnp.zeros_like(carry_ref)

    h = _norm_mod(x_ref[...], g_ref[0], sc_ref[0], sh_ref[0])
    h_ref[...] = h
    h_hi = h.astype(bf16)
    h_lo = (h - h_hi.astype(f32)).astype(bf16)
    hi_terms = _dot(h_hi, rw_ref[...])
    lo_term = _dot(h_lo, rw_ref[:, :LANES])
    logits = ((hi_terms[:, :LANES] + lo_term) + hi_terms[:, LANES:]) + rb_ref[...]
    lane = lax.broadcasted_iota(jnp.int32, (TM, LANES), 1)
    m1 = jnp.max(logits, axis=-1, keepdims=True)
    i1 = jnp.min(jnp.where(logits == m1, lane, LANES), axis=-1, keepdims=True)
    rest = jnp.where(lane == i1, -jnp.inf, logits)
    m2 = jnp.max(rest, axis=-1, keepdims=True)
    i2 = jnp.min(jnp.where(rest == m2, lane, LANES), axis=-1, keepdims=True)
    e = jnp.exp(m2 - m1)
    w0 = 1.0 / (1.0 + e)
    w1 = e / (1.0 + e)
    oh0 = (lane == i1).astype(f32)
    oh1 = (lane == i2).astype(f32)
    oh = oh0 + oh1
    row = lax.broadcasted_iota(jnp.int32, (TM, TM), 0)
    col = lax.broadcasted_iota(jnp.int32, (TM, TM), 1)
    before = jnp.where(col < row, 1.0, 0.0).astype(bf16)
    base = carry_ref[...] + _dot(before, oh.astype(bf16))
    r0 = jnp.sum(oh0 * base, axis=-1, keepdims=True)
    r1 = jnp.sum(oh1 * base, axis=-1, keepdims=True)
    carry_ref[...] += jnp.sum(oh, axis=0, keepdims=True)
    rec = jnp.zeros((TM, LANES), f32)
    for idx, val in ((M_E0, i1.astype(f32)), (M_E1, i2.astype(f32)), (M_R0, r0), (M_R1, r1),
                     (M_W0, w0), (M_W1, w1)):
        rec = jnp.where(lane == idx, val, rec)
    meta_ref[...] = rec
    meta_t_ref[...] = rec.T[:META_ROWS, :]
    cnt_ref[...] = carry_ref[...]


def _router(x, mod, norm_g, router_w, router_b, layer):
    li = layer // 2
    rw = jnp.pad(router_w[li], ((0, 0), (0, LANES - N_EXPERTS)))
    rw_hi = rw.astype(bf16)
    rw_lo = (rw - rw_hi.astype(f32)).astype(bf16)
    rw = jnp.concatenate([rw_hi, rw_lo], axis=1)
    rb = jnp.pad(router_b[li], (0, LANES - N_EXPERTS), constant_values=NEG_INF).reshape(1, LANES)
    return pl.pallas_call(
        _router_kernel,
        grid=(NT,),
        in_specs=[pl.BlockSpec((TM, D), lambda t: (t, 0)),
                  pl.BlockSpec((1, 1, D), lambda t: (layer, 0, 0)),
                  _mod_spec(layer, 4), _mod_spec(layer, 3),
                  pl.BlockSpec((D, 2 * LANES), lambda t: (0, 0)),
                  pl.BlockSpec((1, LANES), lambda t: (0, 0))],
        out_specs=[pl.BlockSpec((TM, D), lambda t: (t, 0)),
                   pl.BlockSpec((TM, LANES), lambda t: (t, 0)),
                   pl.BlockSpec((META_ROWS, TM), lambda t: (0, t)),
                   pl.BlockSpec((1, LANES), lambda t: (0, 0))],
        out_shape=[jax.ShapeDtypeStruct((T, D), f32),
                   jax.ShapeDtypeStruct((T, LANES), f32),
                   jax.ShapeDtypeStruct((META_ROWS, T), f32),
                   jax.ShapeDtypeStruct((1, LANES), f32)],
        scratch_shapes=[pltpu.VMEM((1, LANES), f32)],
        compiler_params=_cp(("arbitrary",)),
    )(x, norm_g, mod, mod, rw, rb)


def _row_copy(src_ref, src_row, dst_ref, dst_row, sem):
    return pltpu.make_async_copy(src_ref.at[pl.ds(src_row, 1)], dst_ref.at[pl.ds(dst_row, 1)], sem)


def _dispatch_kernel(dest_ref, h_ref, xg_in_ref, xg_ref, sem):
    del xg_in_ref

    def issue(g, c):
        for u in range(ROW_UNROLL):
            r = g * ROW_UNROLL + u
            for k in range(2):
                _row_copy(h_ref, r, xg_ref, dest_ref[0, 0, k * TM + r], sem).start()
        return c

    lax.fori_loop(0, TM // ROW_UNROLL, issue, 0)
    for k in range(2):
        pltpu.make_async_copy(h_ref, xg_ref.at[pl.ds(0, TM)], sem).wait()


def _dispatch(dest, h):
    xg0 = jnp.zeros((R_E, D), f32)
    return pl.pallas_call(
        _dispatch_kernel,
        grid=(NT,),
        in_specs=[pl.BlockSpec((1, 1, 2 * TM), lambda t: (t, 0, 0), memory_space=pltpu.SMEM),
                  pl.BlockSpec((TM, D), lambda t: (t, 0)),
                  pl.BlockSpec(memory_space=pl.ANY)],
        out_specs=pl.BlockSpec(memory_space=pl.ANY),
        out_shape=jax.ShapeDtypeStruct((R_E, D), f32),
        scratch_shapes=[pltpu.SemaphoreType.DMA(())],
        input_output_aliases={2: 0},
        compiler_params=_cp(("arbitrary",)),
    )(dest, h, xg0)


NJ_EXP = FFN_EXPERT // TF_EXP


def _expert_kernel(te_ref, nu_ref, xg_ref, w1_ref, w3_ref, w2_ref, y_ref, h_ref):
    t = pl.program_id(0)
    j = pl.program_id(1)
    used = t < nu_ref[0]

    @pl.when(j == 0)
    def _():
        h_ref[...] = xg_ref[...].astype(bf16)
        y_ref[...] = jnp.zeros_like(y_ref)

    @pl.when(used)
    def _():
        h = h_ref[...]
        gate = jax.nn.silu(_dot(h, w1_ref[...].astype(bf16))) * _dot(h, w3_ref[...].astype(bf16))
        y_ref[...] += _dot(gate.astype(bf16), w2_ref[...].astype(bf16))


def _experts(tile_expert, n_used, xg, w1, w3, w2, layer):
    li = layer // 2
    tf = TF_EXP
    nj = NJ_EXP

    def jj(t, j, nu):
        return jnp.where(t < nu[0], j, nj - 1)

    def tt(t, nu):
        return jnp.minimum(t, jnp.maximum(nu[0] - 1, 0))

    grid_spec = pltpu.PrefetchScalarGridSpec(
        num_scalar_prefetch=2,
        grid=(NT_E, nj),
        in_specs=[pl.BlockSpec((TM_E, D), lambda t, j, te, nu: (tt(t, nu), 0)),
                  pl.BlockSpec((None, None, D, tf), lambda t, j, te, nu: (li, te[t], 0, jj(t, j, nu))),
                  pl.BlockSpec((None, None, D, tf), lambda t, j, te, nu: (li, te[t], 0, jj(t, j, nu))),
                  pl.BlockSpec((None, None, tf, D), lambda t, j, te, nu: (li, te[t], jj(t, j, nu), 0))],
        out_specs=pl.BlockSpec((TM_E, D), lambda t, j, te, nu: (t, 0)),
        scratch_shapes=[pltpu.VMEM((TM_E, D), bf16)],
    )
    return pl.pallas_call(
        _expert_kernel,
        grid_spec=grid_spec,
        out_shape=jax.ShapeDtypeStruct((R_E, D), f32),
        compiler_params=_cp(("arbitrary", "arbitrary")),
    )(tile_expert, n_used, xg, w1, w3, w2)


def _combine_kernel(dest_ref, x_ref, g2_ref, meta_ref, y_ref, *rest, final):
    if final:
        fg_ref, op_ref, os_ref, buf_ref, sem = rest
    else:
        o_ref, buf_ref, sem = rest

    def issue(g, c):
        for u in range(ROW_UNROLL):
            r = g * ROW_UNROLL + u
            for k in range(2):
                _row_copy(y_ref, dest_ref[0, 0, k * TM + r], buf_ref.at[k], r, sem).start()
        return c

    lax.fori_loop(0, TM // ROW_UNROLL, issue, 0)
    for k in range(2):
        pltpu.make_async_copy(y_ref.at[pl.ds(0, TM)], buf_ref.at[k], sem).wait()
    w0 = meta_ref[:, M_W0:M_W0 + 1]
    w1 = meta_ref[:, M_W1:M_W1 + 1]
    out = x_ref[...] + g2_ref[0] * (w0 * buf_ref[0] + w1 * buf_ref[1])
    if not final:
        o_ref[...] = out
        return
    normed = (out * lax.rsqrt(jnp.mean(out * out, axis=-1, keepdims=True) + RMS_EPS)) * fg_ref[...]
    t = pl.program_id(0)

    @pl.when(t < NTP)
    def _():
        op_ref[...] = normed

    @pl.when(t >= NTP)
    def _():
        os_ref[...] = normed


def _combine(dest, x, mod, meta, y, layer, final_g=None):
    final = final_g is not None
    in_specs = [pl.BlockSpec((1, 1, 2 * TM), lambda t: (t, 0, 0), memory_space=pltpu.SMEM),
                pl.BlockSpec((TM, D), lambda t: (t, 0)),
                _mod_spec(layer, 5),
                pl.BlockSpec((TM, LANES), lambda t: (t, 0)),
                pl.BlockSpec(memory_space=pl.ANY)]
    args = [dest, x, mod, meta, y]
    if final:
        in_specs.append(pl.BlockSpec((1, D), lambda t: (0, 0)))
        args.append(final_g.reshape(1, D))
        out_specs = [pl.BlockSpec((TM, D), lambda t: (jnp.minimum(t, NTP - 1), 0)),
                     pl.BlockSpec((TM, D), lambda t: (jnp.maximum(t - NTP, 0), 0))]
        out_shape = [jax.ShapeDtypeStruct((T_P, D), f32), jax.ShapeDtypeStruct((T_S, D), f32)]
    else:
        out_specs = pl.BlockSpec((TM, D), lambda t: (t, 0))
        out_shape = jax.ShapeDtypeStruct((T, D), f32)
    return pl.pallas_call(
        functools.partial(_combine_kernel, final=final),
        grid=(NT,),
        in_specs=in_specs,
        out_specs=out_specs,
        out_shape=out_shape,
        scratch_shapes=[pltpu.VMEM((2, TM, D), f32), pltpu.SemaphoreType.DMA(())],
        compiler_params=_cp(("arbitrary",)),
    )(*args)


def _moe(x, mod, norm_g, router_w, router_b, w1, w3, w2, layer, final_g=None):
    h_rows, meta, meta_t, counts = _router(x, mod, norm_g, router_w, router_b, layer)
    cnt = counts[0, :N_EXPERTS].astype(jnp.int32)
    padded = ((cnt + TM_E - 1) // TM_E) * TM_E
    ends = jnp.cumsum(padded)
    starts = ends - padded
    experts = meta_t[M_E0:M_E1 + 1].astype(jnp.int32)
    ranks = meta_t[M_R0:M_R1 + 1].astype(jnp.int32)
    start_of = functools.reduce(lambda acc, e: jnp.where(experts == e, starts[e], acc), range(N_EXPERTS), 0)
    dest = start_of + ranks
    dest = dest.reshape(2, NT, TM).transpose(1, 0, 2).reshape(NT, 1, 2 * TM)
    n_used = (ends[-1] // TM_E).astype(jnp.int32).reshape(1)
    tile_start = jnp.minimum(jnp.arange(NT_E, dtype=jnp.int32), n_used[0] - 1) * TM_E
    tile_expert = jnp.sum((tile_start[:, None] >= ends[None, :]).astype(jnp.int32), axis=1)
    tile_expert = jnp.minimum(tile_expert, N_EXPERTS - 1).astype(jnp.int32)
    xg = _dispatch(dest, h_rows)
    y = _experts(tile_expert, n_used, xg, w1, w3, w2, layer)
    return _combine(dest, x, mod, meta, y, layer, final_g)


def kernel(x_prompt, x_sample, c, cache_k, cache_v, c_ctx, ada_w, ada_b, norm1_g, norm2_g, w_in, conv_w, rpb,
           w_out, ffn_w1, ffn_w3, ffn_w2, fourier_w, router_w, router_b, moe_w1, moe_w3, moe_w2, final_g):
    cvecs = jnp.concatenate([c_ctx[None, :], c, jnp.zeros((MOD_ROWS - 1 - DEC_BATCH, D), f32)], axis=0)
    mod = _modulation(cvecs, ada_w, ada_b).reshape(DEPTH * MOD_ROWS * 6, 1, D)
    n1 = norm1_g.reshape(DEPTH, 1, D)
    n2 = norm2_g.reshape(DEPTH, 1, D)
    bias = _nbr_bias(rpb)
    cache_kt = jnp.swapaxes(cache_k, -1, -2)
    cache_vt = jnp.swapaxes(cache_v, -1, -2)
    x = (x_prompt.reshape(T_P, D), x_sample.reshape(T_S, D))
    caches = [jnp.zeros((BATCH, N_EVEN, NA_HEADS, HEAD_DIM, SEQ), f32) for _ in range(2)]
    for layer in range(DEPTH):
        li = layer // 2
        if layer % 2 == 0:
            proj, *caches = _inproj(x, mod, n1, w_in, layer, caches)
            yb_p = _ctx_attention(proj)
            yb_s = _nbr_attention(proj, cache_kt, cache_vt, bias, li)
            x = _mixout(proj, yb_p, yb_s, x, mod, conv_w, w_out, layer)
            x = _dense_ffn(x, mod, n2, ffn_w1, ffn_w3, ffn_w2, layer)
        else:
            x = _fourier(x, mod, n1, fourier_w, layer)
            last = layer == DEPTH - 1
            x = _moe(x, mod, n2, router_w, router_b, moe_w1, moe_w3, moe_w2, layer, final_g if last else None)
    y_prompt, y_sample = x
    new_kt, new_vt = caches
    return (y_prompt.reshape(BATCH, SEQ, D), y_sample.reshape(DEC_BATCH, DEC_SEQ, D),
            jnp.swapaxes(new_kt, -1, -2), jnp.swapaxes(new_vt, -1, -2))
```

```python
import functools
from typing import Any, NamedTuple

import numpy as np
import jax
import jax.numpy as jnp
from jax import lax
from jax.experimental import pallas as pl
from jax.experimental.pallas import tpu as pltpu

f32 = jnp.float32
bf16 = jnp.bfloat16

D = 1024
BATCH = 16
SEQ = 256
DEPTH = 4
DEC_BATCH = 8
DEC_SEQ = 1024
PAST_LEN = 512
GRID_W = 64
CONV_CH = 512
NA_HEADS = 8
HEAD_DIM = 64
NA_WIDTH = 512
WIN_ROWS = 8
WIN_COLS = 16
IN_WIDTH = 3072
FOURIER_GROUPS = 4
FFN_DENSE = 2816
N_EXPERTS = 8
FFN_EXPERT = 3584
N_EVEN = 2
RMS_EPS = 1e-6
NEG_INF = -1e30
SCALE = HEAD_DIM ** -0.5

LANES = 128
TM = 1024
T_P = BATCH * SEQ
T_S = DEC_BATCH * DEC_SEQ
T = T_P + T_S
NT = T // TM
NTP = T_P // TM
MOD_ROWS = 16
GRID_ROWS = DEC_SEQ // GRID_W
CHUNK_ROWS = 4
N_CHUNKS = GRID_ROWS // CHUNK_ROWS
CHUNK_Q = CHUNK_ROWS * GRID_W
KWIN_ROWS = 12
KWIN = KWIN_ROWS * GRID_W
KB0 = (0, 0, 4, 4)
CHUNKS_PER_GROUP = 4
TF_DENSE = 256
TF_EXP = 512
TM_E = 1024
N_ASSIGN = 2 * T
NT_E = N_ASSIGN // TM_E + N_EXPERTS
R_E = NT_E * TM_E
VMEM_LIMIT = 56 * 1024 * 1024


def _cp(sem, vmem=VMEM_LIMIT):
    return pltpu.CompilerParams(dimension_semantics=sem, vmem_limit_bytes=vmem)


def _mod_index(t):
    return jnp.where(t < NTP, 0, t - (NTP - 1))


def _mod_spec(layer, part):
    def index(t, *_):
        return ((layer * MOD_ROWS + _mod_index(t)) * 6 + part, 0, 0)
    return pl.BlockSpec((1, 1, D), index)


def _norm_mod(x, g, sc, sh):
    y = x * lax.rsqrt(jnp.mean(x * x, axis=-1, keepdims=True) + RMS_EPS)
    return (y * g) * (1 + sc) + sh


def _dot(a, b):
    return jnp.dot(a, b, preferred_element_type=f32)


def _dot_nt(a, b):
    return lax.dot_general(a, b, (((1,), (1,)), ((), ())), preferred_element_type=f32)


def _mod_kernel(cv_ref, w_ref, b_ref, o_ref):
    s = jax.nn.silu(cv_ref[...]).astype(bf16)
    o_ref[0] = _dot(s, w_ref[0].astype(bf16)) + b_ref[0]


def _modulation(cvecs, ada_w, ada_b):
    tn = 1536
    return pl.pallas_call(
        _mod_kernel,
        grid=(DEPTH, 6 * D // tn),
        in_specs=[pl.BlockSpec((MOD_ROWS, D), lambda l, j: (0, 0)),
                  pl.BlockSpec((1, D, tn), lambda l, j: (l, 0, j)),
                  pl.BlockSpec((1, 1, tn), lambda l, j: (l, 0, j))],
        out_specs=pl.BlockSpec((1, MOD_ROWS, tn), lambda l, j: (l, 0, j)),
        out_shape=jax.ShapeDtypeStruct((DEPTH, MOD_ROWS, 6 * D), f32),
        compiler_params=_cp(("arbitrary", "arbitrary")),
    )(cvecs, ada_w, ada_b.reshape(DEPTH, 1, 6 * D))


TN_IN = IN_WIDTH // 2


def _x_specs(x):
    if isinstance(x, tuple):
        return [pl.BlockSpec((TM, D), lambda i, *_: (jnp.minimum(i, NTP - 1), 0)),
                pl.BlockSpec((TM, D), lambda i, *_: (jnp.maximum(i - NTP, 0), 0))]
    return [pl.BlockSpec((TM, D), lambda i, *_: (i, 0))]


def _load_x(i, x_refs):
    if len(x_refs) == 2:
        return jnp.where(i < NTP, x_refs[0][...], x_refs[1][...])
    return x_refs[0][...]


def _inproj_kernel(*refs, n_x, n_alias):
    x_refs, refs = refs[:n_x], refs[n_x + n_alias:]
    g_ref, sc_ref, sh_ref, w_ref, proj_ref, kt_ref, vt_ref, h_ref = refs
    i = pl.program_id(0)
    j = pl.program_id(1)

    @pl.when(j == 0)
    def _():
        h_ref[...] = _norm_mod(_load_x(i, x_refs), g_ref[0], sc_ref[0], sh_ref[0]).astype(bf16)

    acc = _dot(h_ref[...], w_ref[...].astype(bf16))
    proj_ref[...] = acc.astype(bf16)

    @pl.when((j == 1) & (i < NTP))
    def _():
        for ref, col0 in ((kt_ref, NA_WIDTH), (vt_ref, 2 * NA_WIDTH)):
            for s in range(TM // SEQ):
                t = acc[s * SEQ:(s + 1) * SEQ, col0:col0 + NA_WIDTH].T
                for hd in range(NA_HEADS):
                    ref[s, hd] = t[hd * HEAD_DIM:(hd + 1) * HEAD_DIM, :]


def _inproj(x, mod, norm_g, w_in, layer, caches):
    li = layer // 2
    tn = TN_IN
    seqs = TM // SEQ
    xs = x if isinstance(x, tuple) else (x,)
    cache_spec = pl.BlockSpec((seqs, None, NA_HEADS, HEAD_DIM, SEQ),
                              lambda i, j: (jnp.minimum(i, NTP - 1), li, 0, 0, 0))
    cache_shape = jax.ShapeDtypeStruct((BATCH, N_EVEN, NA_HEADS, HEAD_DIM, SEQ), f32)
    aliased = tuple(caches)
    return pl.pallas_call(
        functools.partial(_inproj_kernel, n_x=len(xs), n_alias=len(aliased)),
        grid=(NT, IN_WIDTH // tn),
        in_specs=_x_specs(x) + [pl.BlockSpec(memory_space=pl.ANY)] * len(aliased) + [
            pl.BlockSpec((1, 1, D), lambda i, j: (layer, 0, 0)),
            _mod_spec(layer, 1), _mod_spec(layer, 0),
            pl.BlockSpec((None, D, tn), lambda i, j: (li, 0, j))],
        out_specs=[pl.BlockSpec((TM, tn), lambda i, j: (i, j)), cache_spec, cache_spec],
        out_shape=[jax.ShapeDtypeStruct((T, IN_WIDTH), bf16), cache_shape, cache_shape],
        scratch_shapes=[pltpu.VMEM((TM, D), bf16)],
        input_output_aliases={len(xs) + n: 1 + n for n in range(len(aliased))},
        compiler_params=_cp(("arbitrary", "arbitrary")),
    )(*xs, *aliased, norm_g, mod, mod, w_in)


HEADS_PER_STEP = LANES // HEAD_DIM


class KeyValues(NamedTuple):
    k: jax.Array
    v: jax.Array
    bias: Any = None
    feature_major: bool = False


def _pair_attention(problems):
    first = lax.broadcasted_iota(jnp.int32, (1, LANES), 1) < HEAD_DIM
    mine = (first, jnp.logical_not(first))
    first_t = lax.broadcasted_iota(jnp.int32, (LANES, 1), 0) < HEAD_DIM
    mine_t = (first_t, jnp.logical_not(first_t))
    scores = []
    for q, parts in problems:
        q = q * SCALE
        for hh in range(HEADS_PER_STEP):
            qh = jnp.where(mine[hh], q, 0)
            ss = []
            for part in parts:
                s = _dot(qh, part.k) if part.feature_major else _dot_nt(qh, part.k)
                ss.append(s if part.bias is None else s + part.bias[hh])
            scores.append(ss)
    probs = []
    for ss in scores:
        m = functools.reduce(jnp.maximum, [jnp.max(s, axis=-1, keepdims=True) for s in ss])
        probs.append([jnp.exp((s - m).astype(bf16)) for s in ss])

    def p_times_v(p, part, hh):
        if part.feature_major:
            return _dot_nt(p, jnp.where(mine_t[hh], part.v, 1))
        return _dot(p, jnp.where(mine[hh], part.v, 1))

    results = []
    for n, (_, parts) in enumerate(problems):
        outs = []
        for hh in range(HEADS_PER_STEP):
            pv = functools.reduce(jnp.add, [p_times_v(p, part, hh)
                                            for p, part in zip(probs[n * HEADS_PER_STEP + hh], parts)])
            denom = pv[:, HEAD_DIM:HEAD_DIM + 1] if hh == 0 else pv[:, 0:1]
            outs.append(pv / denom)
        results.append(jnp.where(first, outs[0], outs[1]))
    return results


def _ctx_attn_kernel(q_ref, k_ref, v_ref, o_ref):
    cols = [slice(hp * LANES, (hp + 1) * LANES) for hp in range(NA_HEADS // HEADS_PER_STEP)]
    outs = _pair_attention([(q_ref[:, sl], [KeyValues(k_ref[:, sl], v_ref[:, sl])]) for sl in cols])
    for sl, o in zip(cols, outs):
        o_ref[:, sl] = o.astype(bf16)


def _ctx_attention(proj):
    col = IN_WIDTH // NA_WIDTH - 3
    return pl.pallas_call(
        _ctx_attn_kernel,
        grid=(BATCH,),
        in_specs=[pl.BlockSpec((SEQ, NA_WIDTH), lambda b: (b, col)),
                  pl.BlockSpec((SEQ, NA_WIDTH), lambda b: (b, col + 1)),
                  pl.BlockSpec((SEQ, NA_WIDTH), lambda b: (b, col + 2))],
        out_specs=pl.BlockSpec((SEQ, NA_WIDTH), lambda b: (b, 0)),
        out_shape=jax.ShapeDtypeStruct((T_P, NA_WIDTH), bf16),
        compiler_params=_cp(("arbitrary",)),
    )(proj, proj, proj)


def _win_start(qr):
    return min(max(qr - WIN_ROWS // 2, 0), GRID_ROWS - WIN_ROWS)


N_ROFF = 2 * WIN_ROWS - 1
N_COFF = 2 * WIN_COLS - 1


def _bias_kernel(rpb_ref, o_ref):
    base = (pl.program_id(0) * NA_HEADS + pl.program_id(1)) * (N_ROFF * N_COFF)
    qc = lax.broadcasted_iota(jnp.int32, (GRID_W, GRID_W), 0)
    kc = lax.broadcasted_iota(jnp.int32, (GRID_W, GRID_W), 1)
    coff = kc - qc + (WIN_COLS - 1)
    cs = jnp.clip(qc - WIN_COLS // 2, 0, GRID_W - WIN_COLS)
    valid = (kc >= cs) & (kc < cs + WIN_COLS)
    neg = jnp.full((GRID_W, GRID_W), NEG_INF, f32)
    blocks = []
    for a in range(N_ROFF):
        t = jnp.zeros((GRID_W, GRID_W), f32)
        for b in range(N_COFF):
            t = jnp.where(coff == b, rpb_ref[base + a * N_COFF + b], t)
        blocks.append(jnp.where(valid, t, neg))
    for c in range(N_CHUNKS):
        for ql in range(CHUNK_ROWS):
            qr = c * CHUNK_ROWS + ql
            st = _win_start(qr)
            for kl in range(KWIN_ROWS):
                kr = KB0[c] + kl
                blk = blocks[kr - qr + WIN_ROWS - 1] if st <= kr < st + WIN_ROWS else neg
                o_ref[c, ql * GRID_W:(ql + 1) * GRID_W, kl * GRID_W:(kl + 1) * GRID_W] = blk


def _nbr_bias(rpb):
    return pl.pallas_call(
        _bias_kernel,
        grid=(N_EVEN, NA_HEADS),
        in_specs=[pl.BlockSpec(memory_space=pltpu.SMEM)],
        out_specs=pl.BlockSpec((None, None, N_CHUNKS, CHUNK_Q, KWIN), lambda i, h: (i, h, 0, 0, 0)),
        out_shape=jax.ShapeDtypeStruct((N_EVEN, NA_HEADS, N_CHUNKS, CHUNK_Q, KWIN), f32),
        compiler_params=_cp(("arbitrary", "arbitrary")),
    )(rpb.reshape(-1))


def _nbr_attn_kernel(q_ref, k_ref, v_ref, kc_ref, vc_ref, bias_ref, o_ref):
    ctx = KeyValues(kc_ref[...].reshape(LANES, PAST_LEN).astype(bf16),
                    vc_ref[...].reshape(LANES, PAST_LEN).astype(bf16), feature_major=True)
    for c0 in range(0, N_CHUNKS, CHUNKS_PER_GROUP):
        problems = []
        for c in range(c0, c0 + CHUNKS_PER_GROUP):
            rows = slice(c * CHUNK_Q, (c + 1) * CHUNK_Q)
            win = slice(KB0[c] * GRID_W, KB0[c] * GRID_W + KWIN)
            bias = [bias_ref[hh, c] for hh in range(HEADS_PER_STEP)]
            problems.append((q_ref[rows, :], [KeyValues(k_ref[win, :], v_ref[win, :], bias), ctx]))
        for c, o in zip(range(c0, c0 + CHUNKS_PER_GROUP), _pair_attention(problems)):
            o_ref[c * CHUNK_Q:(c + 1) * CHUNK_Q, :] = o.astype(bf16)


def _nbr_attention(proj, cache_k, cache_v, bias, li):
    qcol = 3 * CONV_CH // LANES
    ncol = NA_WIDTH // LANES
    hp_steps = NA_HEADS // HEADS_PER_STEP

    def col_spec(which):
        return pl.BlockSpec((DEC_SEQ, LANES), lambda hp, b: (NTP + b, qcol + which * ncol + hp))

    ctx_spec = pl.BlockSpec((None, None, HEADS_PER_STEP, HEAD_DIM, PAST_LEN),
                            lambda hp, b: (b, li, hp, 0, 0))
    return pl.pallas_call(
        _nbr_attn_kernel,
        grid=(hp_steps, DEC_BATCH),
        in_specs=[col_spec(0), col_spec(1), col_spec(2), ctx_spec, ctx_spec,
                  pl.BlockSpec((None, HEADS_PER_STEP, N_CHUNKS, CHUNK_Q, KWIN),
                               lambda hp, b: (li, hp, 0, 0, 0))],
        out_specs=pl.BlockSpec((DEC_SEQ, LANES), lambda hp, b: (b, hp)),
        out_shape=jax.ShapeDtypeStruct((T_S, NA_WIDTH), bf16),
        compiler_params=_cp(("arbitrary", "arbitrary")),
    )(proj, proj, proj, cache_k, cache_v, bias)


def _mixout_kernel(*refs, n_x):
    x_refs, (a_ref, ybp_ref, ybs_ref, g1_ref, cw_ref, w_ref, o_ref) = refs[:n_x], refs[n_x:]
    t = pl.program_id(0)
    a_b = a_ref[:, 0:CONV_CH].astype(f32)
    a_c = a_ref[:, CONV_CH:2 * CONV_CH].astype(f32)
    a_x = a_ref[:, 2 * CONV_CH:3 * CONV_CH].astype(f32)
    u = a_c * a_x
    r = lax.broadcasted_iota(jnp.int32, (TM, 1), 0)
    pos = jnp.where(t < NTP, r % SEQ, r)
    last = jnp.where(t < NTP, SEQ - 1, DEC_SEQ - 1)
    u_prev = jnp.where(pos == 0, 0.0, pltpu.roll(u, 1, axis=0))
    u_next = jnp.where(pos == last, 0.0, pltpu.roll(u, TM - 1, axis=0))
    y_a = a_b * (u_prev * cw_ref[0:1, :] + u * cw_ref[1:2, :] + u_next * cw_ref[2:3, :])
    y_b = jnp.where(t < NTP, ybp_ref[...], ybs_ref[...])
    y = (_dot(y_a.astype(bf16), w_ref[0:CONV_CH, :].astype(bf16))
         + _dot(y_b, w_ref[CONV_CH:, :].astype(bf16)))
    o_ref[...] = _load_x(t, x_refs) + g1_ref[0] * y


def _mixout(proj, yb_p, yb_s, x, mod, conv_w, w_out, layer):
    li = layer // 2
    xs = x if isinstance(x, tuple) else (x,)
    return pl.pallas_call(
        functools.partial(_mixout_kernel, n_x=len(xs)),
        grid=(NT,),
        in_specs=_x_specs(x) + [
            pl.BlockSpec((TM, 3 * CONV_CH), lambda t: (t, 0)),
            pl.BlockSpec((TM, NA_WIDTH), lambda t: (jnp.minimum(t, NTP - 1), 0)),
            pl.BlockSpec((TM, NA_WIDTH), lambda t: (jnp.maximum(t - NTP, 0), 0)),
            _mod_spec(layer, 2),
            pl.BlockSpec((None, 3, CONV_CH), lambda t: (li, 0, 0)),
            pl.BlockSpec((None, D, D), lambda t: (li, 0, 0))],
        out_specs=pl.BlockSpec((TM, D), lambda t: (t, 0)),
        out_shape=jax.ShapeDtypeStruct((T, D), f32),
        compiler_params=_cp(("arbitrary",)),
    )(*xs, proj, yb_p, yb_s, mod, conv_w, w_out)


def _ffn_kernel(x_ref, g_ref, sc_ref, sh_ref, g2_ref, w1_ref, w3_ref, w2_ref, o_ref, h_ref, acc_ref):
    j = pl.program_id(1)

    @pl.when(j == 0)
    def _():
        h_ref[...] = _norm_mod(x_ref[...], g_ref[0], sc_ref[0], sh_ref[0]).astype(bf16)
        acc_ref[...] = jnp.zeros_like(acc_ref)

    h = h_ref[...]
    gate = jax.nn.silu(_dot(h, w1_ref[...].astype(bf16))) * _dot(h, w3_ref[...].astype(bf16))
    acc_ref[...] += _dot(gate.astype(bf16), w2_ref[...].astype(bf16))

    @pl.when(j == pl.num_programs(1) - 1)
    def _():
        o_ref[...] = x_ref[...] + g2_ref[0] * acc_ref[...]


def _dense_ffn(x, mod, norm_g, w1, w3, w2, layer):
    li = layer // 2
    tf = TF_DENSE
    return pl.pallas_call(
        _ffn_kernel,
        grid=(NT, FFN_DENSE // tf),
        in_specs=[pl.BlockSpec((TM, D), lambda i, j: (i, 0)),
                  pl.BlockSpec((1, 1, D), lambda i, j: (layer, 0, 0)),
                  _mod_spec(layer, 4), _mod_spec(layer, 3), _mod_spec(layer, 5),
                  pl.BlockSpec((None, D, tf), lambda i, j: (li, 0, j)),
                  pl.BlockSpec((None, D, tf), lambda i, j: (li, 0, j)),
                  pl.BlockSpec((None, tf, D), lambda i, j: (li, j, 0))],
        out_specs=pl.BlockSpec((TM, D), lambda i, j: (i, 0)),
        out_shape=jax.ShapeDtypeStruct((T, D), f32),
        scratch_shapes=[pltpu.VMEM((TM, D), bf16), pltpu.VMEM((TM, D), f32)],
        compiler_params=_cp(("arbitrary", "arbitrary")),
    )(x, norm_g, mod, mod, mod, w1, w3, w2)


GROUP_CH = D // FOURIER_GROUPS


def _dft_mats(n):
    k = np.arange(n, dtype=np.int64)
    ang = 2.0 * np.pi * ((k[:, None] * k[None, :]) % n).astype(np.float64) / n
    return np.cos(ang).astype(np.float32), np.sin(ang).astype(np.float32)


def _fourier_kernel(x_ref, g_ref, sc_ref, sh_ref, g1_ref, cs_ref, ss_ref, cl_ref, sl_ref, wf_ref,
                    o_ref, f_ref):
    t = pl.program_id(0)
    h = _norm_mod(x_ref[...], g_ref[0], sc_ref[0], sh_ref[0]).astype(bf16)
    cs = cs_ref[...].astype(bf16)
    ss = ss_ref[...].astype(bf16)
    ys, zs = [], []
    for g in range(FOURIER_GROUPS):
        hg = h[:, g * GROUP_CH:(g + 1) * GROUP_CH]
        ys.append(_dot(hg, cs))
        zs.append(_dot(hg, ss))
    y = jnp.concatenate(ys, axis=-1).astype(bf16)
    z = jnp.concatenate(zs, axis=-1).astype(bf16)

    @pl.when(t < NTP)
    def _():
        for s in range(TM // SEQ):
            rows = slice(s * SEQ, (s + 1) * SEQ)
            f = _dot(cs, y[rows]) - _dot(ss, z[rows])
            f_ref[rows, :] = (f * ((SEQ * GROUP_CH) ** -0.5)).astype(bf16)

    @pl.when(t >= NTP)
    def _():
        f = _dot(cl_ref[...].astype(bf16), y) - _dot(sl_ref[...].astype(bf16), z)
        f_ref[...] = (f * ((DEC_SEQ * GROUP_CH) ** -0.5)).astype(bf16)

    o_ref[...] = x_ref[...] + g1_ref[0] * _dot(f_ref[...], wf_ref[...].astype(bf16))


def _fourier(x, mod, norm_g, fourier_w, layer):
    li = layer // 2
    c_s, s_s = _dft_mats(SEQ)
    c_l, s_l = _dft_mats(DEC_SEQ)
    const = lambda shape: pl.BlockSpec(shape, lambda t: (0, 0))
    return pl.pallas_call(
        _fourier_kernel,
        grid=(NT,),
        in_specs=[pl.BlockSpec((TM, D), lambda t: (t, 0)),
                  pl.BlockSpec((1, 1, D), lambda t: (layer, 0, 0)),
                  _mod_spec(layer, 1), _mod_spec(layer, 0), _mod_spec(layer, 2),
                  const((SEQ, SEQ)), const((SEQ, SEQ)),
                  const((DEC_SEQ, DEC_SEQ)), const((DEC_SEQ, DEC_SEQ)),
                  pl.BlockSpec((None, D, D), lambda t: (li, 0, 0))],
        out_specs=pl.BlockSpec((TM, D), lambda t: (t, 0)),
        out_shape=jax.ShapeDtypeStruct((T, D), f32),
        scratch_shapes=[pltpu.VMEM((TM, D), bf16)],
        compiler_params=_cp(("arbitrary",)),
    )(x, norm_g, mod, mod, mod, jnp.asarray(c_s), jnp.asarray(s_s), jnp.asarray(c_l), jnp.asarray(s_l),
      fourier_w)


M_E0, M_E1, M_R0, M_R1, M_W0, M_W1 = range(6)
ROW_UNROLL = 8


META_ROWS = 8


def _router_kernel(x_ref, g_ref, sc_ref, sh_ref, rw_ref, rb_ref, h_ref, meta_ref, meta_t_ref, cnt_ref,
                   carry_ref):
    t = pl.program_id(0)

    @pl.when(t == 0)
    def _():
        carry_ref[...] = jnp.zeros_like(carry_ref)

    h = _norm_mod(x_ref[...], g_ref[0], sc_ref[0], sh_ref[0])
    h_ref[...] = h
    h_hi = h.astype(bf16)
    h_lo = (h - h_hi.astype(f32)).astype(bf16)
    hi_terms = _dot(h_hi, rw_ref[...])
    lo_term = _dot(h_lo, rw_ref[:, :LANES])
    logits = ((hi_terms[:, :LANES] + lo_term) + hi_terms[:, LANES:]) + rb_ref[...]
    lane = lax.broadcasted_iota(jnp.int32, (TM, LANES), 1)
    m1 = jnp.max(logits, axis=-1, keepdims=True)
    i1 = jnp.min(jnp.where(logits == m1, lane, LANES), axis=-1, keepdims=True)
    rest = jnp.where(lane == i1, -jnp.inf, logits)
    m2 = jnp.max(rest, axis=-1, keepdims=True)
    i2 = jnp.min(jnp.where(rest == m2, lane, LANES), axis=-1, keepdims=True)
    e = jnp.exp(m2 - m1)
    w0 = 1.0 / (1.0 + e)
    w1 = e / (1.0 + e)
    oh0 = (lane == i1).astype(f32)
    oh1 = (lane == i2).astype(f32)
    oh = oh0 + oh1
    row = lax.broadcasted_iota(jnp.int32, (TM, TM), 0)
    col = lax.broadcasted_iota(jnp.int32, (TM, TM), 1)
    before = jnp.where(col < row, 1.0, 0.0).astype(bf16)
    base = carry_ref[...] + _dot(before, oh.astype(bf16))
    r0 = jnp.sum(oh0 * base, axis=-1, keepdims=True)
    r1 = jnp.sum(oh1 * base, axis=-1, keepdims=True)
    carry_ref[...] += jnp.sum(oh, axis=0, keepdims=True)
    rec = jnp.zeros((TM, LANES), f32)
    for idx, val in ((M_E0, i1.astype(f32)), (M_E1, i2.astype(f32)), (M_R0, r0), (M_R1, r1),
                     (M_W0, w0), (M_W1, w1)):
        rec = jnp.where(lane == idx, val, rec)
    meta_ref[...] = rec
    meta_t_ref[...] = rec.T[:META_ROWS, :]
    cnt_ref[...] = carry_ref[...]


def _router(x, mod, norm_g, router_w, router_b, layer):
    li = layer // 2
    rw = jnp.pad(router_w[li], ((0, 0), (0, LANES - N_EXPERTS)))
    rw_hi = rw.astype(bf16)
    rw_lo = (rw - rw_hi.astype(f32)).astype(bf16)
    rw = jnp.concatenate([rw_hi, rw_lo], axis=1)
    rb = jnp.pad(router_b[li], (0, LANES - N_EXPERTS), constant_values=NEG_INF).reshape(1, LANES)
    return pl.pallas_call(
        _router_kernel,
        grid=(NT,),
        in_specs=[pl.BlockSpec((TM, D), lambda t: (t, 0)),
                  pl.BlockSpec((1, 1, D), lambda t: (layer, 0, 0)),
                  _mod_spec(layer, 4), _mod_spec(layer, 3),
                  pl.BlockSpec((D, 2 * LANES), lambda t: (0, 0)),
                  pl.BlockSpec((1, LANES), lambda t: (0, 0))],
        out_specs=[pl.BlockSpec((TM, D), lambda t: (t, 0)),
                   pl.BlockSpec((TM, LANES), lambda t: (t, 0)),
                   pl.BlockSpec((META_ROWS, TM), lambda t: (0, t)),
                   pl.BlockSpec((1, LANES), lambda t: (0, 0))],
        out_shape=[jax.ShapeDtypeStruct((T, D), f32),
                   jax.ShapeDtypeStruct((T, LANES), f32),
                   jax.ShapeDtypeStruct((META_ROWS, T), f32),
                   jax.ShapeDtypeStruct((1, LANES), f32)],
        scratch_shapes=[pltpu.VMEM((1, LANES), f32)],
        compiler_params=_cp(("arbitrary",)),
    )(x, norm_g, mod, mod, rw, rb)


def _row_copy(src_ref, src_row, dst_ref, dst_row, sem):
    return pltpu.make_async_copy(src_ref.at[pl.ds(src_row, 1)], dst_ref.at[pl.ds(dst_row, 1)], sem)


ZERO_BLOCK = 8


def _dispatch_kernel(lo_ref, hi_ref, nu_ref, dest_ref, h_ref, xg_ref, zero_ref, sem, zsem):
    @pl.when(pl.program_id(0) == 0)
    def _():
        zero_ref[...] = jnp.zeros_like(zero_ref)

        def row_zero(r):
            return _row_copy(zero_ref, 0, xg_ref, r, zsem)

        def block_zero(b):
            return pltpu.make_async_copy(zero_ref.at[pl.ds(0, ZERO_BLOCK)],
                                         xg_ref.at[pl.ds(pl.multiple_of(b * ZERO_BLOCK, ZERO_BLOCK), ZERO_BLOCK)],
                                         zsem)

        def tile_zero(tile):
            return pltpu.make_async_copy(zero_ref, xg_ref.at[pl.ds(pl.multiple_of(tile * TM_E, TM_E), TM_E)],
                                         zsem)

        def each(lo, hi, copy, wait):
            def step(i, c):
                if wait:
                    copy(i).wait()
                else:
                    copy(i).start()
                return c
            lax.fori_loop(lo, hi, step, 0)

        for wait in (False, True):
            for e in range(N_EXPERTS):
                lo, hi = lo_ref[e], hi_ref[e]
                aligned = jnp.minimum((lo + ZERO_BLOCK - 1) // ZERO_BLOCK * ZERO_BLOCK, hi)
                each(lo, aligned, row_zero, wait)
                each(aligned // ZERO_BLOCK, hi // ZERO_BLOCK, block_zero, wait)
            each(nu_ref[0], NT_E, tile_zero, wait)

    def issue(g, c):
        for u in range(ROW_UNROLL):
            r = g * ROW_UNROLL + u
            for k in range(2):
                _row_copy(h_ref, r, xg_ref, dest_ref[0, 0, k * TM + r], sem).start()
        return c

    lax.fori_loop(0, TM // ROW_UNROLL, issue, 0)
    for k in range(2):
        pltpu.make_async_copy(h_ref, xg_ref.at[pl.ds(0, TM)], sem).wait()


def _dispatch(pad_lo, pad_hi, n_used, dest, h):
    grid_spec = pltpu.PrefetchScalarGridSpec(
        num_scalar_prefetch=3,
        grid=(NT,),
        in_specs=[pl.BlockSpec((1, 1, 2 * TM), lambda t, *_: (t, 0, 0), memory_space=pltpu.SMEM),
                  pl.BlockSpec((TM, D), lambda t, *_: (t, 0))],
        out_specs=pl.BlockSpec(memory_space=pl.ANY),
        scratch_shapes=[pltpu.VMEM((TM_E, D), f32), pltpu.SemaphoreType.DMA(()), pltpu.SemaphoreType.DMA(())],
    )
    return pl.pallas_call(
        _dispatch_kernel,
        grid_spec=grid_spec,
        out_shape=jax.ShapeDtypeStruct((R_E, D), f32),
        compiler_params=_cp(("arbitrary",)),
    )(pad_lo, pad_hi, n_used, dest, h)


NJ_EXP = FFN_EXPERT // TF_EXP
ROW_CLASSES = (TM_E // 4, TM_E // 2, TM_E)


def _expert_kernel(te_ref, nu_ref, rows_ref, xg_ref, w1_ref, w3_ref, w2_ref, y_ref, h_ref):
    t = pl.program_id(0)
    j = pl.program_id(1)
    n_rows = rows_ref[t]

    @pl.when(j == 0)
    def _():
        h_ref[...] = xg_ref[...].astype(bf16)
        y_ref[...] = jnp.zeros_like(y_ref)

    for below, m in zip((0,) + ROW_CLASSES, ROW_CLASSES):
        @pl.when((n_rows > below) & (n_rows <= m))
        def _():
            h = h_ref[:m, :]
            gate = jax.nn.silu(_dot(h, w1_ref[...].astype(bf16))) * _dot(h, w3_ref[...].astype(bf16))
            y_ref[:m, :] += _dot(gate.astype(bf16), w2_ref[...].astype(bf16))


def _experts(tile_expert, n_used, tile_rows, xg, w1, w3, w2, layer):
    li = layer // 2
    tf = TF_EXP
    nj = NJ_EXP

    def jj(t, j, nu):
        return jnp.where(t < nu[0], j, nj - 1)

    def tt(t, nu):
        return jnp.minimum(t, jnp.maximum(nu[0] - 1, 0))

    grid_spec = pltpu.PrefetchScalarGridSpec(
        num_scalar_prefetch=3,
        grid=(NT_E, nj),
        in_specs=[pl.BlockSpec((TM_E, D), lambda t, j, te, nu, nr: (tt(t, nu), 0)),
                  pl.BlockSpec((None, None, D, tf), lambda t, j, te, nu, nr: (li, te[t], 0, jj(t, j, nu))),
                  pl.BlockSpec((None, None, D, tf), lambda t, j, te, nu, nr: (li, te[t], 0, jj(t, j, nu))),
                  pl.BlockSpec((None, None, tf, D), lambda t, j, te, nu, nr: (li, te[t], jj(t, j, nu), 0))],
        out_specs=pl.BlockSpec((TM_E, D), lambda t, j, te, nu, nr: (t, 0)),
        scratch_shapes=[pltpu.VMEM((TM_E, D), bf16)],
    )
    return pl.pallas_call(
        _expert_kernel,
        grid_spec=grid_spec,
        out_shape=jax.ShapeDtypeStruct((R_E, D), f32),
        compiler_params=_cp(("arbitrary", "arbitrary")),
    )(tile_expert, n_used, tile_rows, xg, w1, w3, w2)


def _combine_kernel(dest_ref, x_ref, g2_ref, meta_ref, y_ref, *rest, final):
    if final:
        fg_ref, op_ref, os_ref, buf_ref, sem = rest
    else:
        o_ref, buf_ref, sem = rest

    def issue(g, c):
        for u in range(ROW_UNROLL):
            r = g * ROW_UNROLL + u
            for k in range(2):
                _row_copy(y_ref, dest_ref[0, 0, k * TM + r], buf_ref.at[k], r, sem).start()
        return c

    lax.fori_loop(0, TM // ROW_UNROLL, issue, 0)
    for k in range(2):
        pltpu.make_async_copy(y_ref.at[pl.ds(0, TM)], buf_ref.at[k], sem).wait()
    w0 = meta_ref[:, M_W0:M_W0 + 1]
    w1 = meta_ref[:, M_W1:M_W1 + 1]
    out = x_ref[...] + g2_ref[0] * (w0 * buf_ref[0] + w1 * buf_ref[1])
    if not final:
        o_ref[...] = out
        return
    normed = (out * lax.rsqrt(jnp.mean(out * out, axis=-1, keepdims=True) + RMS_EPS)) * fg_ref[...]
    t = pl.program_id(0)

    @pl.when(t < NTP)
    def _():
        op_ref[...] = normed

    @pl.when(t >= NTP)
    def _():
        os_ref[...] = normed


def _combine(dest, x, mod, meta, y, layer, final_g=None):
    final = final_g is not None
    in_specs = [pl.BlockSpec((1, 1, 2 * TM), lambda t: (t, 0, 0), memory_space=pltpu.SMEM),
                pl.BlockSpec((TM, D), lambda t: (t, 0)),
                _mod_spec(layer, 5),
                pl.BlockSpec((TM, LANES), lambda t: (t, 0)),
                pl.BlockSpec(memory_space=pl.ANY)]
    args = [dest, x, mod, meta, y]
    if final:
        in_specs.append(pl.BlockSpec((1, D), lambda t: (0, 0)))
        args.append(final_g.reshape(1, D))
        out_specs = [pl.BlockSpec((TM, D), lambda t: (jnp.minimum(t, NTP - 1), 0)),
                     pl.BlockSpec((TM, D), lambda t: (jnp.maximum(t - NTP, 0), 0))]
        out_shape = [jax.ShapeDtypeStruct((T_P, D), f32), jax.ShapeDtypeStruct((T_S, D), f32)]
    else:
        out_specs = pl.BlockSpec((TM, D), lambda t: (t, 0))
        out_shape = jax.ShapeDtypeStruct((T, D), f32)
    return pl.pallas_call(
        functools.partial(_combine_kernel, final=final),
        grid=(NT,),
        in_specs=in_specs,
        out_specs=out_specs,
        out_shape=out_shape,
        scratch_shapes=[pltpu.VMEM((2, TM, D), f32), pltpu.SemaphoreType.DMA(())],
        compiler_params=_cp(("arbitrary",)),
    )(*args)


def _moe(x, mod, norm_g, router_w, router_b, w1, w3, w2, layer, final_g=None):
    h_rows, meta, meta_t, counts = _router(x, mod, norm_g, router_w, router_b, layer)
    cnt = counts[0, :N_EXPERTS].astype(jnp.int32)
    padded = ((cnt + TM_E - 1) // TM_E) * TM_E
    ends = jnp.cumsum(padded)
    starts = ends - padded
    experts = meta_t[M_E0:M_E1 + 1].astype(jnp.int32)
    ranks = meta_t[M_R0:M_R1 + 1].astype(jnp.int32)
    start_of = functools.reduce(lambda acc, e: jnp.where(experts == e, starts[e], acc), range(N_EXPERTS), 0)
    dest = start_of + ranks
    dest = dest.reshape(2, NT, TM).transpose(1, 0, 2).reshape(NT, 1, 2 * TM)
    n_used = (ends[-1] // TM_E).astype(jnp.int32).reshape(1)
    tile_start = jnp.minimum(jnp.arange(NT_E, dtype=jnp.int32), n_used[0] - 1) * TM_E
    tile_expert = jnp.sum((tile_start[:, None] >= ends[None, :]).astype(jnp.int32), axis=1)
    tile_expert = jnp.minimum(tile_expert, N_EXPERTS - 1).astype(jnp.int32)
    tile_ids = jnp.arange(NT_E, dtype=jnp.int32)
    tile_rows = jnp.clip((starts + cnt)[tile_expert] - tile_ids * TM_E, 0, TM_E)
    tile_rows = jnp.where(tile_ids < n_used[0], tile_rows, 0).astype(jnp.int32)
    xg = _dispatch((starts + cnt).astype(jnp.int32), ends.astype(jnp.int32), n_used, dest, h_rows)
    y = _experts(tile_expert, n_used, tile_rows, xg, w1, w3, w2, layer)
    return _combine(dest, x, mod, meta, y, layer, final_g)


def kernel(x_prompt, x_sample, c, cache_k, cache_v, c_ctx, ada_w, ada_b, norm1_g, norm2_g, w_in, conv_w, rpb,
           w_out, ffn_w1, ffn_w3, ffn_w2, fourier_w, router_w, router_b, moe_w1, moe_w3, moe_w2, final_g):
    cvecs = jnp.concatenate([c_ctx[None, :], c, jnp.zeros((MOD_ROWS - 1 - DEC_BATCH, D), f32)], axis=0)
    mod = _modulation(cvecs, ada_w, ada_b).reshape(DEPTH * MOD_ROWS * 6, 1, D)
    n1 = norm1_g.reshape(DEPTH, 1, D)
    n2 = norm2_g.reshape(DEPTH, 1, D)
    bias = _nbr_bias(rpb)
    cache_kt = jnp.swapaxes(cache_k, -1, -2)
    cache_vt = jnp.swapaxes(cache_v, -1, -2)
    x = (x_prompt.reshape(T_P, D), x_sample.reshape(T_S, D))
    caches = [jnp.zeros((BATCH, N_EVEN, NA_HEADS, HEAD_DIM, SEQ), f32) for _ in range(2)]
    for layer in range(DEPTH):
        li = layer // 2
        if layer % 2 == 0:
            proj, *caches = _inproj(x, mod, n1, w_in, layer, caches)
            yb_p = _ctx_attention(proj)
            yb_s = _nbr_attention(proj, cache_kt, cache_vt, bias, li)
            x = _mixout(proj, yb_p, yb_s, x, mod, conv_w, w_out, layer)
            x = _dense_ffn(x, mod, n2, ffn_w1, ffn_w3, ffn_w2, layer)
        else:
            x = _fourier(x, mod, n1, fourier_w, layer)
            last = layer == DEPTH - 1
            x = _moe(x, mod, n2, router_w, router_b, moe_w1, moe_w3, moe_w2, layer, final_g if last else None)
    y_prompt, y_sample = x
    new_kt, new_vt = caches
    return (y_prompt.reshape(BATCH, SEQ, D), y_sample.reshape(DEC_BATCH, DEC_SEQ, D),
            jnp.swapaxes(new_kt, -1, -2), jnp.swapaxes(new_vt, -1, -2))
```

```python
import functools
from typing import Any, NamedTuple

import numpy as np
import jax
import jax.numpy as jnp
from jax import lax
from jax.experimental import pallas as pl
from jax.experimental.pallas import tpu as pltpu

f32 = jnp.float32
bf16 = jnp.bfloat16

D = 1024
BATCH = 16
SEQ = 256
DEPTH = 4
DEC_BATCH = 8
DEC_SEQ = 1024
PAST_LEN = 512
GRID_W = 64
CONV_CH = 512
NA_HEADS = 8
HEAD_DIM = 64
NA_WIDTH = 512
WIN_ROWS = 8
WIN_COLS = 16
IN_WIDTH = 3072
FOURIER_GROUPS = 4
FFN_DENSE = 2816
N_EXPERTS = 8
FFN_EXPERT = 3584
N_EVEN = 2
RMS_EPS = 1e-6
NEG_INF = -1e30
SCALE = HEAD_DIM ** -0.5

LANES = 128
TM = 1024
T_P = BATCH * SEQ
T_S = DEC_BATCH * DEC_SEQ
T = T_P + T_S
NT = T // TM
NTP = T_P // TM
MOD_ROWS = 16
GRID_ROWS = DEC_SEQ // GRID_W
CHUNK_ROWS = 4
N_CHUNKS = GRID_ROWS // CHUNK_ROWS
CHUNK_Q = CHUNK_ROWS * GRID_W
KWIN_ROWS = 12
KWIN = KWIN_ROWS * GRID_W
KB0 = (0, 0, 4, 4)
CHUNKS_PER_GROUP = 4
TF_DENSE = 256
TF_EXP = 512
TM_E = 1024
N_ASSIGN = 2 * T
NT_E = N_ASSIGN // TM_E + N_EXPERTS
R_E = NT_E * TM_E
VMEM_LIMIT = 56 * 1024 * 1024


def _cp(sem, vmem=VMEM_LIMIT):
    return pltpu.CompilerParams(dimension_semantics=sem, vmem_limit_bytes=vmem)


def _mod_index(t):
    return jnp.where(t < NTP, 0, t - (NTP - 1))


def _mod_spec(layer, part):
    def index(t, *_):
        return ((layer * MOD_ROWS + _mod_index(t)) * 6 + part, 0, 0)
    return pl.BlockSpec((1, 1, D), index)


def _norm_mod(x, g, sc, sh):
    y = x * lax.rsqrt(jnp.mean(x * x, axis=-1, keepdims=True) + RMS_EPS)
    return (y * g) * (1 + sc) + sh


def _dot(a, b):
    return jnp.dot(a, b, preferred_element_type=f32)


def _dot_nt(a, b):
    return lax.dot_general(a, b, (((1,), (1,)), ((), ())), preferred_element_type=f32)


def _mod_kernel(cv_ref, w_ref, b_ref, o_ref):
    s = jax.nn.silu(cv_ref[...]).astype(bf16)
    o_ref[0] = _dot(s, w_ref[0].astype(bf16)) + b_ref[0]


def _modulation(cvecs, ada_w, ada_b):
    tn = 1536
    return pl.pallas_call(
        _mod_kernel,
        grid=(DEPTH, 6 * D // tn),
        in_specs=[pl.BlockSpec((MOD_ROWS, D), lambda l, j: (0, 0)),
                  pl.BlockSpec((1, D, tn), lambda l, j: (l, 0, j)),
                  pl.BlockSpec((1, 1, tn), lambda l, j: (l, 0, j))],
        out_specs=pl.BlockSpec((1, MOD_ROWS, tn), lambda l, j: (l, 0, j)),
        out_shape=jax.ShapeDtypeStruct((DEPTH, MOD_ROWS, 6 * D), f32),
        compiler_params=_cp(("arbitrary", "arbitrary")),
    )(cvecs, ada_w, ada_b.reshape(DEPTH, 1, 6 * D))


TN_IN = IN_WIDTH // 2


def _x_specs(x):
    if isinstance(x, tuple):
        return [pl.BlockSpec((TM, D), lambda i, *_: (jnp.minimum(i, NTP - 1), 0)),
                pl.BlockSpec((TM, D), lambda i, *_: (jnp.maximum(i - NTP, 0), 0))]
    return [pl.BlockSpec((TM, D), lambda i, *_: (i, 0))]


def _load_x(i, x_refs):
    if len(x_refs) == 2:
        return jnp.where(i < NTP, x_refs[0][...], x_refs[1][...])
    return x_refs[0][...]


def _inproj_kernel(*refs, n_x, n_alias):
    x_refs, refs = refs[:n_x], refs[n_x + n_alias:]
    g_ref, sc_ref, sh_ref, w_ref, proj_ref, kt_ref, vt_ref, h_ref = refs
    i = pl.program_id(0)
    j = pl.program_id(1)

    @pl.when(j == 0)
    def _():
        h_ref[...] = _norm_mod(_load_x(i, x_refs), g_ref[0], sc_ref[0], sh_ref[0]).astype(bf16)

    acc = _dot(h_ref[...], w_ref[...].astype(bf16))
    proj_ref[...] = acc.astype(bf16)

    @pl.when((j == 1) & (i < NTP))
    def _():
        for ref, col0 in ((kt_ref, NA_WIDTH), (vt_ref, 2 * NA_WIDTH)):
            for s in range(TM // SEQ):
                t = acc[s * SEQ:(s + 1) * SEQ, col0:col0 + NA_WIDTH].T
                for hd in range(NA_HEADS):
                    ref[s, hd] = t[hd * HEAD_DIM:(hd + 1) * HEAD_DIM, :]


def _inproj(x, mod, norm_g, w_in, layer, caches):
    li = layer // 2
    tn = TN_IN
    seqs = TM // SEQ
    xs = x if isinstance(x, tuple) else (x,)
    cache_spec = pl.BlockSpec((seqs, None, NA_HEADS, HEAD_DIM, SEQ),
                              lambda i, j: (jnp.minimum(i, NTP - 1), li, 0, 0, 0))
    cache_shape = jax.ShapeDtypeStruct((BATCH, N_EVEN, NA_HEADS, HEAD_DIM, SEQ), f32)
    aliased = tuple(caches)
    return pl.pallas_call(
        functools.partial(_inproj_kernel, n_x=len(xs), n_alias=len(aliased)),
        grid=(NT, IN_WIDTH // tn),
        in_specs=_x_specs(x) + [pl.BlockSpec(memory_space=pl.ANY)] * len(aliased) + [
            pl.BlockSpec((1, 1, D), lambda i, j: (layer, 0, 0)),
            _mod_spec(layer, 1), _mod_spec(layer, 0),
            pl.BlockSpec((None, D, tn), lambda i, j: (li, 0, j))],
        out_specs=[pl.BlockSpec((TM, tn), lambda i, j: (i, j)), cache_spec, cache_spec],
        out_shape=[jax.ShapeDtypeStruct((T, IN_WIDTH), bf16), cache_shape, cache_shape],
        scratch_shapes=[pltpu.VMEM((TM, D), bf16)],
        input_output_aliases={len(xs) + n: 1 + n for n in range(len(aliased))},
        compiler_params=_cp(("arbitrary", "arbitrary")),
    )(*xs, *aliased, norm_g, mod, mod, w_in)


HEADS_PER_STEP = LANES // HEAD_DIM


class KeyValues(NamedTuple):
    k: jax.Array
    v: jax.Array
    bias: Any = None
    feature_major: bool = False


def _pair_attention(problems):
    first = lax.broadcasted_iota(jnp.int32, (1, LANES), 1) < HEAD_DIM
    mine = (first, jnp.logical_not(first))
    first_t = lax.broadcasted_iota(jnp.int32, (LANES, 1), 0) < HEAD_DIM
    mine_t = (first_t, jnp.logical_not(first_t))
    scores = []
    for q, parts in problems:
        q = q * SCALE
        for hh in range(HEADS_PER_STEP):
            qh = jnp.where(mine[hh], q, 0)
            ss = []
            for part in parts:
                s = _dot(qh, part.k) if part.feature_major else _dot_nt(qh, part.k)
                ss.append(s if part.bias is None else s + part.bias[hh])
            scores.append(ss)
    probs = []
    for ss in scores:
        m = functools.reduce(jnp.maximum, [jnp.max(s, axis=-1, keepdims=True) for s in ss])
        probs.append([jnp.exp((s - m).astype(bf16)) for s in ss])

    def p_times_v(p, part, hh):
        if part.feature_major:
            return _dot_nt(p, jnp.where(mine_t[hh], part.v, 1))
        return _dot(p, jnp.where(mine[hh], part.v, 1))

    results = []
    for n, (_, parts) in enumerate(problems):
        outs = []
        for hh in range(HEADS_PER_STEP):
            pv = functools.reduce(jnp.add, [p_times_v(p, part, hh)
                                            for p, part in zip(probs[n * HEADS_PER_STEP + hh], parts)])
            denom = pv[:, HEAD_DIM:HEAD_DIM + 1] if hh == 0 else pv[:, 0:1]
            outs.append(pv / denom)
        results.append(jnp.where(first, outs[0], outs[1]))
    return results


def _ctx_attn_kernel(q_ref, k_ref, v_ref, o_ref):
    cols = [slice(hp * LANES, (hp + 1) * LANES) for hp in range(NA_HEADS // HEADS_PER_STEP)]
    outs = _pair_attention([(q_ref[:, sl], [KeyValues(k_ref[:, sl], v_ref[:, sl])]) for sl in cols])
    for sl, o in zip(cols, outs):
        o_ref[:, sl] = o.astype(bf16)


def _ctx_attention(proj):
    col = IN_WIDTH // NA_WIDTH - 3
    return pl.pallas_call(
        _ctx_attn_kernel,
        grid=(BATCH,),
        in_specs=[pl.BlockSpec((SEQ, NA_WIDTH), lambda b: (b, col)),
                  pl.BlockSpec((SEQ, NA_WIDTH), lambda b: (b, col + 1)),
                  pl.BlockSpec((SEQ, NA_WIDTH), lambda b: (b, col + 2))],
        out_specs=pl.BlockSpec((SEQ, NA_WIDTH), lambda b: (b, 0)),
        out_shape=jax.ShapeDtypeStruct((T_P, NA_WIDTH), bf16),
        compiler_params=_cp(("arbitrary",)),
    )(proj, proj, proj)


def _win_start(qr):
    return min(max(qr - WIN_ROWS // 2, 0), GRID_ROWS - WIN_ROWS)


N_ROFF = 2 * WIN_ROWS - 1
N_COFF = 2 * WIN_COLS - 1


def _bias_kernel(rpb_ref, o_ref):
    base = (pl.program_id(0) * NA_HEADS + pl.program_id(1)) * (N_ROFF * N_COFF)
    qc = lax.broadcasted_iota(jnp.int32, (GRID_W, GRID_W), 0)
    kc = lax.broadcasted_iota(jnp.int32, (GRID_W, GRID_W), 1)
    coff = kc - qc + (WIN_COLS - 1)
    cs = jnp.clip(qc - WIN_COLS // 2, 0, GRID_W - WIN_COLS)
    valid = (kc >= cs) & (kc < cs + WIN_COLS)
    neg = jnp.full((GRID_W, GRID_W), NEG_INF, f32)
    blocks = []
    for a in range(N_ROFF):
        t = jnp.zeros((GRID_W, GRID_W), f32)
        for b in range(N_COFF):
            t = jnp.where(coff == b, rpb_ref[base + a * N_COFF + b], t)
        blocks.append(jnp.where(valid, t, neg))
    for c in range(N_CHUNKS):
        for ql in range(CHUNK_ROWS):
            qr = c * CHUNK_ROWS + ql
            st = _win_start(qr)
            for kl in range(KWIN_ROWS):
                kr = KB0[c] + kl
                blk = blocks[kr - qr + WIN_ROWS - 1] if st <= kr < st + WIN_ROWS else neg
                o_ref[c, ql * GRID_W:(ql + 1) * GRID_W, kl * GRID_W:(kl + 1) * GRID_W] = blk


def _nbr_bias(rpb):
    return pl.pallas_call(
        _bias_kernel,
        grid=(N_EVEN, NA_HEADS),
        in_specs=[pl.BlockSpec(memory_space=pltpu.SMEM)],
        out_specs=pl.BlockSpec((None, None, N_CHUNKS, CHUNK_Q, KWIN), lambda i, h: (i, h, 0, 0, 0)),
        out_shape=jax.ShapeDtypeStruct((N_EVEN, NA_HEADS, N_CHUNKS, CHUNK_Q, KWIN), f32),
        compiler_params=_cp(("arbitrary", "arbitrary")),
    )(rpb.reshape(-1))


def _nbr_attn_kernel(q_ref, k_ref, v_ref, kc_ref, vc_ref, bias_ref, o_ref):
    ctx = KeyValues(kc_ref[...].reshape(LANES, PAST_LEN).astype(bf16),
                    vc_ref[...].reshape(LANES, PAST_LEN).astype(bf16), feature_major=True)
    for c0 in range(0, N_CHUNKS, CHUNKS_PER_GROUP):
        problems = []
        for c in range(c0, c0 + CHUNKS_PER_GROUP):
            rows = slice(c * CHUNK_Q, (c + 1) * CHUNK_Q)
            win = slice(KB0[c] * GRID_W, KB0[c] * GRID_W + KWIN)
            bias = [bias_ref[hh, c] for hh in range(HEADS_PER_STEP)]
            problems.append((q_ref[rows, :], [KeyValues(k_ref[win, :], v_ref[win, :], bias), ctx]))
        for c, o in zip(range(c0, c0 + CHUNKS_PER_GROUP), _pair_attention(problems)):
            o_ref[c * CHUNK_Q:(c + 1) * CHUNK_Q, :] = o.astype(bf16)


def _nbr_attention(proj, cache_k, cache_v, bias, li):
    qcol = 3 * CONV_CH // LANES
    ncol = NA_WIDTH // LANES
    hp_steps = NA_HEADS // HEADS_PER_STEP

    def col_spec(which):
        return pl.BlockSpec((DEC_SEQ, LANES), lambda hp, b: (NTP + b, qcol + which * ncol + hp))

    ctx_spec = pl.BlockSpec((None, None, HEADS_PER_STEP, HEAD_DIM, PAST_LEN),
                            lambda hp, b: (b, li, hp, 0, 0))
    return pl.pallas_call(
        _nbr_attn_kernel,
        grid=(hp_steps, DEC_BATCH),
        in_specs=[col_spec(0), col_spec(1), col_spec(2), ctx_spec, ctx_spec,
                  pl.BlockSpec((None, HEADS_PER_STEP, N_CHUNKS, CHUNK_Q, KWIN),
                               lambda hp, b: (li, hp, 0, 0, 0))],
        out_specs=pl.BlockSpec((DEC_SEQ, LANES), lambda hp, b: (b, hp)),
        out_shape=jax.ShapeDtypeStruct((T_S, NA_WIDTH), bf16),
        compiler_params=_cp(("arbitrary", "arbitrary")),
    )(proj, proj, proj, cache_k, cache_v, bias)


def _mixout_kernel(*refs, n_x):
    x_refs, (a_ref, ybp_ref, ybs_ref, g1_ref, cw_ref, w_ref, o_ref) = refs[:n_x], refs[n_x:]
    t = pl.program_id(0)
    a_b = a_ref[:, 0:CONV_CH].astype(f32)
    a_c = a_ref[:, CONV_CH:2 * CONV_CH].astype(f32)
    a_x = a_ref[:, 2 * CONV_CH:3 * CONV_CH].astype(f32)
    u = a_c * a_x
    r = lax.broadcasted_iota(jnp.int32, (TM, 1), 0)
    pos = jnp.where(t < NTP, r % SEQ, r)
    last = jnp.where(t < NTP, SEQ - 1, DEC_SEQ - 1)
    u_prev = jnp.where(pos == 0, 0.0, pltpu.roll(u, 1, axis=0))
    u_next = jnp.where(pos == last, 0.0, pltpu.roll(u, TM - 1, axis=0))
    y_a = a_b * (u_prev * cw_ref[0:1, :] + u * cw_ref[1:2, :] + u_next * cw_ref[2:3, :])
    y_b = jnp.where(t < NTP, ybp_ref[...], ybs_ref[...])
    y = (_dot(y_a.astype(bf16), w_ref[0:CONV_CH, :].astype(bf16))
         + _dot(y_b, w_ref[CONV_CH:, :].astype(bf16)))
    o_ref[...] = _load_x(t, x_refs) + g1_ref[0] * y


def _mixout(proj, yb_p, yb_s, x, mod, conv_w, w_out, layer):
    li = layer // 2
    xs = x if isinstance(x, tuple) else (x,)
    return pl.pallas_call(
        functools.partial(_mixout_kernel, n_x=len(xs)),
        grid=(NT,),
        in_specs=_x_specs(x) + [
            pl.BlockSpec((TM, 3 * CONV_CH), lambda t: (t, 0)),
            pl.BlockSpec((TM, NA_WIDTH), lambda t: (jnp.minimum(t, NTP - 1), 0)),
            pl.BlockSpec((TM, NA_WIDTH), lambda t: (jnp.maximum(t - NTP, 0), 0)),
            _mod_spec(layer, 2),
            pl.BlockSpec((None, 3, CONV_CH), lambda t: (li, 0, 0)),
            pl.BlockSpec((None, D, D), lambda t: (li, 0, 0))],
        out_specs=pl.BlockSpec((TM, D), lambda t: (t, 0)),
        out_shape=jax.ShapeDtypeStruct((T, D), f32),
        compiler_params=_cp(("arbitrary",)),
    )(*xs, proj, yb_p, yb_s, mod, conv_w, w_out)


FFN_TILES = 2


def _ffn_kernel(x_ref, g_ref, *refs):
    mods, (w1_ref, w3_ref, w2_ref, o_ref, h_ref) = refs[:3 * FFN_TILES], refs[3 * FFN_TILES:]
    j = pl.program_id(1)
    parts = [(slice(n * TM, (n + 1) * TM), mods[3 * n:3 * n + 3]) for n in range(FFN_TILES)]

    @pl.when(j == 0)
    def _():
        for rows, (sc_ref, sh_ref, _) in parts:
            h_ref[rows, :] = _norm_mod(x_ref[rows, :], g_ref[0], sc_ref[0], sh_ref[0]).astype(bf16)
        o_ref[...] = jnp.zeros_like(o_ref)

    h = h_ref[...]
    gate = jax.nn.silu(_dot(h, w1_ref[...].astype(bf16))) * _dot(h, w3_ref[...].astype(bf16))
    o_ref[...] += _dot(gate.astype(bf16), w2_ref[...].astype(bf16))

    @pl.when(j == pl.num_programs(1) - 1)
    def _():
        for rows, (_, _, g2_ref) in parts:
            o_ref[rows, :] = x_ref[rows, :] + g2_ref[0] * o_ref[rows, :]


def _dense_ffn(x, mod, norm_g, w1, w3, w2, layer):
    li = layer // 2
    tf = TF_DENSE
    rows = FFN_TILES * TM

    def mod_spec(part, n):
        return pl.BlockSpec((1, 1, D), lambda i, j: (
            (layer * MOD_ROWS + _mod_index(i * FFN_TILES + n)) * 6 + part, 0, 0))

    mod_specs = [mod_spec(part, n) for n in range(FFN_TILES) for part in (4, 3, 5)]
    return pl.pallas_call(
        _ffn_kernel,
        grid=(T // rows, FFN_DENSE // tf),
        in_specs=[pl.BlockSpec((rows, D), lambda i, j: (i, 0)),
                  pl.BlockSpec((1, 1, D), lambda i, j: (layer, 0, 0))] + mod_specs + [
                  pl.BlockSpec((None, D, tf), lambda i, j: (li, 0, j)),
                  pl.BlockSpec((None, D, tf), lambda i, j: (li, 0, j)),
                  pl.BlockSpec((None, tf, D), lambda i, j: (li, j, 0))],
        out_specs=pl.BlockSpec((rows, D), lambda i, j: (i, 0)),
        out_shape=jax.ShapeDtypeStruct((T, D), f32),
        scratch_shapes=[pltpu.VMEM((rows, D), bf16)],
        compiler_params=_cp(("arbitrary", "arbitrary")),
    )(x, norm_g, *([mod] * len(mod_specs)), w1, w3, w2)


GROUP_CH = D // FOURIER_GROUPS


def _dft_mats(n):
    k = np.arange(n, dtype=np.int64)
    ang = 2.0 * np.pi * ((k[:, None] * k[None, :]) % n).astype(np.float64) / n
    return np.cos(ang).astype(np.float32), np.sin(ang).astype(np.float32)


def _fourier_kernel(x_ref, g_ref, sc_ref, sh_ref, g1_ref, cs_ref, ss_ref, cl_ref, sl_ref, wf_ref,
                    o_ref, f_ref):
    t = pl.program_id(0)
    h = _norm_mod(x_ref[...], g_ref[0], sc_ref[0], sh_ref[0]).astype(bf16)
    cs = cs_ref[...].astype(bf16)
    ss = ss_ref[...].astype(bf16)
    ys, zs = [], []
    for g in range(FOURIER_GROUPS):
        hg = h[:, g * GROUP_CH:(g + 1) * GROUP_CH]
        ys.append(_dot(hg, cs))
        zs.append(_dot(hg, ss))
    y = jnp.concatenate(ys, axis=-1).astype(bf16)
    z = jnp.concatenate(zs, axis=-1).astype(bf16)

    @pl.when(t < NTP)
    def _():
        for s in range(TM // SEQ):
            rows = slice(s * SEQ, (s + 1) * SEQ)
            f = _dot(cs, y[rows]) - _dot(ss, z[rows])
            f_ref[rows, :] = (f * ((SEQ * GROUP_CH) ** -0.5)).astype(bf16)

    @pl.when(t >= NTP)
    def _():
        f = _dot(cl_ref[...].astype(bf16), y) - _dot(sl_ref[...].astype(bf16), z)
        f_ref[...] = (f * ((DEC_SEQ * GROUP_CH) ** -0.5)).astype(bf16)

    o_ref[...] = x_ref[...] + g1_ref[0] * _dot(f_ref[...], wf_ref[...].astype(bf16))


def _fourier(x, mod, norm_g, fourier_w, layer):
    li = layer // 2
    c_s, s_s = _dft_mats(SEQ)
    c_l, s_l = _dft_mats(DEC_SEQ)
    const = lambda shape: pl.BlockSpec(shape, lambda t: (0, 0))
    return pl.pallas_call(
        _fourier_kernel,
        grid=(NT,),
        in_specs=[pl.BlockSpec((TM, D), lambda t: (t, 0)),
                  pl.BlockSpec((1, 1, D), lambda t: (layer, 0, 0)),
                  _mod_spec(layer, 1), _mod_spec(layer, 0), _mod_spec(layer, 2),
                  const((SEQ, SEQ)), const((SEQ, SEQ)),
                  const((DEC_SEQ, DEC_SEQ)), const((DEC_SEQ, DEC_SEQ)),
                  pl.BlockSpec((None, D, D), lambda t: (li, 0, 0))],
        out_specs=pl.BlockSpec((TM, D), lambda t: (t, 0)),
        out_shape=jax.ShapeDtypeStruct((T, D), f32),
        scratch_shapes=[pltpu.VMEM((TM, D), bf16)],
        compiler_params=_cp(("arbitrary",)),
    )(x, norm_g, mod, mod, mod, jnp.asarray(c_s), jnp.asarray(s_s), jnp.asarray(c_l), jnp.asarray(s_l),
      fourier_w)


M_E0, M_E1, M_R0, M_R1, M_W0, M_W1 = range(6)
ROW_UNROLL = 8


META_ROWS = 8


def _router_kernel(x_ref, g_ref, sc_ref, sh_ref, rw_ref, rb_ref, h_ref, meta_ref, meta_t_ref, cnt_ref,
                   carry_ref):
    t = pl.program_id(0)

    @pl.when(t == 0)
    def _():
        carry_ref[...] = jnp.zeros_like(carry_ref)

    h = _norm_mod(x_ref[...], g_ref[0], sc_ref[0], sh_ref[0])
    h_ref[...] = h
    h_hi = h.astype(bf16)
    h_lo = (h - h_hi.astype(f32)).astype(bf16)
    hi_terms = _dot(h_hi, rw_ref[...])
    lo_term = _dot(h_lo, rw_ref[:, :LANES])
    logits = ((hi_terms[:, :LANES] + lo_term) + hi_terms[:, LANES:]) + rb_ref[...]
    lane = lax.broadcasted_iota(jnp.int32, (TM, LANES), 1)
    m1 = jnp.max(logits, axis=-1, keepdims=True)
    i1 = jnp.min(jnp.where(logits == m1, lane, LANES), axis=-1, keepdims=True)
    rest = jnp.where(lane == i1, -jnp.inf, logits)
    m2 = jnp.max(rest, axis=-1, keepdims=True)
    i2 = jnp.min(jnp.where(rest == m2, lane, LANES), axis=-1, keepdims=True)
    e = jnp.exp(m2 - m1)
    w0 = 1.0 / (1.0 + e)
    w1 = e / (1.0 + e)
    oh0 = (lane == i1).astype(f32)
    oh1 = (lane == i2).astype(f32)
    oh = oh0 + oh1
    row = lax.broadcasted_iota(jnp.int32, (TM, TM), 0)
    col = lax.broadcasted_iota(jnp.int32, (TM, TM), 1)
    before = jnp.where(col < row, 1.0, 0.0).astype(bf16)
    base = carry_ref[...] + _dot(before, oh.astype(bf16))
    r0 = jnp.sum(oh0 * base, axis=-1, keepdims=True)
    r1 = jnp.sum(oh1 * base, axis=-1, keepdims=True)
    carry_ref[...] += jnp.sum(oh, axis=0, keepdims=True)
    rec = jnp.zeros((TM, LANES), f32)
    for idx, val in ((M_E0, i1.astype(f32)), (M_E1, i2.astype(f32)), (M_R0, r0), (M_R1, r1),
                     (M_W0, w0), (M_W1, w1)):
        rec = jnp.where(lane == idx, val, rec)
    meta_ref[...] = rec
    meta_t_ref[...] = rec.T[:META_ROWS, :]
    cnt_ref[...] = carry_ref[...]


def _router(x, mod, norm_g, router_w, router_b, layer):
    li = layer // 2
    rw = jnp.pad(router_w[li], ((0, 0), (0, LANES - N_EXPERTS)))
    rw_hi = rw.astype(bf16)
    rw_lo = (rw - rw_hi.astype(f32)).astype(bf16)
    rw = jnp.concatenate([rw_hi, rw_lo], axis=1)
    rb = jnp.pad(router_b[li], (0, LANES - N_EXPERTS), constant_values=NEG_INF).reshape(1, LANES)
    return pl.pallas_call(
        _router_kernel,
        grid=(NT,),
        in_specs=[pl.BlockSpec((TM, D), lambda t: (t, 0)),
                  pl.BlockSpec((1, 1, D), lambda t: (layer, 0, 0)),
                  _mod_spec(layer, 4), _mod_spec(layer, 3),
                  pl.BlockSpec((D, 2 * LANES), lambda t: (0, 0)),
                  pl.BlockSpec((1, LANES), lambda t: (0, 0))],
        out_specs=[pl.BlockSpec((TM, D), lambda t: (t, 0)),
                   pl.BlockSpec((TM, LANES), lambda t: (t, 0)),
                   pl.BlockSpec((META_ROWS, TM), lambda t: (0, t)),
                   pl.BlockSpec((1, LANES), lambda t: (0, 0))],
        out_shape=[jax.ShapeDtypeStruct((T, D), f32),
                   jax.ShapeDtypeStruct((T, LANES), f32),
                   jax.ShapeDtypeStruct((META_ROWS, T), f32),
                   jax.ShapeDtypeStruct((1, LANES), f32)],
        scratch_shapes=[pltpu.VMEM((1, LANES), f32)],
        compiler_params=_cp(("arbitrary",)),
    )(x, norm_g, mod, mod, rw, rb)


def _row_copy(src_ref, src_row, dst_ref, dst_row, sem):
    return pltpu.make_async_copy(src_ref.at[pl.ds(src_row, 1)], dst_ref.at[pl.ds(dst_row, 1)], sem)


ZERO_BLOCK = 8


def _dispatch_kernel(lo_ref, hi_ref, nu_ref, dest_ref, h_ref, xg_ref, zero_ref, sem, zsem):
    @pl.when(pl.program_id(0) == 0)
    def _():
        zero_ref[...] = jnp.zeros_like(zero_ref)

        def row_zero(r):
            return _row_copy(zero_ref, 0, xg_ref, r, zsem)

        def block_zero(b):
            return pltpu.make_async_copy(zero_ref.at[pl.ds(0, ZERO_BLOCK)],
                                         xg_ref.at[pl.ds(pl.multiple_of(b * ZERO_BLOCK, ZERO_BLOCK), ZERO_BLOCK)],
                                         zsem)

        def tile_zero(tile):
            return pltpu.make_async_copy(zero_ref, xg_ref.at[pl.ds(pl.multiple_of(tile * TM_E, TM_E), TM_E)],
                                         zsem)

        def each(lo, hi, copy, wait):
            def step(i, c):
                if wait:
                    copy(i).wait()
                else:
                    copy(i).start()
                return c
            lax.fori_loop(lo, hi, step, 0)

        for wait in (False, True):
            for e in range(N_EXPERTS):
                lo, hi = lo_ref[e], hi_ref[e]
                aligned = jnp.minimum((lo + ZERO_BLOCK - 1) // ZERO_BLOCK * ZERO_BLOCK, hi)
                each(lo, aligned, row_zero, wait)
                each(aligned // ZERO_BLOCK, hi // ZERO_BLOCK, block_zero, wait)
            each(nu_ref[0], NT_E, tile_zero, wait)

    def issue(g, c):
        for u in range(ROW_UNROLL):
            r = g * ROW_UNROLL + u
            for k in range(2):
                _row_copy(h_ref, r, xg_ref, dest_ref[0, 0, k * TM + r], sem).start()
        return c

    lax.fori_loop(0, TM // ROW_UNROLL, issue, 0)
    for k in range(2):
        pltpu.make_async_copy(h_ref, xg_ref.at[pl.ds(0, TM)], sem).wait()


def _dispatch(pad_lo, pad_hi, n_used, dest, h):
    grid_spec = pltpu.PrefetchScalarGridSpec(
        num_scalar_prefetch=3,
        grid=(NT,),
        in_specs=[pl.BlockSpec((1, 1, 2 * TM), lambda t, *_: (t, 0, 0), memory_space=pltpu.SMEM),
                  pl.BlockSpec((TM, D), lambda t, *_: (t, 0))],
        out_specs=pl.BlockSpec(memory_space=pl.ANY),
        scratch_shapes=[pltpu.VMEM((TM_E, D), f32), pltpu.SemaphoreType.DMA(()), pltpu.SemaphoreType.DMA(())],
    )
    return pl.pallas_call(
        _dispatch_kernel,
        grid_spec=grid_spec,
        out_shape=jax.ShapeDtypeStruct((R_E, D), f32),
        compiler_params=_cp(("arbitrary",)),
    )(pad_lo, pad_hi, n_used, dest, h)


NJ_EXP = FFN_EXPERT // TF_EXP
ROW_CLASSES = tuple(TM_E * q // 4 for q in (1, 2, 3, 4))


def _expert_kernel(te_ref, nu_ref, rows_ref, xg_ref, w1_ref, w3_ref, w2_ref, y_ref, h_ref):
    t = pl.program_id(0)
    j = pl.program_id(1)
    n_rows = rows_ref[t]

    @pl.when(j == 0)
    def _():
        h_ref[...] = xg_ref[...].astype(bf16)
        y_ref[...] = jnp.zeros_like(y_ref)

    for below, m in zip((0,) + ROW_CLASSES, ROW_CLASSES):
        @pl.when((n_rows > below) & (n_rows <= m))
        def _():
            h = h_ref[:m, :]
            gate = jax.nn.silu(_dot(h, w1_ref[...].astype(bf16))) * _dot(h, w3_ref[...].astype(bf16))
            y_ref[:m, :] += _dot(gate.astype(bf16), w2_ref[...].astype(bf16))


def _experts(tile_expert, n_used, tile_rows, xg, w1, w3, w2, layer):
    li = layer // 2
    tf = TF_EXP
    nj = NJ_EXP

    def jj(t, j, nu):
        return jnp.where(t < nu[0], j, nj - 1)

    def tt(t, nu):
        return jnp.minimum(t, jnp.maximum(nu[0] - 1, 0))

    grid_spec = pltpu.PrefetchScalarGridSpec(
        num_scalar_prefetch=3,
        grid=(NT_E, nj),
        in_specs=[pl.BlockSpec((TM_E, D), lambda t, j, te, nu, nr: (tt(t, nu), 0)),
                  pl.BlockSpec((None, None, D, tf), lambda t, j, te, nu, nr: (li, te[t], 0, jj(t, j, nu))),
                  pl.BlockSpec((None, None, D, tf), lambda t, j, te, nu, nr: (li, te[t], 0, jj(t, j, nu))),
                  pl.BlockSpec((None, None, tf, D), lambda t, j, te, nu, nr: (li, te[t], jj(t, j, nu), 0))],
        out_specs=pl.BlockSpec((TM_E, D), lambda t, j, te, nu, nr: (t, 0)),
        scratch_shapes=[pltpu.VMEM((TM_E, D), bf16)],
    )
    return pl.pallas_call(
        _expert_kernel,
        grid_spec=grid_spec,
        out_shape=jax.ShapeDtypeStruct((R_E, D), f32),
        compiler_params=_cp(("arbitrary", "arbitrary")),
    )(tile_expert, n_used, tile_rows, xg, w1, w3, w2)


def _combine_kernel(dest_ref, x_ref, g2_ref, meta_ref, y_ref, *rest, final):
    if final:
        fg_ref, op_ref, os_ref, buf_ref, sem = rest
    else:
        o_ref, buf_ref, sem = rest

    def issue(g, c):
        for u in range(ROW_UNROLL):
            r = g * ROW_UNROLL + u
            for k in range(2):
                _row_copy(y_ref, dest_ref[0, 0, k * TM + r], buf_ref.at[k], r, sem).start()
        return c

    lax.fori_loop(0, TM // ROW_UNROLL, issue, 0)
    for k in range(2):
        pltpu.make_async_copy(y_ref.at[pl.ds(0, TM)], buf_ref.at[k], sem).wait()
    w0 = meta_ref[:, M_W0:M_W0 + 1]
    w1 = meta_ref[:, M_W1:M_W1 + 1]
    out = x_ref[...] + g2_ref[0] * (w0 * buf_ref[0] + w1 * buf_ref[1])
    if not final:
        o_ref[...] = out
        return
    normed = (out * lax.rsqrt(jnp.mean(out * out, axis=-1, keepdims=True) + RMS_EPS)) * fg_ref[...]
    t = pl.program_id(0)

    @pl.when(t < NTP)
    def _():
        op_ref[...] = normed

    @pl.when(t >= NTP)
    def _():
        os_ref[...] = normed


def _combine(dest, x, mod, meta, y, layer, final_g=None):
    final = final_g is not None
    in_specs = [pl.BlockSpec((1, 1, 2 * TM), lambda t: (t, 0, 0), memory_space=pltpu.SMEM),
                pl.BlockSpec((TM, D), lambda t: (t, 0)),
                _mod_spec(layer, 5),
                pl.BlockSpec((TM, LANES), lambda t: (t, 0)),
                pl.BlockSpec(memory_space=pl.ANY)]
    args = [dest, x, mod, meta, y]
    if final:
        in_specs.append(pl.BlockSpec((1, D), lambda t: (0, 0)))
        args.append(final_g.reshape(1, D))
        out_specs = [pl.BlockSpec((TM, D), lambda t: (jnp.minimum(t, NTP - 1), 0)),
                     pl.BlockSpec((TM, D), lambda t: (jnp.maximum(t - NTP, 0), 0))]
        out_shape = [jax.ShapeDtypeStruct((T_P, D), f32), jax.ShapeDtypeStruct((T_S, D), f32)]
    else:
        out_specs = pl.BlockSpec((TM, D), lambda t: (t, 0))
        out_shape = jax.ShapeDtypeStruct((T, D), f32)
    return pl.pallas_call(
        functools.partial(_combine_kernel, final=final),
        grid=(NT,),
        in_specs=in_specs,
        out_specs=out_specs,
        out_shape=out_shape,
        scratch_shapes=[pltpu.VMEM((2, TM, D), f32), pltpu.SemaphoreType.DMA(())],
        compiler_params=_cp(("arbitrary",)),
    )(*args)


def _moe(x, mod, norm_g, router_w, router_b, w1, w3, w2, layer, final_g=None):
    h_rows, meta, meta_t, counts = _router(x, mod, norm_g, router_w, router_b, layer)
    cnt = counts[0, :N_EXPERTS].astype(jnp.int32)
    padded = ((cnt + TM_E - 1) // TM_E) * TM_E
    ends = jnp.cumsum(padded)
    starts = ends - padded
    experts = meta_t[M_E0:M_E1 + 1].astype(jnp.int32)
    ranks = meta_t[M_R0:M_R1 + 1].astype(jnp.int32)
    start_of = functools.reduce(lambda acc, e: jnp.where(experts == e, starts[e], acc), range(N_EXPERTS), 0)
    dest = start_of + ranks
    dest = dest.reshape(2, NT, TM).transpose(1, 0, 2).reshape(NT, 1, 2 * TM)
    n_used = (ends[-1] // TM_E).astype(jnp.int32).reshape(1)
    tile_start = jnp.minimum(jnp.arange(NT_E, dtype=jnp.int32), n_used[0] - 1) * TM_E
    tile_expert = jnp.sum((tile_start[:, None] >= ends[None, :]).astype(jnp.int32), axis=1)
    tile_expert = jnp.minimum(tile_expert, N_EXPERTS - 1).astype(jnp.int32)
    tile_ids = jnp.arange(NT_E, dtype=jnp.int32)
    tile_rows = jnp.clip((starts + cnt)[tile_expert] - tile_ids * TM_E, 0, TM_E)
    tile_rows = jnp.where(tile_ids < n_used[0], tile_rows, 0).astype(jnp.int32)
    xg = _dispatch((starts + cnt).astype(jnp.int32), ends.astype(jnp.int32), n_used, dest, h_rows)
    y = _experts(tile_expert, n_used, tile_rows, xg, w1, w3, w2, layer)
    return _combine(dest, x, mod, meta, y, layer, final_g)


def kernel(x_prompt, x_sample, c, cache_k, cache_v, c_ctx, ada_w, ada_b, norm1_g, norm2_g, w_in, conv_w, rpb,
           w_out, ffn_w1, ffn_w3, ffn_w2, fourier_w, router_w, router_b, moe_w1, moe_w3, moe_w2, final_g):
    cvecs = jnp.concatenate([c_ctx[None, :], c, jnp.zeros((MOD_ROWS - 1 - DEC_BATCH, D), f32)], axis=0)
    mod = _modulation(cvecs, ada_w, ada_b).reshape(DEPTH * MOD_ROWS * 6, 1, D)
    n1 = norm1_g.reshape(DEPTH, 1, D)
    n2 = norm2_g.reshape(DEPTH, 1, D)
    bias = _nbr_bias(rpb)
    cache_kt = jnp.swapaxes(cache_k, -1, -2)
    cache_vt = jnp.swapaxes(cache_v, -1, -2)
    x = (x_prompt.reshape(T_P, D), x_sample.reshape(T_S, D))
    caches = [jnp.zeros((BATCH, N_EVEN, NA_HEADS, HEAD_DIM, SEQ), f32) for _ in range(2)]
    for layer in range(DEPTH):
        li = layer // 2
        if layer % 2 == 0:
            proj, *caches = _inproj(x, mod, n1, w_in, layer, caches)
            yb_p = _ctx_attention(proj)
            yb_s = _nbr_attention(proj, cache_kt, cache_vt, bias, li)
            x = _mixout(proj, yb_p, yb_s, x, mod, conv_w, w_out, layer)
            x = _dense_ffn(x, mod, n2, ffn_w1, ffn_w3, ffn_w2, layer)
        else:
            x = _fourier(x, mod, n1, fourier_w, layer)
            last = layer == DEPTH - 1
            x = _moe(x, mod, n2, router_w, router_b, moe_w1, moe_w3, moe_w2, layer, final_g if last else None)
    y_prompt, y_sample = x
    new_kt, new_vt = caches
    return (y_prompt.reshape(BATCH, SEQ, D), y_sample.reshape(DEC_BATCH, DEC_SEQ, D),
            jnp.swapaxes(new_kt, -1, -2), jnp.swapaxes(new_vt, -1, -2))
```

```python
import functools
from typing import Any, NamedTuple

import numpy as np
import jax
import jax.numpy as jnp
from jax import lax
from jax.experimental import pallas as pl
from jax.experimental.pallas import tpu as pltpu

f32 = jnp.float32
bf16 = jnp.bfloat16

D = 1024
BATCH = 16
SEQ = 256
DEPTH = 4
DEC_BATCH = 8
DEC_SEQ = 1024
PAST_LEN = 512
GRID_W = 64
CONV_CH = 512
NA_HEADS = 8
HEAD_DIM = 64
NA_WIDTH = 512
WIN_ROWS = 8
WIN_COLS = 16
IN_WIDTH = 3072
FOURIER_GROUPS = 4
FFN_DENSE = 2816
N_EXPERTS = 8
FFN_EXPERT = 3584
N_EVEN = 2
RMS_EPS = 1e-6
NEG_INF = -1e30
SCALE = HEAD_DIM ** -0.5

LANES = 128
TM = 1024
T_P = BATCH * SEQ
T_S = DEC_BATCH * DEC_SEQ
T = T_P + T_S
NT = T // TM
NTP = T_P // TM
MOD_ROWS = 16
GRID_ROWS = DEC_SEQ // GRID_W
CHUNK_ROWS = 4
N_CHUNKS = GRID_ROWS // CHUNK_ROWS
CHUNK_Q = CHUNK_ROWS * GRID_W
KWIN_ROWS = 12
KWIN = KWIN_ROWS * GRID_W
KB0 = (0, 0, 4, 8)
KW_ROWS = (8, 12, 12, 8)
CHUNKS_PER_GROUP = 4
TF_DENSE = 256
TF_EXP = 512
TM_E = 1024
N_ASSIGN = 2 * T
NT_E = N_ASSIGN // TM_E + N_EXPERTS
R_E = NT_E * TM_E
VMEM_LIMIT = 56 * 1024 * 1024


def _cp(sem, vmem=VMEM_LIMIT):
    return pltpu.CompilerParams(dimension_semantics=sem, vmem_limit_bytes=vmem)


def _mod_index(t):
    return jnp.where(t < NTP, 0, t - (NTP - 1))


def _mod_spec(layer, part):
    def index(t, *_):
        return ((layer * MOD_ROWS + _mod_index(t)) * 6 + part, 0, 0)
    return pl.BlockSpec((1, 1, D), index)


def _norm_mod(x, g, sc, sh):
    y = x * lax.rsqrt(jnp.mean(x * x, axis=-1, keepdims=True) + RMS_EPS)
    return (y * g) * (1 + sc) + sh


def _dot(a, b):
    return jnp.dot(a, b, preferred_element_type=f32)


def _dot_nt(a, b):
    return lax.dot_general(a, b, (((1,), (1,)), ((), ())), preferred_element_type=f32)


def _mod_kernel(cv_ref, w_ref, b_ref, o_ref):
    s = jax.nn.silu(cv_ref[...]).astype(bf16)
    o_ref[0] = _dot(s, w_ref[0].astype(bf16)) + b_ref[0]


def _modulation(cvecs, ada_w, ada_b):
    tn = 1536
    return pl.pallas_call(
        _mod_kernel,
        grid=(DEPTH, 6 * D // tn),
        in_specs=[pl.BlockSpec((MOD_ROWS, D), lambda l, j: (0, 0)),
                  pl.BlockSpec((1, D, tn), lambda l, j: (l, 0, j)),
                  pl.BlockSpec((1, 1, tn), lambda l, j: (l, 0, j))],
        out_specs=pl.BlockSpec((1, MOD_ROWS, tn), lambda l, j: (l, 0, j)),
        out_shape=jax.ShapeDtypeStruct((DEPTH, MOD_ROWS, 6 * D), f32),
        compiler_params=_cp(("arbitrary", "arbitrary")),
    )(cvecs, ada_w, ada_b.reshape(DEPTH, 1, 6 * D))


TN_IN = IN_WIDTH // 2


def _x_specs(x):
    if isinstance(x, tuple):
        return [pl.BlockSpec((TM, D), lambda i, *_: (jnp.minimum(i, NTP - 1), 0)),
                pl.BlockSpec((TM, D), lambda i, *_: (jnp.maximum(i - NTP, 0), 0))]
    return [pl.BlockSpec((TM, D), lambda i, *_: (i, 0))]


def _load_x(i, x_refs):
    if len(x_refs) == 2:
        return jnp.where(i < NTP, x_refs[0][...], x_refs[1][...])
    return x_refs[0][...]


def _inproj_kernel(*refs, n_x, n_alias):
    x_refs, refs = refs[:n_x], refs[n_x + n_alias:]
    g_ref, sc_ref, sh_ref, w_ref, proj_ref, kt_ref, vt_ref, h_ref = refs
    i = pl.program_id(0)
    j = pl.program_id(1)

    @pl.when(j == 0)
    def _():
        h_ref[...] = _norm_mod(_load_x(i, x_refs), g_ref[0], sc_ref[0], sh_ref[0]).astype(bf16)

    acc = _dot(h_ref[...], w_ref[...].astype(bf16))
    proj_ref[...] = acc.astype(bf16)

    @pl.when((j == 1) & (i < NTP))
    def _():
        for ref, col0 in ((kt_ref, NA_WIDTH), (vt_ref, 2 * NA_WIDTH)):
            for s in range(TM // SEQ):
                t = acc[s * SEQ:(s + 1) * SEQ, col0:col0 + NA_WIDTH].T
                for hd in range(NA_HEADS):
                    ref[s, hd] = t[hd * HEAD_DIM:(hd + 1) * HEAD_DIM, :]


def _inproj(x, mod, norm_g, w_in, layer, caches):
    li = layer // 2
    tn = TN_IN
    seqs = TM // SEQ
    xs = x if isinstance(x, tuple) else (x,)
    cache_spec = pl.BlockSpec((seqs, None, NA_HEADS, HEAD_DIM, SEQ),
                              lambda i, j: (jnp.minimum(i, NTP - 1), li, 0, 0, 0))
    cache_shape = jax.ShapeDtypeStruct((BATCH, N_EVEN, NA_HEADS, HEAD_DIM, SEQ), f32)
    aliased = tuple(caches)
    return pl.pallas_call(
        functools.partial(_inproj_kernel, n_x=len(xs), n_alias=len(aliased)),
        grid=(NT, IN_WIDTH // tn),
        in_specs=_x_specs(x) + [pl.BlockSpec(memory_space=pl.ANY)] * len(aliased) + [
            pl.BlockSpec((1, 1, D), lambda i, j: (layer, 0, 0)),
            _mod_spec(layer, 1), _mod_spec(layer, 0),
            pl.BlockSpec((None, D, tn), lambda i, j: (li, 0, j))],
        out_specs=[pl.BlockSpec((TM, tn), lambda i, j: (i, j)), cache_spec, cache_spec],
        out_shape=[jax.ShapeDtypeStruct((T, IN_WIDTH), bf16), cache_shape, cache_shape],
        scratch_shapes=[pltpu.VMEM((TM, D), bf16)],
        input_output_aliases={len(xs) + n: 1 + n for n in range(len(aliased))},
        compiler_params=_cp(("arbitrary", "arbitrary")),
    )(*xs, *aliased, norm_g, mod, mod, w_in)


HEADS_PER_STEP = LANES // HEAD_DIM


class KeyValues(NamedTuple):
    k: jax.Array
    v: jax.Array
    bias: Any = None
    feature_major: bool = False


def _pair_attention(problems):
    first = lax.broadcasted_iota(jnp.int32, (1, LANES), 1) < HEAD_DIM
    mine = (first, jnp.logical_not(first))
    first_t = lax.broadcasted_iota(jnp.int32, (LANES, 1), 0) < HEAD_DIM
    mine_t = (first_t, jnp.logical_not(first_t))
    scores = []
    for q, parts in problems:
        q = q * SCALE
        for hh in range(HEADS_PER_STEP):
            qh = jnp.where(mine[hh], q, 0)
            ss = []
            for part in parts:
                s = _dot(qh, part.k) if part.feature_major else _dot_nt(qh, part.k)
                ss.append(s if part.bias is None else s + part.bias[hh])
            scores.append(ss)
    probs = []
    for ss in scores:
        m = functools.reduce(jnp.maximum, [jnp.max(s, axis=-1, keepdims=True) for s in ss])
        probs.append([jnp.exp((s - m).astype(bf16)) for s in ss])

    def p_times_v(p, part, hh):
        if part.feature_major:
            return _dot_nt(p, jnp.where(mine_t[hh], part.v, 1))
        return _dot(p, jnp.where(mine[hh], part.v, 1))

    results = []
    for n, (_, parts) in enumerate(problems):
        outs = []
        for hh in range(HEADS_PER_STEP):
            pv = functools.reduce(jnp.add, [p_times_v(p, part, hh)
                                            for p, part in zip(probs[n * HEADS_PER_STEP + hh], parts)])
            denom = pv[:, HEAD_DIM:HEAD_DIM + 1] if hh == 0 else pv[:, 0:1]
            outs.append(pv / denom)
        results.append(jnp.where(first, outs[0], outs[1]))
    return results


def _ctx_attn_kernel(q_ref, k_ref, v_ref, o_ref):
    cols = [slice(hp * LANES, (hp + 1) * LANES) for hp in range(NA_HEADS // HEADS_PER_STEP)]
    outs = _pair_attention([(q_ref[:, sl], [KeyValues(k_ref[:, sl], v_ref[:, sl])]) for sl in cols])
    for sl, o in zip(cols, outs):
        o_ref[:, sl] = o.astype(bf16)


def _ctx_attention(proj):
    col = IN_WIDTH // NA_WIDTH - 3
    return pl.pallas_call(
        _ctx_attn_kernel,
        grid=(BATCH,),
        in_specs=[pl.BlockSpec((SEQ, NA_WIDTH), lambda b: (b, col)),
                  pl.BlockSpec((SEQ, NA_WIDTH), lambda b: (b, col + 1)),
                  pl.BlockSpec((SEQ, NA_WIDTH), lambda b: (b, col + 2))],
        out_specs=pl.BlockSpec((SEQ, NA_WIDTH), lambda b: (b, 0)),
        out_shape=jax.ShapeDtypeStruct((T_P, NA_WIDTH), bf16),
        compiler_params=_cp(("arbitrary",)),
    )(proj, proj, proj)


def _win_start(qr):
    return min(max(qr - WIN_ROWS // 2, 0), GRID_ROWS - WIN_ROWS)


N_ROFF = 2 * WIN_ROWS - 1
N_COFF = 2 * WIN_COLS - 1


def _bias_kernel(rpb_ref, o_ref):
    base = (pl.program_id(0) * NA_HEADS + pl.program_id(1)) * (N_ROFF * N_COFF)
    qc = lax.broadcasted_iota(jnp.int32, (GRID_W, GRID_W), 0)
    kc = lax.broadcasted_iota(jnp.int32, (GRID_W, GRID_W), 1)
    coff = kc - qc + (WIN_COLS - 1)
    cs = jnp.clip(qc - WIN_COLS // 2, 0, GRID_W - WIN_COLS)
    valid = (kc >= cs) & (kc < cs + WIN_COLS)
    neg = jnp.full((GRID_W, GRID_W), NEG_INF, f32)
    blocks = []
    for a in range(N_ROFF):
        t = jnp.zeros((GRID_W, GRID_W), f32)
        for b in range(N_COFF):
            t = jnp.where(coff == b, rpb_ref[base + a * N_COFF + b], t)
        blocks.append(jnp.where(valid, t, neg))
    for c in range(N_CHUNKS):
        for ql in range(CHUNK_ROWS):
            qr = c * CHUNK_ROWS + ql
            st = _win_start(qr)
            for kl in range(KWIN_ROWS):
                kr = KB0[c] + kl
                blk = blocks[kr - qr + WIN_ROWS - 1] if st <= kr < st + WIN_ROWS else neg
                o_ref[c, ql * GRID_W:(ql + 1) * GRID_W, kl * GRID_W:(kl + 1) * GRID_W] = blk


def _nbr_bias(rpb):
    return pl.pallas_call(
        _bias_kernel,
        grid=(N_EVEN, NA_HEADS),
        in_specs=[pl.BlockSpec(memory_space=pltpu.SMEM)],
        out_specs=pl.BlockSpec((None, None, N_CHUNKS, CHUNK_Q, KWIN), lambda i, h: (i, h, 0, 0, 0)),
        out_shape=jax.ShapeDtypeStruct((N_EVEN, NA_HEADS, N_CHUNKS, CHUNK_Q, KWIN), f32),
        compiler_params=_cp(("arbitrary", "arbitrary")),
    )(rpb.reshape(-1))


def _nbr_attn_kernel(q_ref, k_ref, v_ref, kc_ref, vc_ref, bias_ref, o_ref):
    ctx = KeyValues(kc_ref[...].reshape(LANES, PAST_LEN).astype(bf16),
                    vc_ref[...].reshape(LANES, PAST_LEN).astype(bf16), feature_major=True)
    for c0 in range(0, N_CHUNKS, CHUNKS_PER_GROUP):
        problems = []
        for c in range(c0, c0 + CHUNKS_PER_GROUP):
            rows = slice(c * CHUNK_Q, (c + 1) * CHUNK_Q)
            n_keys = KW_ROWS[c] * GRID_W
            win = slice(KB0[c] * GRID_W, KB0[c] * GRID_W + n_keys)
            bias = [bias_ref[hh, c, :, :n_keys] for hh in range(HEADS_PER_STEP)]
            problems.append((q_ref[rows, :], [KeyValues(k_ref[win, :], v_ref[win, :], bias), ctx]))
        for c, o in zip(range(c0, c0 + CHUNKS_PER_GROUP), _pair_attention(problems)):
            o_ref[c * CHUNK_Q:(c + 1) * CHUNK_Q, :] = o.astype(bf16)


def _nbr_attention(proj, cache_k, cache_v, bias, li):
    qcol = 3 * CONV_CH // LANES
    ncol = NA_WIDTH // LANES
    hp_steps = NA_HEADS // HEADS_PER_STEP

    def col_spec(which):
        return pl.BlockSpec((DEC_SEQ, LANES), lambda hp, b: (NTP + b, qcol + which * ncol + hp))

    ctx_spec = pl.BlockSpec((None, None, HEADS_PER_STEP, HEAD_DIM, PAST_LEN),
                            lambda hp, b: (b, li, hp, 0, 0))
    return pl.pallas_call(
        _nbr_attn_kernel,
        grid=(hp_steps, DEC_BATCH),
        in_specs=[col_spec(0), col_spec(1), col_spec(2), ctx_spec, ctx_spec,
                  pl.BlockSpec((None, HEADS_PER_STEP, N_CHUNKS, CHUNK_Q, KWIN),
                               lambda hp, b: (li, hp, 0, 0, 0))],
        out_specs=pl.BlockSpec((DEC_SEQ, LANES), lambda hp, b: (b, hp)),
        out_shape=jax.ShapeDtypeStruct((T_S, NA_WIDTH), bf16),
        compiler_params=_cp(("arbitrary", "arbitrary")),
    )(proj, proj, proj, cache_k, cache_v, bias)


def _mixout_kernel(*refs, n_x):
    x_refs, (a_ref, ybp_ref, ybs_ref, g1_ref, cw_ref, w_ref, o_ref) = refs[:n_x], refs[n_x:]
    t = pl.program_id(0)
    a_b = a_ref[:, 0:CONV_CH].astype(f32)
    a_c = a_ref[:, CONV_CH:2 * CONV_CH].astype(f32)
    a_x = a_ref[:, 2 * CONV_CH:3 * CONV_CH].astype(f32)
    u = a_c * a_x
    r = lax.broadcasted_iota(jnp.int32, (TM, 1), 0)
    pos = jnp.where(t < NTP, r % SEQ, r)
    last = jnp.where(t < NTP, SEQ - 1, DEC_SEQ - 1)
    u_prev = jnp.where(pos == 0, 0.0, pltpu.roll(u, 1, axis=0))
    u_next = jnp.where(pos == last, 0.0, pltpu.roll(u, TM - 1, axis=0))
    y_a = a_b * (u_prev * cw_ref[0:1, :] + u * cw_ref[1:2, :] + u_next * cw_ref[2:3, :])
    y_b = jnp.where(t < NTP, ybp_ref[...], ybs_ref[...])
    y = (_dot(y_a.astype(bf16), w_ref[0:CONV_CH, :].astype(bf16))
         + _dot(y_b, w_ref[CONV_CH:, :].astype(bf16)))
    o_ref[...] = _load_x(t, x_refs) + g1_ref[0] * y


def _mixout(proj, yb_p, yb_s, x, mod, conv_w, w_out, layer):
    li = layer // 2
    xs = x if isinstance(x, tuple) else (x,)
    return pl.pallas_call(
        functools.partial(_mixout_kernel, n_x=len(xs)),
        grid=(NT,),
        in_specs=_x_specs(x) + [
            pl.BlockSpec((TM, 3 * CONV_CH), lambda t: (t, 0)),
            pl.BlockSpec((TM, NA_WIDTH), lambda t: (jnp.minimum(t, NTP - 1), 0)),
            pl.BlockSpec((TM, NA_WIDTH), lambda t: (jnp.maximum(t - NTP, 0), 0)),
            _mod_spec(layer, 2),
            pl.BlockSpec((None, 3, CONV_CH), lambda t: (li, 0, 0)),
            pl.BlockSpec((None, D, D), lambda t: (li, 0, 0))],
        out_specs=pl.BlockSpec((TM, D), lambda t: (t, 0)),
        out_shape=jax.ShapeDtypeStruct((T, D), f32),
        compiler_params=_cp(("arbitrary",)),
    )(*xs, proj, yb_p, yb_s, mod, conv_w, w_out)


FFN_TILES = 2


def _ffn_kernel(x_ref, g_ref, *refs):
    mods, (w1_ref, w3_ref, w2_ref, o_ref, h_ref) = refs[:3 * FFN_TILES], refs[3 * FFN_TILES:]
    j = pl.program_id(1)
    parts = [(slice(n * TM, (n + 1) * TM), mods[3 * n:3 * n + 3]) for n in range(FFN_TILES)]

    @pl.when(j == 0)
    def _():
        for rows, (sc_ref, sh_ref, _) in parts:
            h_ref[rows, :] = _norm_mod(x_ref[rows, :], g_ref[0], sc_ref[0], sh_ref[0]).astype(bf16)
        o_ref[...] = jnp.zeros_like(o_ref)

    h = h_ref[...]
    gate = jax.nn.silu(_dot(h, w1_ref[...].astype(bf16))) * _dot(h, w3_ref[...].astype(bf16))
    o_ref[...] += _dot(gate.astype(bf16), w2_ref[...].astype(bf16))

    @pl.when(j == pl.num_programs(1) - 1)
    def _():
        for rows, (_, _, g2_ref) in parts:
            o_ref[rows, :] = x_ref[rows, :] + g2_ref[0] * o_ref[rows, :]


def _dense_ffn(x, mod, norm_g, w1, w3, w2, layer):
    li = layer // 2
    tf = TF_DENSE
    rows = FFN_TILES * TM

    def mod_spec(part, n):
        return pl.BlockSpec((1, 1, D), lambda i, j: (
            (layer * MOD_ROWS + _mod_index(i * FFN_TILES + n)) * 6 + part, 0, 0))

    mod_specs = [mod_spec(part, n) for n in range(FFN_TILES) for part in (4, 3, 5)]
    return pl.pallas_call(
        _ffn_kernel,
        grid=(T // rows, FFN_DENSE // tf),
        in_specs=[pl.BlockSpec((rows, D), lambda i, j: (i, 0)),
                  pl.BlockSpec((1, 1, D), lambda i, j: (layer, 0, 0))] + mod_specs + [
                  pl.BlockSpec((None, D, tf), lambda i, j: (li, 0, j)),
                  pl.BlockSpec((None, D, tf), lambda i, j: (li, 0, j)),
                  pl.BlockSpec((None, tf, D), lambda i, j: (li, j, 0))],
        out_specs=pl.BlockSpec((rows, D), lambda i, j: (i, 0)),
        out_shape=jax.ShapeDtypeStruct((T, D), f32),
        scratch_shapes=[pltpu.VMEM((rows, D), bf16)],
        compiler_params=_cp(("arbitrary", "arbitrary")),
    )(x, norm_g, *([mod] * len(mod_specs)), w1, w3, w2)


GROUP_CH = D // FOURIER_GROUPS


def _dft_mats(n):
    k = np.arange(n, dtype=np.int64)
    ang = 2.0 * np.pi * ((k[:, None] * k[None, :]) % n).astype(np.float64) / n
    return np.cos(ang).astype(np.float32), np.sin(ang).astype(np.float32)


def _fourier_kernel(x_ref, g_ref, sc_ref, sh_ref, g1_ref, cs_ref, ss_ref, cl_ref, sl_ref, wf_ref,
                    o_ref, f_ref):
    t = pl.program_id(0)
    h = _norm_mod(x_ref[...], g_ref[0], sc_ref[0], sh_ref[0]).astype(bf16)
    cs = cs_ref[...].astype(bf16)
    ss = ss_ref[...].astype(bf16)
    ys, zs = [], []
    for g in range(FOURIER_GROUPS):
        hg = h[:, g * GROUP_CH:(g + 1) * GROUP_CH]
        ys.append(_dot(hg, cs))
        zs.append(_dot(hg, ss))
    y = jnp.concatenate(ys, axis=-1).astype(bf16)
    z = jnp.concatenate(zs, axis=-1).astype(bf16)

    @pl.when(t < NTP)
    def _():
        for s in range(TM // SEQ):
            rows = slice(s * SEQ, (s + 1) * SEQ)
            f = _dot(cs, y[rows]) - _dot(ss, z[rows])
            f_ref[rows, :] = (f * ((SEQ * GROUP_CH) ** -0.5)).astype(bf16)

    @pl.when(t >= NTP)
    def _():
        f = _dot(cl_ref[...].astype(bf16), y) - _dot(sl_ref[...].astype(bf16), z)
        f_ref[...] = (f * ((DEC_SEQ * GROUP_CH) ** -0.5)).astype(bf16)

    o_ref[...] = x_ref[...] + g1_ref[0] * _dot(f_ref[...], wf_ref[...].astype(bf16))


def _fourier(x, mod, norm_g, fourier_w, layer):
    li = layer // 2
    c_s, s_s = _dft_mats(SEQ)
    c_l, s_l = _dft_mats(DEC_SEQ)
    const = lambda shape: pl.BlockSpec(shape, lambda t: (0, 0))
    return pl.pallas_call(
        _fourier_kernel,
        grid=(NT,),
        in_specs=[pl.BlockSpec((TM, D), lambda t: (t, 0)),
                  pl.BlockSpec((1, 1, D), lambda t: (layer, 0, 0)),
                  _mod_spec(layer, 1), _mod_spec(layer, 0), _mod_spec(layer, 2),
                  const((SEQ, SEQ)), const((SEQ, SEQ)),
                  const((DEC_SEQ, DEC_SEQ)), const((DEC_SEQ, DEC_SEQ)),
                  pl.BlockSpec((None, D, D), lambda t: (li, 0, 0))],
        out_specs=pl.BlockSpec((TM, D), lambda t: (t, 0)),
        out_shape=jax.ShapeDtypeStruct((T, D), f32),
        scratch_shapes=[pltpu.VMEM((TM, D), bf16)],
        compiler_params=_cp(("arbitrary",)),
    )(x, norm_g, mod, mod, mod, jnp.asarray(c_s), jnp.asarray(s_s), jnp.asarray(c_l), jnp.asarray(s_l),
      fourier_w)


M_E0, M_E1, M_R0, M_R1, M_W0, M_W1 = range(6)
ROW_UNROLL = 8


META_ROWS = 8


def _router_kernel(x_ref, g_ref, sc_ref, sh_ref, rw_ref, rb_ref, h_ref, meta_ref, meta_t_ref, cnt_ref,
                   carry_ref):
    t = pl.program_id(0)

    @pl.when(t == 0)
    def _():
        carry_ref[...] = jnp.zeros_like(carry_ref)

    h = _norm_mod(x_ref[...], g_ref[0], sc_ref[0], sh_ref[0])
    h_ref[...] = h
    h_hi = h.astype(bf16)
    h_lo = (h - h_hi.astype(f32)).astype(bf16)
    hi_terms = _dot(h_hi, rw_ref[...])
    lo_term = _dot(h_lo, rw_ref[:, :LANES])
    logits = ((hi_terms[:, :LANES] + lo_term) + hi_terms[:, LANES:]) + rb_ref[...]
    lane = lax.broadcasted_iota(jnp.int32, (TM, LANES), 1)
    m1 = jnp.max(logits, axis=-1, keepdims=True)
    i1 = jnp.min(jnp.where(logits == m1, lane, LANES), axis=-1, keepdims=True)
    rest = jnp.where(lane == i1, -jnp.inf, logits)
    m2 = jnp.max(rest, axis=-1, keepdims=True)
    i2 = jnp.min(jnp.where(rest == m2, lane, LANES), axis=-1, keepdims=True)
    e = jnp.exp(m2 - m1)
    w0 = 1.0 / (1.0 + e)
    w1 = e / (1.0 + e)
    oh0 = (lane == i1).astype(f32)
    oh1 = (lane == i2).astype(f32)
    oh = oh0 + oh1
    row = lax.broadcasted_iota(jnp.int32, (TM, TM), 0)
    col = lax.broadcasted_iota(jnp.int32, (TM, TM), 1)
    before = jnp.where(col < row, 1.0, 0.0).astype(bf16)
    base = carry_ref[...] + _dot(before, oh.astype(bf16))
    r0 = jnp.sum(oh0 * base, axis=-1, keepdims=True)
    r1 = jnp.sum(oh1 * base, axis=-1, keepdims=True)
    carry_ref[...] += jnp.sum(oh, axis=0, keepdims=True)
    rec = jnp.zeros((TM, LANES), f32)
    for idx, val in ((M_E0, i1.astype(f32)), (M_E1, i2.astype(f32)), (M_R0, r0), (M_R1, r1),
                     (M_W0, w0), (M_W1, w1)):
        rec = jnp.where(lane == idx, val, rec)
    meta_ref[...] = rec
    meta_t_ref[...] = rec.T[:META_ROWS, :]
    cnt_ref[...] = carry_ref[...]


def _router(x, mod, norm_g, router_w, router_b, layer):
    li = layer // 2
    rw = jnp.pad(router_w[li], ((0, 0), (0, LANES - N_EXPERTS)))
    rw_hi = rw.astype(bf16)
    rw_lo = (rw - rw_hi.astype(f32)).astype(bf16)
    rw = jnp.concatenate([rw_hi, rw_lo], axis=1)
    rb = jnp.pad(router_b[li], (0, LANES - N_EXPERTS), constant_values=NEG_INF).reshape(1, LANES)
    return pl.pallas_call(
        _router_kernel,
        grid=(NT,),
        in_specs=[pl.BlockSpec((TM, D), lambda t: (t, 0)),
                  pl.BlockSpec((1, 1, D), lambda t: (layer, 0, 0)),
                  _mod_spec(layer, 4), _mod_spec(layer, 3),
                  pl.BlockSpec((D, 2 * LANES), lambda t: (0, 0)),
                  pl.BlockSpec((1, LANES), lambda t: (0, 0))],
        out_specs=[pl.BlockSpec((TM, D), lambda t: (t, 0)),
                   pl.BlockSpec((TM, LANES), lambda t: (t, 0)),
                   pl.BlockSpec((META_ROWS, TM), lambda t: (0, t)),
                   pl.BlockSpec((1, LANES), lambda t: (0, 0))],
        out_shape=[jax.ShapeDtypeStruct((T, D), f32),
                   jax.ShapeDtypeStruct((T, LANES), f32),
                   jax.ShapeDtypeStruct((META_ROWS, T), f32),
                   jax.ShapeDtypeStruct((1, LANES), f32)],
        scratch_shapes=[pltpu.VMEM((1, LANES), f32)],
        compiler_params=_cp(("arbitrary",)),
    )(x, norm_g, mod, mod, rw, rb)


def _row_copy(src_ref, src_row, dst_ref, dst_row, sem):
    return pltpu.make_async_copy(src_ref.at[pl.ds(src_row, 1)], dst_ref.at[pl.ds(dst_row, 1)], sem)


ZERO_BLOCK = 8


def _dispatch_kernel(lo_ref, hi_ref, nu_ref, dest_ref, h_ref, xg_ref, zero_ref, sem, zsem):
    @pl.when(pl.program_id(0) == 0)
    def _():
        zero_ref[...] = jnp.zeros_like(zero_ref)

        def row_zero(r):
            return _row_copy(zero_ref, 0, xg_ref, r, zsem)

        def block_zero(b):
            return pltpu.make_async_copy(zero_ref.at[pl.ds(0, ZERO_BLOCK)],
                                         xg_ref.at[pl.ds(pl.multiple_of(b * ZERO_BLOCK, ZERO_BLOCK), ZERO_BLOCK)],
                                         zsem)

        def tile_zero(tile):
            return pltpu.make_async_copy(zero_ref, xg_ref.at[pl.ds(pl.multiple_of(tile * TM_E, TM_E), TM_E)],
                                         zsem)

        def each(lo, hi, copy, wait):
            def step(i, c):
                if wait:
                    copy(i).wait()
                else:
                    copy(i).start()
                return c
            lax.fori_loop(lo, hi, step, 0)

        for wait in (False, True):
            for e in range(N_EXPERTS):
                lo, hi = lo_ref[e], hi_ref[e]
                aligned = jnp.minimum((lo + ZERO_BLOCK - 1) // ZERO_BLOCK * ZERO_BLOCK, hi)
                each(lo, aligned, row_zero, wait)
                each(aligned // ZERO_BLOCK, hi // ZERO_BLOCK, block_zero, wait)
            each(nu_ref[0], NT_E, tile_zero, wait)

    def issue(g, c):
        for u in range(ROW_UNROLL):
            r = g * ROW_UNROLL + u
            for k in range(2):
                _row_copy(h_ref, r, xg_ref, dest_ref[0, 0, k * TM + r], sem).start()
        return c

    lax.fori_loop(0, TM // ROW_UNROLL, issue, 0)
    for k in range(2):
        pltpu.make_async_copy(h_ref, xg_ref.at[pl.ds(0, TM)], sem).wait()


def _dispatch(pad_lo, pad_hi, n_used, dest, h):
    grid_spec = pltpu.PrefetchScalarGridSpec(
        num_scalar_prefetch=3,
        grid=(NT,),
        in_specs=[pl.BlockSpec((1, 1, 2 * TM), lambda t, *_: (t, 0, 0), memory_space=pltpu.SMEM),
                  pl.BlockSpec((TM, D), lambda t, *_: (t, 0))],
        out_specs=pl.BlockSpec(memory_space=pl.ANY),
        scratch_shapes=[pltpu.VMEM((TM_E, D), f32), pltpu.SemaphoreType.DMA(()), pltpu.SemaphoreType.DMA(())],
    )
    return pl.pallas_call(
        _dispatch_kernel,
        grid_spec=grid_spec,
        out_shape=jax.ShapeDtypeStruct((R_E, D), f32),
        compiler_params=_cp(("arbitrary",)),
    )(pad_lo, pad_hi, n_used, dest, h)


NJ_EXP = FFN_EXPERT // TF_EXP
ROW_CLASSES = tuple(TM_E * q // 4 for q in (1, 2, 3, 4))


def _expert_kernel(te_ref, nu_ref, rows_ref, xg_ref, w1_ref, w3_ref, w2_ref, y_ref, h_ref):
    t = pl.program_id(0)
    j = pl.program_id(1)
    n_rows = rows_ref[t]

    @pl.when(j == 0)
    def _():
        h_ref[...] = xg_ref[...].astype(bf16)
        y_ref[...] = jnp.zeros_like(y_ref)

    for below, m in zip((0,) + ROW_CLASSES, ROW_CLASSES):
        @pl.when((n_rows > below) & (n_rows <= m))
        def _():
            h = h_ref[:m, :]
            gate = jax.nn.silu(_dot(h, w1_ref[...].astype(bf16))) * _dot(h, w3_ref[...].astype(bf16))
            y_ref[:m, :] += _dot(gate.astype(bf16), w2_ref[...].astype(bf16))


def _experts(tile_expert, n_used, tile_rows, xg, w1, w3, w2, layer):
    li = layer // 2
    tf = TF_EXP
    nj = NJ_EXP

    def jj(t, j, nu):
        return jnp.where(t < nu[0], j, nj - 1)

    def tt(t, nu):
        return jnp.minimum(t, jnp.maximum(nu[0] - 1, 0))

    grid_spec = pltpu.PrefetchScalarGridSpec(
        num_scalar_prefetch=3,
        grid=(NT_E, nj),
        in_specs=[pl.BlockSpec((TM_E, D), lambda t, j, te, nu, nr: (tt(t, nu), 0)),
                  pl.BlockSpec((None, None, D, tf), lambda t, j, te, nu, nr: (li, te[t], 0, jj(t, j, nu))),
                  pl.BlockSpec((None, None, D, tf), lambda t, j, te, nu, nr: (li, te[t], 0, jj(t, j, nu))),
                  pl.BlockSpec((None, None, tf, D), lambda t, j, te, nu, nr: (li, te[t], jj(t, j, nu), 0))],
        out_specs=pl.BlockSpec((TM_E, D), lambda t, j, te, nu, nr: (t, 0)),
        scratch_shapes=[pltpu.VMEM((TM_E, D), bf16)],
    )
    return pl.pallas_call(
        _expert_kernel,
        grid_spec=grid_spec,
        out_shape=jax.ShapeDtypeStruct((R_E, D), f32),
        compiler_params=_cp(("arbitrary", "arbitrary")),
    )(tile_expert, n_used, tile_rows, xg, w1, w3, w2)


def _combine_kernel(dest_ref, x_ref, g2_ref, meta_ref, y_ref, *rest, final):
    if final:
        fg_ref, op_ref, os_ref, buf_ref, sem = rest
    else:
        o_ref, buf_ref, sem = rest

    def issue(g, c):
        for u in range(ROW_UNROLL):
            r = g * ROW_UNROLL + u
            for k in range(2):
                _row_copy(y_ref, dest_ref[0, 0, k * TM + r], buf_ref.at[k], r, sem).start()
        return c

    lax.fori_loop(0, TM // ROW_UNROLL, issue, 0)
    for k in range(2):
        pltpu.make_async_copy(y_ref.at[pl.ds(0, TM)], buf_ref.at[k], sem).wait()
    w0 = meta_ref[:, M_W0:M_W0 + 1]
    w1 = meta_ref[:, M_W1:M_W1 + 1]
    out = x_ref[...] + g2_ref[0] * (w0 * buf_ref[0] + w1 * buf_ref[1])
    if not final:
        o_ref[...] = out
        return
    normed = (out * lax.rsqrt(jnp.mean(out * out, axis=-1, keepdims=True) + RMS_EPS)) * fg_ref[...]
    t = pl.program_id(0)

    @pl.when(t < NTP)
    def _():
        op_ref[...] = normed

    @pl.when(t >= NTP)
    def _():
        os_ref[...] = normed


def _combine(dest, x, mod, meta, y, layer, final_g=None):
    final = final_g is not None
    in_specs = [pl.BlockSpec((1, 1, 2 * TM), lambda t: (t, 0, 0), memory_space=pltpu.SMEM),
                pl.BlockSpec((TM, D), lambda t: (t, 0)),
                _mod_spec(layer, 5),
                pl.BlockSpec((TM, LANES), lambda t: (t, 0)),
                pl.BlockSpec(memory_space=pl.ANY)]
    args = [dest, x, mod, meta, y]
    if final:
        in_specs.append(pl.BlockSpec((1, D), lambda t: (0, 0)))
        args.append(final_g.reshape(1, D))
        out_specs = [pl.BlockSpec((TM, D), lambda t: (jnp.minimum(t, NTP - 1), 0)),
                     pl.BlockSpec((TM, D), lambda t: (jnp.maximum(t - NTP, 0), 0))]
        out_shape = [jax.ShapeDtypeStruct((T_P, D), f32), jax.ShapeDtypeStruct((T_S, D), f32)]
    else:
        out_specs = pl.BlockSpec((TM, D), lambda t: (t, 0))
        out_shape = jax.ShapeDtypeStruct((T, D), f32)
    return pl.pallas_call(
        functools.partial(_combine_kernel, final=final),
        grid=(NT,),
        in_specs=in_specs,
        out_specs=out_specs,
        out_shape=out_shape,
        scratch_shapes=[pltpu.VMEM((2, TM, D), f32), pltpu.SemaphoreType.DMA(())],
        compiler_params=_cp(("arbitrary",)),
    )(*args)


def _moe(x, mod, norm_g, router_w, router_b, w1, w3, w2, layer, final_g=None):
    h_rows, meta, meta_t, counts = _router(x, mod, norm_g, router_w, router_b, layer)
    cnt = counts[0, :N_EXPERTS].astype(jnp.int32)
    padded = ((cnt + TM_E - 1) // TM_E) * TM_E
    ends = jnp.cumsum(padded)
    starts = ends - padded
    experts = meta_t[M_E0:M_E1 + 1].astype(jnp.int32)
    ranks = meta_t[M_R0:M_R1 + 1].astype(jnp.int32)
    start_of = functools.reduce(lambda acc, e: jnp.where(experts == e, starts[e], acc), range(N_EXPERTS), 0)
    dest = start_of + ranks
    dest = dest.reshape(2, NT, TM).transpose(1, 0, 2).reshape(NT, 1, 2 * TM)
    n_used = (ends[-1] // TM_E).astype(jnp.int32).reshape(1)
    tile_start = jnp.minimum(jnp.arange(NT_E, dtype=jnp.int32), n_used[0] - 1) * TM_E
    tile_expert = jnp.sum((tile_start[:, None] >= ends[None, :]).astype(jnp.int32), axis=1)
    tile_expert = jnp.minimum(tile_expert, N_EXPERTS - 1).astype(jnp.int32)
    tile_ids = jnp.arange(NT_E, dtype=jnp.int32)
    tile_rows = jnp.clip((starts + cnt)[tile_expert] - tile_ids * TM_E, 0, TM_E)
    tile_rows = jnp.where(tile_ids < n_used[0], tile_rows, 0).astype(jnp.int32)
    xg = _dispatch((starts + cnt).astype(jnp.int32), ends.astype(jnp.int32), n_used, dest, h_rows)
    y = _experts(tile_expert, n_used, tile_rows, xg, w1, w3, w2, layer)
    return _combine(dest, x, mod, meta, y, layer, final_g)


def kernel(x_prompt, x_sample, c, cache_k, cache_v, c_ctx, ada_w, ada_b, norm1_g, norm2_g, w_in, conv_w, rpb,
           w_out, ffn_w1, ffn_w3, ffn_w2, fourier_w, router_w, router_b, moe_w1, moe_w3, moe_w2, final_g):
    cvecs = jnp.concatenate([c_ctx[None, :], c, jnp.zeros((MOD_ROWS - 1 - DEC_BATCH, D), f32)], axis=0)
    mod = _modulation(cvecs, ada_w, ada_b).reshape(DEPTH * MOD_ROWS * 6, 1, D)
    n1 = norm1_g.reshape(DEPTH, 1, D)
    n2 = norm2_g.reshape(DEPTH, 1, D)
    bias = _nbr_bias(rpb)
    cache_kt = jnp.swapaxes(cache_k, -1, -2)
    cache_vt = jnp.swapaxes(cache_v, -1, -2)
    x = (x_prompt.reshape(T_P, D), x_sample.reshape(T_S, D))
    caches = [jnp.zeros((BATCH, N_EVEN, NA_HEADS, HEAD_DIM, SEQ), f32) for _ in range(2)]
    for layer in range(DEPTH):
        li = layer // 2
        if layer % 2 == 0:
            proj, *caches = _inproj(x, mod, n1, w_in, layer, caches)
            yb_p = _ctx_attention(proj)
            yb_s = _nbr_attention(proj, cache_kt, cache_vt, bias, li)
            x = _mixout(proj, yb_p, yb_s, x, mod, conv_w, w_out, layer)
            x = _dense_ffn(x, mod, n2, ffn_w1, ffn_w3, ffn_w2, layer)
        else:
            x = _fourier(x, mod, n1, fourier_w, layer)
            last = layer == DEPTH - 1
            x = _moe(x, mod, n2, router_w, router_b, moe_w1, moe_w3, moe_w2, layer, final_g if last else None)
    y_prompt, y_sample = x
    new_kt, new_vt = caches
    return (y_prompt.reshape(BATCH, SEQ, D), y_sample.reshape(DEC_BATCH, DEC_SEQ, D),
            jnp.swapaxes(new_kt, -1, -2), jnp.swapaxes(new_vt, -1, -2))
```

```python
import functools
from typing import Any, NamedTuple

import numpy as np
import jax
import jax.numpy as jnp
from jax import lax
from jax.experimental import pallas as pl
from jax.experimental.pallas import tpu as pltpu

f32 = jnp.float32
bf16 = jnp.bfloat16

D = 1024
BATCH = 16
SEQ = 256
DEPTH = 4
DEC_BATCH = 8
DEC_SEQ = 1024
PAST_LEN = 512
GRID_W = 64
CONV_CH = 512
NA_HEADS = 8
HEAD_DIM = 64
NA_WIDTH = 512
WIN_ROWS = 8
WIN_COLS = 16
IN_WIDTH = 3072
FOURIER_GROUPS = 4
FFN_DENSE = 2816
N_EXPERTS = 8
FFN_EXPERT = 3584
N_EVEN = 2
RMS_EPS = 1e-6
NEG_INF = -1e30
SCALE = HEAD_DIM ** -0.5

LANES = 128
TM = 1024
T_P = BATCH * SEQ
T_S = DEC_BATCH * DEC_SEQ
T = T_P + T_S
NT = T // TM
NTP = T_P // TM
MOD_ROWS = 16
GRID_ROWS = DEC_SEQ // GRID_W
CHUNK_ROWS = 4
N_CHUNKS = GRID_ROWS // CHUNK_ROWS
CHUNK_Q = CHUNK_ROWS * GRID_W
KWIN_ROWS = 12
KWIN = KWIN_ROWS * GRID_W
KB0 = (0, 0, 4, 8)
KW_ROWS = (8, 12, 12, 8)
CHUNKS_PER_GROUP = 4
TF_DENSE = 256
TF_EXP = 512
TM_E = 1024
N_ASSIGN = 2 * T
NT_E = N_ASSIGN // TM_E + N_EXPERTS
R_E = NT_E * TM_E
VMEM_LIMIT = 56 * 1024 * 1024


def _cp(sem, vmem=VMEM_LIMIT):
    return pltpu.CompilerParams(dimension_semantics=sem, vmem_limit_bytes=vmem)


def _mod_index(t):
    return jnp.where(t < NTP, 0, t - (NTP - 1))


def _mod_spec(layer, part):
    def index(t, *_):
        return ((layer * MOD_ROWS + _mod_index(t)) * 6 + part, 0, 0)
    return pl.BlockSpec((1, 1, D), index)


def _norm_mod(x, g, sc, sh):
    y = x * lax.rsqrt(jnp.mean(x * x, axis=-1, keepdims=True) + RMS_EPS)
    return (y * g) * (1 + sc) + sh


def _dot(a, b):
    return jnp.dot(a, b, preferred_element_type=f32)


def _dot_nt(a, b):
    return lax.dot_general(a, b, (((1,), (1,)), ((), ())), preferred_element_type=f32)


def _mod_kernel(cv_ref, w_ref, b_ref, o_ref):
    s = jax.nn.silu(cv_ref[...]).astype(bf16)
    o_ref[0] = _dot(s, w_ref[0].astype(bf16)) + b_ref[0]


def _modulation(cvecs, ada_w, ada_b):
    tn = 1536
    return pl.pallas_call(
        _mod_kernel,
        grid=(DEPTH, 6 * D // tn),
        in_specs=[pl.BlockSpec((MOD_ROWS, D), lambda l, j: (0, 0)),
                  pl.BlockSpec((1, D, tn), lambda l, j: (l, 0, j)),
                  pl.BlockSpec((1, 1, tn), lambda l, j: (l, 0, j))],
        out_specs=pl.BlockSpec((1, MOD_ROWS, tn), lambda l, j: (l, 0, j)),
        out_shape=jax.ShapeDtypeStruct((DEPTH, MOD_ROWS, 6 * D), f32),
        compiler_params=_cp(("arbitrary", "arbitrary")),
    )(cvecs, ada_w, ada_b.reshape(DEPTH, 1, 6 * D))


TN_IN = IN_WIDTH // 2


def _x_specs(x):
    if isinstance(x, tuple):
        return [pl.BlockSpec((TM, D), lambda i, *_: (jnp.minimum(i, NTP - 1), 0)),
                pl.BlockSpec((TM, D), lambda i, *_: (jnp.maximum(i - NTP, 0), 0))]
    return [pl.BlockSpec((TM, D), lambda i, *_: (i, 0))]


def _load_x(i, x_refs):
    if len(x_refs) == 2:
        return jnp.where(i < NTP, x_refs[0][...], x_refs[1][...])
    return x_refs[0][...]


def _inproj_kernel(*refs, n_x, n_alias):
    x_refs, refs = refs[:n_x], refs[n_x + n_alias:]
    g_ref, sc_ref, sh_ref, w_ref, proj_ref, kt_ref, vt_ref, h_ref = refs
    i = pl.program_id(0)
    j = pl.program_id(1)

    @pl.when(j == 0)
    def _():
        h_ref[...] = _norm_mod(_load_x(i, x_refs), g_ref[0], sc_ref[0], sh_ref[0]).astype(bf16)

    acc = _dot(h_ref[...], w_ref[...].astype(bf16))
    proj_ref[...] = acc.astype(bf16)

    @pl.when((j == 1) & (i < NTP))
    def _():
        for ref, col0 in ((kt_ref, NA_WIDTH), (vt_ref, 2 * NA_WIDTH)):
            for s in range(TM // SEQ):
                t = acc[s * SEQ:(s + 1) * SEQ, col0:col0 + NA_WIDTH].T
                for hd in range(NA_HEADS):
                    ref[s, hd] = t[hd * HEAD_DIM:(hd + 1) * HEAD_DIM, :]


def _inproj(x, mod, norm_g, w_in, layer, caches):
    li = layer // 2
    tn = TN_IN
    seqs = TM // SEQ
    xs = x if isinstance(x, tuple) else (x,)
    cache_spec = pl.BlockSpec((seqs, None, NA_HEADS, HEAD_DIM, SEQ),
                              lambda i, j: (jnp.minimum(i, NTP - 1), li, 0, 0, 0))
    cache_shape = jax.ShapeDtypeStruct((BATCH, N_EVEN, NA_HEADS, HEAD_DIM, SEQ), f32)
    aliased = tuple(caches)
    return pl.pallas_call(
        functools.partial(_inproj_kernel, n_x=len(xs), n_alias=len(aliased)),
        grid=(NT, IN_WIDTH // tn),
        in_specs=_x_specs(x) + [pl.BlockSpec(memory_space=pl.ANY)] * len(aliased) + [
            pl.BlockSpec((1, 1, D), lambda i, j: (layer, 0, 0)),
            _mod_spec(layer, 1), _mod_spec(layer, 0),
            pl.BlockSpec((None, D, tn), lambda i, j: (li, 0, j))],
        out_specs=[pl.BlockSpec((TM, tn), lambda i, j: (i, j)), cache_spec, cache_spec],
        out_shape=[jax.ShapeDtypeStruct((T, IN_WIDTH), bf16), cache_shape, cache_shape],
        scratch_shapes=[pltpu.VMEM((TM, D), bf16)],
        input_output_aliases={len(xs) + n: 1 + n for n in range(len(aliased))},
        compiler_params=_cp(("arbitrary", "arbitrary")),
    )(*xs, *aliased, norm_g, mod, mod, w_in)


HEADS_PER_STEP = LANES // HEAD_DIM


class KeyValues(NamedTuple):
    k: jax.Array
    v: jax.Array
    bias: Any = None
    feature_major: bool = False


def _pair_attention(problems):
    first = lax.broadcasted_iota(jnp.int32, (1, LANES), 1) < HEAD_DIM
    mine = (first, jnp.logical_not(first))
    first_t = lax.broadcasted_iota(jnp.int32, (LANES, 1), 0) < HEAD_DIM
    mine_t = (first_t, jnp.logical_not(first_t))
    scores = []
    for q, parts in problems:
        q = q * SCALE
        for hh in range(HEADS_PER_STEP):
            qh = jnp.where(mine[hh], q, 0)
            ss = []
            for part in parts:
                s = _dot(qh, part.k) if part.feature_major else _dot_nt(qh, part.k)
                ss.append(s if part.bias is None else s + part.bias[hh])
            scores.append(ss)
    probs = []
    for ss in scores:
        m = functools.reduce(jnp.maximum, [jnp.max(s, axis=-1, keepdims=True) for s in ss])
        probs.append([jnp.exp((s - m).astype(bf16)) for s in ss])

    def p_times_v(p, part, hh):
        if part.feature_major:
            return _dot_nt(p, jnp.where(mine_t[hh], part.v, 1))
        return _dot(p, jnp.where(mine[hh], part.v, 1))

    results = []
    for n, (_, parts) in enumerate(problems):
        outs = []
        for hh in range(HEADS_PER_STEP):
            pv = functools.reduce(jnp.add, [p_times_v(p, part, hh)
                                            for p, part in zip(probs[n * HEADS_PER_STEP + hh], parts)])
            denom = pv[:, HEAD_DIM:HEAD_DIM + 1] if hh == 0 else pv[:, 0:1]
            outs.append(pv / denom)
        results.append(jnp.where(first, outs[0], outs[1]))
    return results


def _ctx_attn_kernel(q_ref, k_ref, v_ref, o_ref):
    cols = [slice(hp * LANES, (hp + 1) * LANES) for hp in range(NA_HEADS // HEADS_PER_STEP)]
    outs = _pair_attention([(q_ref[:, sl], [KeyValues(k_ref[:, sl], v_ref[:, sl])]) for sl in cols])
    for sl, o in zip(cols, outs):
        o_ref[:, sl] = o.astype(bf16)


def _ctx_attention(proj):
    col = IN_WIDTH // NA_WIDTH - 3
    return pl.pallas_call(
        _ctx_attn_kernel,
        grid=(BATCH,),
        in_specs=[pl.BlockSpec((SEQ, NA_WIDTH), lambda b: (b, col)),
                  pl.BlockSpec((SEQ, NA_WIDTH), lambda b: (b, col + 1)),
                  pl.BlockSpec((SEQ, NA_WIDTH), lambda b: (b, col + 2))],
        out_specs=pl.BlockSpec((SEQ, NA_WIDTH), lambda b: (b, 0)),
        out_shape=jax.ShapeDtypeStruct((T_P, NA_WIDTH), bf16),
        compiler_params=_cp(("arbitrary",)),
    )(proj, proj, proj)


def _win_start(qr):
    return min(max(qr - WIN_ROWS // 2, 0), GRID_ROWS - WIN_ROWS)


N_ROFF = 2 * WIN_ROWS - 1
N_COFF = 2 * WIN_COLS - 1


def _bias_kernel(rpb_ref, o_ref):
    base = (pl.program_id(0) * NA_HEADS + pl.program_id(1)) * (N_ROFF * N_COFF)
    qc = lax.broadcasted_iota(jnp.int32, (GRID_W, GRID_W), 0)
    kc = lax.broadcasted_iota(jnp.int32, (GRID_W, GRID_W), 1)
    coff = kc - qc + (WIN_COLS - 1)
    cs = jnp.clip(qc - WIN_COLS // 2, 0, GRID_W - WIN_COLS)
    valid = (kc >= cs) & (kc < cs + WIN_COLS)
    neg = jnp.full((GRID_W, GRID_W), NEG_INF, f32)
    blocks = []
    for a in range(N_ROFF):
        t = jnp.zeros((GRID_W, GRID_W), f32)
        for b in range(N_COFF):
            t = jnp.where(coff == b, rpb_ref[base + a * N_COFF + b], t)
        blocks.append(jnp.where(valid, t, neg))
    for c in range(N_CHUNKS):
        for ql in range(CHUNK_ROWS):
            qr = c * CHUNK_ROWS + ql
            st = _win_start(qr)
            for kl in range(KWIN_ROWS):
                kr = KB0[c] + kl
                blk = blocks[kr - qr + WIN_ROWS - 1] if st <= kr < st + WIN_ROWS else neg
                o_ref[c, ql * GRID_W:(ql + 1) * GRID_W, kl * GRID_W:(kl + 1) * GRID_W] = blk


def _nbr_bias(rpb):
    return pl.pallas_call(
        _bias_kernel,
        grid=(N_EVEN, NA_HEADS),
        in_specs=[pl.BlockSpec(memory_space=pltpu.SMEM)],
        out_specs=pl.BlockSpec((None, None, N_CHUNKS, CHUNK_Q, KWIN), lambda i, h: (i, h, 0, 0, 0)),
        out_shape=jax.ShapeDtypeStruct((N_EVEN, NA_HEADS, N_CHUNKS, CHUNK_Q, KWIN), f32),
        compiler_params=_cp(("arbitrary", "arbitrary")),
    )(rpb.reshape(-1))


def _nbr_attn_kernel(q_ref, k_ref, v_ref, kc_ref, vc_ref, bias_ref, o_ref):
    ctx = KeyValues(kc_ref[...].reshape(LANES, PAST_LEN).astype(bf16),
                    vc_ref[...].reshape(LANES, PAST_LEN).astype(bf16), feature_major=True)
    for c0 in range(0, N_CHUNKS, CHUNKS_PER_GROUP):
        problems = []
        for c in range(c0, c0 + CHUNKS_PER_GROUP):
            rows = slice(c * CHUNK_Q, (c + 1) * CHUNK_Q)
            n_keys = KW_ROWS[c] * GRID_W
            win = slice(KB0[c] * GRID_W, KB0[c] * GRID_W + n_keys)
            bias = [bias_ref[hh, c, :, :n_keys] for hh in range(HEADS_PER_STEP)]
            problems.append((q_ref[rows, :], [KeyValues(k_ref[win, :], v_ref[win, :], bias), ctx]))
        for c, o in zip(range(c0, c0 + CHUNKS_PER_GROUP), _pair_attention(problems)):
            o_ref[c * CHUNK_Q:(c + 1) * CHUNK_Q, :] = o.astype(bf16)


def _nbr_attention(proj, cache_k, cache_v, bias, li):
    qcol = 3 * CONV_CH // LANES
    ncol = NA_WIDTH // LANES
    hp_steps = NA_HEADS // HEADS_PER_STEP

    def col_spec(which):
        return pl.BlockSpec((DEC_SEQ, LANES), lambda hp, b: (NTP + b, qcol + which * ncol + hp))

    ctx_spec = pl.BlockSpec((None, None, HEADS_PER_STEP, HEAD_DIM, PAST_LEN),
                            lambda hp, b: (b, li, hp, 0, 0))
    return pl.pallas_call(
        _nbr_attn_kernel,
        grid=(hp_steps, DEC_BATCH),
        in_specs=[col_spec(0), col_spec(1), col_spec(2), ctx_spec, ctx_spec,
                  pl.BlockSpec((None, HEADS_PER_STEP, N_CHUNKS, CHUNK_Q, KWIN),
                               lambda hp, b: (li, hp, 0, 0, 0))],
        out_specs=pl.BlockSpec((DEC_SEQ, LANES), lambda hp, b: (b, hp)),
        out_shape=jax.ShapeDtypeStruct((T_S, NA_WIDTH), bf16),
        compiler_params=_cp(("arbitrary", "arbitrary")),
    )(proj, proj, proj, cache_k, cache_v, bias)


def _mixout_kernel(*refs, n_x):
    x_refs, (a_ref, ybp_ref, ybs_ref, g1_ref, cw_ref, w_ref, o_ref) = refs[:n_x], refs[n_x:]
    t = pl.program_id(0)
    a_b = a_ref[:, 0:CONV_CH].astype(f32)
    a_c = a_ref[:, CONV_CH:2 * CONV_CH].astype(f32)
    a_x = a_ref[:, 2 * CONV_CH:3 * CONV_CH].astype(f32)
    u = a_c * a_x
    r = lax.broadcasted_iota(jnp.int32, (TM, 1), 0)
    pos = jnp.where(t < NTP, r % SEQ, r)
    last = jnp.where(t < NTP, SEQ - 1, DEC_SEQ - 1)
    u_prev = jnp.where(pos == 0, 0.0, pltpu.roll(u, 1, axis=0))
    u_next = jnp.where(pos == last, 0.0, pltpu.roll(u, TM - 1, axis=0))
    y_a = a_b * (u_prev * cw_ref[0:1, :] + u * cw_ref[1:2, :] + u_next * cw_ref[2:3, :])
    y_b = jnp.where(t < NTP, ybp_ref[...], ybs_ref[...])
    y = (_dot(y_a.astype(bf16), w_ref[0:CONV_CH, :].astype(bf16))
         + _dot(y_b, w_ref[CONV_CH:, :].astype(bf16)))
    o_ref[...] = _load_x(t, x_refs) + g1_ref[0] * y


def _mixout(proj, yb_p, yb_s, x, mod, conv_w, w_out, layer):
    li = layer // 2
    xs = x if isinstance(x, tuple) else (x,)
    return pl.pallas_call(
        functools.partial(_mixout_kernel, n_x=len(xs)),
        grid=(NT,),
        in_specs=_x_specs(x) + [
            pl.BlockSpec((TM, 3 * CONV_CH), lambda t: (t, 0)),
            pl.BlockSpec((TM, NA_WIDTH), lambda t: (jnp.minimum(t, NTP - 1), 0)),
            pl.BlockSpec((TM, NA_WIDTH), lambda t: (jnp.maximum(t - NTP, 0), 0)),
            _mod_spec(layer, 2),
            pl.BlockSpec((None, 3, CONV_CH), lambda t: (li, 0, 0)),
            pl.BlockSpec((None, D, D), lambda t: (li, 0, 0))],
        out_specs=pl.BlockSpec((TM, D), lambda t: (t, 0)),
        out_shape=jax.ShapeDtypeStruct((T, D), f32),
        compiler_params=_cp(("arbitrary",)),
    )(*xs, proj, yb_p, yb_s, mod, conv_w, w_out)


FFN_TILES = 2


def _ffn_kernel(x_ref, g_ref, *refs):
    mods, (w1_ref, w3_ref, w2_ref, o_ref, h_ref) = refs[:3 * FFN_TILES], refs[3 * FFN_TILES:]
    j = pl.program_id(1)
    parts = [(slice(n * TM, (n + 1) * TM), mods[3 * n:3 * n + 3]) for n in range(FFN_TILES)]

    @pl.when(j == 0)
    def _():
        for rows, (sc_ref, sh_ref, _) in parts:
            h_ref[rows, :] = _norm_mod(x_ref[rows, :], g_ref[0], sc_ref[0], sh_ref[0]).astype(bf16)
        o_ref[...] = jnp.zeros_like(o_ref)

    h = h_ref[...]
    gate = jax.nn.silu(_dot(h, w1_ref[...].astype(bf16))) * _dot(h, w3_ref[...].astype(bf16))
    o_ref[...] += _dot(gate.astype(bf16), w2_ref[...].astype(bf16))

    @pl.when(j == pl.num_programs(1) - 1)
    def _():
        for rows, (_, _, g2_ref) in parts:
            o_ref[rows, :] = x_ref[rows, :] + g2_ref[0] * o_ref[rows, :]


def _dense_ffn(x, mod, norm_g, w1, w3, w2, layer):
    li = layer // 2
    tf = TF_DENSE
    rows = FFN_TILES * TM

    def mod_spec(part, n):
        return pl.BlockSpec((1, 1, D), lambda i, j: (
            (layer * MOD_ROWS + _mod_index(i * FFN_TILES + n)) * 6 + part, 0, 0))

    mod_specs = [mod_spec(part, n) for n in range(FFN_TILES) for part in (4, 3, 5)]
    return pl.pallas_call(
        _ffn_kernel,
        grid=(T // rows, FFN_DENSE // tf),
        in_specs=[pl.BlockSpec((rows, D), lambda i, j: (i, 0)),
                  pl.BlockSpec((1, 1, D), lambda i, j: (layer, 0, 0))] + mod_specs + [
                  pl.BlockSpec((None, D, tf), lambda i, j: (li, 0, j)),
                  pl.BlockSpec((None, D, tf), lambda i, j: (li, 0, j)),
                  pl.BlockSpec((None, tf, D), lambda i, j: (li, j, 0))],
        out_specs=pl.BlockSpec((rows, D), lambda i, j: (i, 0)),
        out_shape=jax.ShapeDtypeStruct((T, D), f32),
        scratch_shapes=[pltpu.VMEM((rows, D), bf16)],
        compiler_params=_cp(("arbitrary", "arbitrary")),
    )(x, norm_g, *([mod] * len(mod_specs)), w1, w3, w2)


GROUP_CH = D // FOURIER_GROUPS


def _dft_mats(n):
    k = np.arange(n, dtype=np.int64)
    ang = 2.0 * np.pi * ((k[:, None] * k[None, :]) % n).astype(np.float64) / n
    return np.cos(ang).astype(np.float32), np.sin(ang).astype(np.float32)


def _fourier_kernel(x_ref, g_ref, sc_ref, sh_ref, g1_ref, cs_ref, ss_ref, cl_ref, sl_ref, wf_ref,
                    o_ref, f_ref):
    t = pl.program_id(0)
    h = _norm_mod(x_ref[...], g_ref[0], sc_ref[0], sh_ref[0]).astype(bf16)
    cs = cs_ref[...].astype(bf16)
    ss = ss_ref[...].astype(bf16)
    ys, zs = [], []
    for g in range(FOURIER_GROUPS):
        hg = h[:, g * GROUP_CH:(g + 1) * GROUP_CH]
        ys.append(_dot(hg, cs))
        zs.append(_dot(hg, ss))
    y = jnp.concatenate(ys, axis=-1).astype(bf16)
    z = jnp.concatenate(zs, axis=-1).astype(bf16)

    @pl.when(t < NTP)
    def _():
        for s in range(TM // SEQ):
            rows = slice(s * SEQ, (s + 1) * SEQ)
            f = _dot(cs, y[rows]) - _dot(ss, z[rows])
            f_ref[rows, :] = (f * ((SEQ * GROUP_CH) ** -0.5)).astype(bf16)

    @pl.when(t >= NTP)
    def _():
        f = _dot(cl_ref[...].astype(bf16), y) - _dot(sl_ref[...].astype(bf16), z)
        f_ref[...] = (f * ((DEC_SEQ * GROUP_CH) ** -0.5)).astype(bf16)

    o_ref[...] = x_ref[...] + g1_ref[0] * _dot(f_ref[...], wf_ref[...].astype(bf16))


def _fourier(x, mod, norm_g, fourier_w, layer):
    li = layer // 2
    c_s, s_s = _dft_mats(SEQ)
    c_l, s_l = _dft_mats(DEC_SEQ)
    const = lambda shape: pl.BlockSpec(shape, lambda t: (0, 0))
    return pl.pallas_call(
        _fourier_kernel,
        grid=(NT,),
        in_specs=[pl.BlockSpec((TM, D), lambda t: (t, 0)),
                  pl.BlockSpec((1, 1, D), lambda t: (layer, 0, 0)),
                  _mod_spec(layer, 1), _mod_spec(layer, 0), _mod_spec(layer, 2),
                  const((SEQ, SEQ)), const((SEQ, SEQ)),
                  const((DEC_SEQ, DEC_SEQ)), const((DEC_SEQ, DEC_SEQ)),
                  pl.BlockSpec((None, D, D), lambda t: (li, 0, 0))],
        out_specs=pl.BlockSpec((TM, D), lambda t: (t, 0)),
        out_shape=jax.ShapeDtypeStruct((T, D), f32),
        scratch_shapes=[pltpu.VMEM((TM, D), bf16)],
        compiler_params=_cp(("arbitrary",)),
    )(x, norm_g, mod, mod, mod, jnp.asarray(c_s), jnp.asarray(s_s), jnp.asarray(c_l), jnp.asarray(s_l),
      fourier_w)


M_E0, M_E1, M_R0, M_R1, M_W0, M_W1 = range(6)
ROW_UNROLL = 8


META_ROWS = 8


def _router_kernel(x_ref, g_ref, sc_ref, sh_ref, rw_ref, rb_ref, h_ref, meta_ref, meta_t_ref, cnt_ref,
                   carry_ref):
    t = pl.program_id(0)

    @pl.when(t == 0)
    def _():
        carry_ref[...] = jnp.zeros_like(carry_ref)

    h = _norm_mod(x_ref[...], g_ref[0], sc_ref[0], sh_ref[0])
    h_ref[...] = h
    h_hi = h.astype(bf16)
    h_lo = (h - h_hi.astype(f32)).astype(bf16)
    hi_terms = _dot(h_hi, rw_ref[...])
    lo_term = _dot(h_lo, rw_ref[:, :LANES])
    logits = ((hi_terms[:, :LANES] + lo_term) + hi_terms[:, LANES:]) + rb_ref[...]
    lane = lax.broadcasted_iota(jnp.int32, (TM, LANES), 1)
    m1 = jnp.max(logits, axis=-1, keepdims=True)
    i1 = jnp.min(jnp.where(logits == m1, lane, LANES), axis=-1, keepdims=True)
    rest = jnp.where(lane == i1, -jnp.inf, logits)
    m2 = jnp.max(rest, axis=-1, keepdims=True)
    i2 = jnp.min(jnp.where(rest == m2, lane, LANES), axis=-1, keepdims=True)
    e = jnp.exp(m2 - m1)
    w0 = 1.0 / (1.0 + e)
    w1 = e / (1.0 + e)
    oh0 = (lane == i1).astype(f32)
    oh1 = (lane == i2).astype(f32)
    oh = oh0 + oh1
    row = lax.broadcasted_iota(jnp.int32, (TM, TM), 0)
    col = lax.broadcasted_iota(jnp.int32, (TM, TM), 1)
    before = jnp.where(col < row, 1.0, 0.0).astype(bf16)
    base = carry_ref[...] + _dot(before, oh.astype(bf16))
    r0 = jnp.sum(oh0 * base, axis=-1, keepdims=True)
    r1 = jnp.sum(oh1 * base, axis=-1, keepdims=True)
    carry_ref[...] += jnp.sum(oh, axis=0, keepdims=True)
    rec = jnp.zeros((TM, LANES), f32)
    for idx, val in ((M_E0, i1.astype(f32)), (M_E1, i2.astype(f32)), (M_R0, r0), (M_R1, r1),
                     (M_W0, w0), (M_W1, w1)):
        rec = jnp.where(lane == idx, val, rec)
    meta_ref[...] = rec
    meta_t_ref[...] = rec.T[:META_ROWS, :]
    cnt_ref[...] = carry_ref[...]


def _router(x, mod, norm_g, router_w, router_b, layer):
    li = layer // 2
    rw = jnp.pad(router_w[li], ((0, 0), (0, LANES - N_EXPERTS)))
    rw_hi = rw.astype(bf16)
    rw_lo = (rw - rw_hi.astype(f32)).astype(bf16)
    rw = jnp.concatenate([rw_hi, rw_lo], axis=1)
    rb = jnp.pad(router_b[li], (0, LANES - N_EXPERTS), constant_values=NEG_INF).reshape(1, LANES)
    return pl.pallas_call(
        _router_kernel,
        grid=(NT,),
        in_specs=[pl.BlockSpec((TM, D), lambda t: (t, 0)),
                  pl.BlockSpec((1, 1, D), lambda t: (layer, 0, 0)),
                  _mod_spec(layer, 4), _mod_spec(layer, 3),
                  pl.BlockSpec((D, 2 * LANES), lambda t: (0, 0)),
                  pl.BlockSpec((1, LANES), lambda t: (0, 0))],
        out_specs=[pl.BlockSpec((TM, D), lambda t: (t, 0)),
                   pl.BlockSpec((TM, LANES), lambda t: (t, 0)),
                   pl.BlockSpec((META_ROWS, TM), lambda t: (0, t)),
                   pl.BlockSpec((1, LANES), lambda t: (0, 0))],
        out_shape=[jax.ShapeDtypeStruct((T, D), f32),
                   jax.ShapeDtypeStruct((T, LANES), f32),
                   jax.ShapeDtypeStruct((META_ROWS, T), f32),
                   jax.ShapeDtypeStruct((1, LANES), f32)],
        scratch_shapes=[pltpu.VMEM((1, LANES), f32)],
        compiler_params=_cp(("arbitrary",)),
    )(x, norm_g, mod, mod, rw, rb)


def _row_copy(src_ref, src_row, dst_ref, dst_row, sem):
    return pltpu.make_async_copy(src_ref.at[pl.ds(src_row, 1)], dst_ref.at[pl.ds(dst_row, 1)], sem)


ZERO_BLOCK = 8


def _dispatch_kernel(lo_ref, hi_ref, nu_ref, dest_ref, h_ref, xg_ref, zero_ref, sem, zsem):
    @pl.when(pl.program_id(0) == 0)
    def _():
        zero_ref[...] = jnp.zeros_like(zero_ref)

        def row_zero(r):
            return _row_copy(zero_ref, 0, xg_ref, r, zsem)

        def block_zero(b):
            return pltpu.make_async_copy(zero_ref.at[pl.ds(0, ZERO_BLOCK)],
                                         xg_ref.at[pl.ds(pl.multiple_of(b * ZERO_BLOCK, ZERO_BLOCK), ZERO_BLOCK)],
                                         zsem)

        def tile_zero(tile):
            return pltpu.make_async_copy(zero_ref, xg_ref.at[pl.ds(pl.multiple_of(tile * TM_E, TM_E), TM_E)],
                                         zsem)

        def each(lo, hi, copy, wait):
            def step(i, c):
                if wait:
                    copy(i).wait()
                else:
                    copy(i).start()
                return c
            lax.fori_loop(lo, hi, step, 0)

        for wait in (False, True):
            for e in range(N_EXPERTS):
                lo, hi = lo_ref[e], hi_ref[e]
                aligned = jnp.minimum((lo + ZERO_BLOCK - 1) // ZERO_BLOCK * ZERO_BLOCK, hi)
                each(lo, aligned, row_zero, wait)
                each(aligned // ZERO_BLOCK, hi // ZERO_BLOCK, block_zero, wait)
            each(nu_ref[0], NT_E, tile_zero, wait)

    def issue(g, c):
        for u in range(ROW_UNROLL):
            r = g * ROW_UNROLL + u
            for k in range(2):
                _row_copy(h_ref, r, xg_ref, dest_ref[0, 0, k * TM + r], sem).start()
        return c

    lax.fori_loop(0, TM // ROW_UNROLL, issue, 0)
    for k in range(2):
        pltpu.make_async_copy(h_ref, xg_ref.at[pl.ds(0, TM)], sem).wait()


def _dispatch(pad_lo, pad_hi, n_used, dest, h):
    grid_spec = pltpu.PrefetchScalarGridSpec(
        num_scalar_prefetch=3,
        grid=(NT,),
        in_specs=[pl.BlockSpec((1, 1, 2 * TM), lambda t, *_: (t, 0, 0), memory_space=pltpu.SMEM),
                  pl.BlockSpec((TM, D), lambda t, *_: (t, 0))],
        out_specs=pl.BlockSpec(memory_space=pl.ANY),
        scratch_shapes=[pltpu.VMEM((TM_E, D), f32), pltpu.SemaphoreType.DMA(()), pltpu.SemaphoreType.DMA(())],
    )
    return pl.pallas_call(
        _dispatch_kernel,
        grid_spec=grid_spec,
        out_shape=jax.ShapeDtypeStruct((R_E, D), f32),
        compiler_params=_cp(("arbitrary",)),
    )(pad_lo, pad_hi, n_used, dest, h)


NJ_EXP = FFN_EXPERT // TF_EXP
ROW_CLASSES = tuple(TM_E * q // 4 for q in (1, 2, 3, 4))


def _expert_kernel(te_ref, nu_ref, rows_ref, xg_ref, w1_ref, w3_ref, w2_ref, y_ref, h_ref):
    t = pl.program_id(0)
    j = pl.program_id(1)
    n_rows = rows_ref[t]

    @pl.when(j == 0)
    def _():
        h_ref[...] = xg_ref[...].astype(bf16)
        y_ref[...] = jnp.zeros_like(y_ref)

    for below, m in zip((0,) + ROW_CLASSES, ROW_CLASSES):
        @pl.when((n_rows > below) & (n_rows <= m))
        def _():
            h = h_ref[:m, :]
            gate = jax.nn.silu(_dot(h, w1_ref[...].astype(bf16))) * _dot(h, w3_ref[...].astype(bf16))
            y_ref[:m, :] += _dot(gate.astype(bf16), w2_ref[...].astype(bf16))


def _experts(tile_expert, n_used, tile_rows, xg, w1, w3, w2, layer):
    li = layer // 2
    tf = TF_EXP
    nj = NJ_EXP

    def jj(t, j, nu):
        return jnp.where(t < nu[0], j, nj - 1)

    def tt(t, nu):
        return jnp.minimum(t, jnp.maximum(nu[0] - 1, 0))

    grid_spec = pltpu.PrefetchScalarGridSpec(
        num_scalar_prefetch=3,
        grid=(NT_E, nj),
        in_specs=[pl.BlockSpec((TM_E, D), lambda t, j, te, nu, nr: (tt(t, nu), 0)),
                  pl.BlockSpec((None, None, D, tf), lambda t, j, te, nu, nr: (li, te[t], 0, jj(t, j, nu))),
                  pl.BlockSpec((None, None, D, tf), lambda t, j, te, nu, nr: (li, te[t], 0, jj(t, j, nu))),
                  pl.BlockSpec((None, None, tf, D), lambda t, j, te, nu, nr: (li, te[t], jj(t, j, nu), 0))],
        out_specs=pl.BlockSpec((TM_E, D), lambda t, j, te, nu, nr: (t, 0)),
        scratch_shapes=[pltpu.VMEM((TM_E, D), bf16)],
    )
    return pl.pallas_call(
        _expert_kernel,
        grid_spec=grid_spec,
        out_shape=jax.ShapeDtypeStruct((R_E, D), f32),
        compiler_params=_cp(("arbitrary", "arbitrary")),
    )(tile_expert, n_used, tile_rows, xg, w1, w3, w2)


def _combine_kernel(dest_ref, dest_next_ref, x_ref, g2_ref, meta_ref, y_ref, *rest, final):
    if final:
        fg_ref, op_ref, os_ref, buf_ref, sems = rest
    else:
        o_ref, buf_ref, sems = rest
    t = pl.program_id(0)
    slot = t % 2

    def gather(rows_ref, s):
        def issue(g, c):
            for u in range(ROW_UNROLL):
                r = g * ROW_UNROLL + u
                for k in range(2):
                    _row_copy(y_ref, rows_ref[0, 0, k * TM + r], buf_ref.at[s, k], r, sems.at[s]).start()
            return c
        lax.fori_loop(0, TM // ROW_UNROLL, issue, 0)

    @pl.when(t == 0)
    def _():
        gather(dest_ref, 0)

    @pl.when(t + 1 < NT)
    def _():
        gather(dest_next_ref, 1 - slot)

    for k in range(2):
        pltpu.make_async_copy(y_ref.at[pl.ds(0, TM)], buf_ref.at[slot, k], sems.at[slot]).wait()
    w0 = meta_ref[:, M_W0:M_W0 + 1]
    w1 = meta_ref[:, M_W1:M_W1 + 1]
    out = x_ref[...] + g2_ref[0] * (w0 * buf_ref[slot, 0] + w1 * buf_ref[slot, 1])
    if not final:
        o_ref[...] = out
        return
    normed = (out * lax.rsqrt(jnp.mean(out * out, axis=-1, keepdims=True) + RMS_EPS)) * fg_ref[...]

    @pl.when(t < NTP)
    def _():
        op_ref[...] = normed

    @pl.when(t >= NTP)
    def _():
        os_ref[...] = normed


def _combine(dest, x, mod, meta, y, layer, final_g=None):
    final = final_g is not None
    in_specs = [pl.BlockSpec((1, 1, 2 * TM), lambda t: (t, 0, 0), memory_space=pltpu.SMEM),
                pl.BlockSpec((1, 1, 2 * TM), lambda t: (jnp.minimum(t + 1, NT - 1), 0, 0),
                             memory_space=pltpu.SMEM),
                pl.BlockSpec((TM, D), lambda t: (t, 0)),
                _mod_spec(layer, 5),
                pl.BlockSpec((TM, LANES), lambda t: (t, 0)),
                pl.BlockSpec(memory_space=pl.ANY)]
    args = [dest, dest, x, mod, meta, y]
    if final:
        in_specs.append(pl.BlockSpec((1, D), lambda t: (0, 0)))
        args.append(final_g.reshape(1, D))
        out_specs = [pl.BlockSpec((TM, D), lambda t: (jnp.minimum(t, NTP - 1), 0)),
                     pl.BlockSpec((TM, D), lambda t: (jnp.maximum(t - NTP, 0), 0))]
        out_shape = [jax.ShapeDtypeStruct((T_P, D), f32), jax.ShapeDtypeStruct((T_S, D), f32)]
    else:
        out_specs = pl.BlockSpec((TM, D), lambda t: (t, 0))
        out_shape = jax.ShapeDtypeStruct((T, D), f32)
    return pl.pallas_call(
        functools.partial(_combine_kernel, final=final),
        grid=(NT,),
        in_specs=in_specs,
        out_specs=out_specs,
        out_shape=out_shape,
        scratch_shapes=[pltpu.VMEM((2, 2, TM, D), f32), pltpu.SemaphoreType.DMA((2,))],
        compiler_params=_cp(("arbitrary",)),
    )(*args)


def _moe(x, mod, norm_g, router_w, router_b, w1, w3, w2, layer, final_g=None):
    h_rows, meta, meta_t, counts = _router(x, mod, norm_g, router_w, router_b, layer)
    cnt = counts[0, :N_EXPERTS].astype(jnp.int32)
    padded = ((cnt + TM_E - 1) // TM_E) * TM_E
    ends = jnp.cumsum(padded)
    starts = ends - padded
    experts = meta_t[M_E0:M_E1 + 1].astype(jnp.int32)
    ranks = meta_t[M_R0:M_R1 + 1].astype(jnp.int32)
    start_of = functools.reduce(lambda acc, e: jnp.where(experts == e, starts[e], acc), range(N_EXPERTS), 0)
    dest = start_of + ranks
    dest = dest.reshape(2, NT, TM).transpose(1, 0, 2).reshape(NT, 1, 2 * TM)
    n_used = (ends[-1] // TM_E).astype(jnp.int32).reshape(1)
    tile_start = jnp.minimum(jnp.arange(NT_E, dtype=jnp.int32), n_used[0] - 1) * TM_E
    tile_expert = jnp.sum((tile_start[:, None] >= ends[None, :]).astype(jnp.int32), axis=1)
    tile_expert = jnp.minimum(tile_expert, N_EXPERTS - 1).astype(jnp.int32)
    tile_ids = jnp.arange(NT_E, dtype=jnp.int32)
    tile_rows = jnp.clip((starts + cnt)[tile_expert] - tile_ids * TM_E, 0, TM_E)
    tile_rows = jnp.where(tile_ids < n_used[0], tile_rows, 0).astype(jnp.int32)
    xg = _dispatch((starts + cnt).astype(jnp.int32), ends.astype(jnp.int32), n_used, dest, h_rows)
    y = _experts(tile_expert, n_used, tile_rows, xg, w1, w3, w2, layer)
    return _combine(dest, x, mod, meta, y, layer, final_g)


def kernel(x_prompt, x_sample, c, cache_k, cache_v, c_ctx, ada_w, ada_b, norm1_g, norm2_g, w_in, conv_w, rpb,
           w_out, ffn_w1, ffn_w3, ffn_w2, fourier_w, router_w, router_b, moe_w1, moe_w3, moe_w2, final_g):
    cvecs = jnp.concatenate([c_ctx[None, :], c, jnp.zeros((MOD_ROWS - 1 - DEC_BATCH, D), f32)], axis=0)
    mod = _modulation(cvecs, ada_w, ada_b).reshape(DEPTH * MOD_ROWS * 6, 1, D)
    n1 = norm1_g.reshape(DEPTH, 1, D)
    n2 = norm2_g.reshape(DEPTH, 1, D)
    bias = _nbr_bias(rpb)
    cache_kt = jnp.swapaxes(cache_k, -1, -2)
    cache_vt = jnp.swapaxes(cache_v, -1, -2)
    x = (x_prompt.reshape(T_P, D), x_sample.reshape(T_S, D))
    caches = [jnp.zeros((BATCH, N_EVEN, NA_HEADS, HEAD_DIM, SEQ), f32) for _ in range(2)]
    for layer in range(DEPTH):
        li = layer // 2
        if layer % 2 == 0:
            proj, *caches = _inproj(x, mod, n1, w_in, layer, caches)
            yb_p = _ctx_attention(proj)
            yb_s = _nbr_attention(proj, cache_kt, cache_vt, bias, li)
            x = _mixout(proj, yb_p, yb_s, x, mod, conv_w, w_out, layer)
            x = _dense_ffn(x, mod, n2, ffn_w1, ffn_w3, ffn_w2, layer)
        else:
            x = _fourier(x, mod, n1, fourier_w, layer)
            last = layer == DEPTH - 1
            x = _moe(x, mod, n2, router_w, router_b, moe_w1, moe_w3, moe_w2, layer, final_g if last else None)
    y_prompt, y_sample = x
    new_kt, new_vt = caches
    return (y_prompt.reshape(BATCH, SEQ, D), y_sample.reshape(DEC_BATCH, DEC_SEQ, D),
            jnp.swapaxes(new_kt, -1, -2), jnp.swapaxes(new_vt, -1, -2))
```

```python
import functools
from typing import Any, NamedTuple

import numpy as np
import jax
import jax.numpy as jnp
from jax import lax
from jax.experimental import pallas as pl
from jax.experimental.pallas import tpu as pltpu

f32 = jnp.float32
bf16 = jnp.bfloat16

D = 1024
BATCH = 16
SEQ = 256
DEPTH = 4
DEC_BATCH = 8
DEC_SEQ = 1024
PAST_LEN = 512
GRID_W = 64
CONV_CH = 512
NA_HEADS = 8
HEAD_DIM = 64
NA_WIDTH = 512
WIN_ROWS = 8
WIN_COLS = 16
IN_WIDTH = 3072
FOURIER_GROUPS = 4
FFN_DENSE = 2816
N_EXPERTS = 8
FFN_EXPERT = 3584
N_EVEN = 2
RMS_EPS = 1e-6
NEG_INF = -1e30
SCALE = HEAD_DIM ** -0.5

LANES = 128
TM = 1024
T_P = BATCH * SEQ
T_S = DEC_BATCH * DEC_SEQ
T = T_P + T_S
NT = T // TM
NTP = T_P // TM
MOD_ROWS = 16
GRID_ROWS = DEC_SEQ // GRID_W
CHUNK_ROWS = 4
N_CHUNKS = GRID_ROWS // CHUNK_ROWS
CHUNK_Q = CHUNK_ROWS * GRID_W
KWIN_ROWS = 12
KWIN = KWIN_ROWS * GRID_W
KB0 = (0, 0, 4, 8)
KW_ROWS = (8, 12, 12, 8)
CHUNKS_PER_GROUP = 4
TF_DENSE = 256
TF_EXP = 512
TM_E = 1024
N_ASSIGN = 2 * T
NT_E = N_ASSIGN // TM_E + N_EXPERTS
R_E = NT_E * TM_E
VMEM_LIMIT = 56 * 1024 * 1024


def _cp(sem, vmem=VMEM_LIMIT):
    return pltpu.CompilerParams(dimension_semantics=sem, vmem_limit_bytes=vmem)


def _mod_index(t):
    return jnp.where(t < NTP, 0, t - (NTP - 1))


def _mod_spec(layer, part):
    def index(t, *_):
        return ((layer * MOD_ROWS + _mod_index(t)) * 6 + part, 0, 0)
    return pl.BlockSpec((1, 1, D), index)


def _norm_mod(x, g, sc, sh):
    y = x * lax.rsqrt(jnp.mean(x * x, axis=-1, keepdims=True) + RMS_EPS)
    return (y * g) * (1 + sc) + sh


def _dot(a, b):
    return jnp.dot(a, b, preferred_element_type=f32)


def _dot_nt(a, b):
    return lax.dot_general(a, b, (((1,), (1,)), ((), ())), preferred_element_type=f32)


def _mod_kernel(cv_ref, w_ref, b_ref, o_ref):
    s = jax.nn.silu(cv_ref[...]).astype(bf16)
    o_ref[0] = _dot(s, w_ref[0].astype(bf16)) + b_ref[0]


def _modulation(cvecs, ada_w, ada_b):
    tn = 1536
    return pl.pallas_call(
        _mod_kernel,
        grid=(DEPTH, 6 * D // tn),
        in_specs=[pl.BlockSpec((MOD_ROWS, D), lambda l, j: (0, 0)),
                  pl.BlockSpec((1, D, tn), lambda l, j: (l, 0, j)),
                  pl.BlockSpec((1, 1, tn), lambda l, j: (l, 0, j))],
        out_specs=pl.BlockSpec((1, MOD_ROWS, tn), lambda l, j: (l, 0, j)),
        out_shape=jax.ShapeDtypeStruct((DEPTH, MOD_ROWS, 6 * D), f32),
        compiler_params=_cp(("arbitrary", "arbitrary")),
    )(cvecs, ada_w, ada_b.reshape(DEPTH, 1, 6 * D))


TN_IN = IN_WIDTH // 2


def _x_specs(x):
    if isinstance(x, tuple):
        return [pl.BlockSpec((TM, D), lambda i, *_: (jnp.minimum(i, NTP - 1), 0)),
                pl.BlockSpec((TM, D), lambda i, *_: (jnp.maximum(i - NTP, 0), 0))]
    return [pl.BlockSpec((TM, D), lambda i, *_: (i, 0))]


def _load_x(i, x_refs):
    if len(x_refs) == 2:
        return jnp.where(i < NTP, x_refs[0][...], x_refs[1][...])
    return x_refs[0][...]


def _inproj_kernel(*refs, n_x, n_alias):
    x_refs, refs = refs[:n_x], refs[n_x + n_alias:]
    g_ref, sc_ref, sh_ref, w_ref, proj_ref, kt_ref, vt_ref, h_ref = refs
    i = pl.program_id(0)
    j = pl.program_id(1)

    def project(h):
        acc = _dot(h, w_ref[...].astype(bf16))
        proj_ref[...] = acc.astype(bf16)
        return acc

    @pl.when(j == 0)
    def _():
        h = _norm_mod(_load_x(i, x_refs), g_ref[0], sc_ref[0], sh_ref[0]).astype(bf16)
        h_ref[...] = h
        project(h)

    @pl.when((j == 1) & (i >= NTP))
    def _():
        project(h_ref[...])

    @pl.when((j == 1) & (i < NTP))
    def _():
        acc = project(h_ref[...])
        for ref, col0 in ((kt_ref, NA_WIDTH), (vt_ref, 2 * NA_WIDTH)):
            for s in range(TM // SEQ):
                t = acc[s * SEQ:(s + 1) * SEQ, col0:col0 + NA_WIDTH].T
                for hd in range(NA_HEADS):
                    ref[s, hd] = t[hd * HEAD_DIM:(hd + 1) * HEAD_DIM, :]


def _inproj(x, mod, norm_g, w_in, layer, caches):
    li = layer // 2
    tn = TN_IN
    seqs = TM // SEQ
    xs = x if isinstance(x, tuple) else (x,)
    cache_spec = pl.BlockSpec((seqs, None, NA_HEADS, HEAD_DIM, SEQ),
                              lambda i, j: (jnp.minimum(i, NTP - 1), li, 0, 0, 0))
    cache_shape = jax.ShapeDtypeStruct((BATCH, N_EVEN, NA_HEADS, HEAD_DIM, SEQ), f32)
    aliased = tuple(caches)
    return pl.pallas_call(
        functools.partial(_inproj_kernel, n_x=len(xs), n_alias=len(aliased)),
        grid=(NT, IN_WIDTH // tn),
        in_specs=_x_specs(x) + [pl.BlockSpec(memory_space=pl.ANY)] * len(aliased) + [
            pl.BlockSpec((1, 1, D), lambda i, j: (layer, 0, 0)),
            _mod_spec(layer, 1), _mod_spec(layer, 0),
            pl.BlockSpec((None, D, tn), lambda i, j: (li, 0, j))],
        out_specs=[pl.BlockSpec((TM, tn), lambda i, j: (i, j)), cache_spec, cache_spec],
        out_shape=[jax.ShapeDtypeStruct((T, IN_WIDTH), bf16), cache_shape, cache_shape],
        scratch_shapes=[pltpu.VMEM((TM, D), bf16)],
        input_output_aliases={len(xs) + n: 1 + n for n in range(len(aliased))},
        compiler_params=_cp(("arbitrary", "arbitrary")),
    )(*xs, *aliased, norm_g, mod, mod, w_in)


HEADS_PER_STEP = LANES // HEAD_DIM


class KeyValues(NamedTuple):
    k: jax.Array
    v: jax.Array
    bias: Any = None
    feature_major: bool = False


def _pair_attention(problems):
    first = lax.broadcasted_iota(jnp.int32, (1, LANES), 1) < HEAD_DIM
    mine = (first, jnp.logical_not(first))
    first_t = lax.broadcasted_iota(jnp.int32, (LANES, 1), 0) < HEAD_DIM
    mine_t = (first_t, jnp.logical_not(first_t))
    scores = []
    for q, parts in problems:
        q = q * SCALE
        for hh in range(HEADS_PER_STEP):
            qh = jnp.where(mine[hh], q, 0)
            ss = []
            for part in parts:
                s = _dot(qh, part.k) if part.feature_major else _dot_nt(qh, part.k)
                ss.append(s if part.bias is None else s + part.bias[hh])
            scores.append(ss)
    probs = []
    for ss in scores:
        m = functools.reduce(jnp.maximum, [jnp.max(s, axis=-1, keepdims=True) for s in ss])
        probs.append([jnp.exp((s - m).astype(bf16)) for s in ss])

    def p_times_v(p, part, hh):
        if part.feature_major:
            return _dot_nt(p, jnp.where(mine_t[hh], part.v, 1))
        return _dot(p, jnp.where(mine[hh], part.v, 1))

    results = []
    for n, (_, parts) in enumerate(problems):
        outs = []
        for hh in range(HEADS_PER_STEP):
            pv = functools.reduce(jnp.add, [p_times_v(p, part, hh)
                                            for p, part in zip(probs[n * HEADS_PER_STEP + hh], parts)])
            denom = pv[:, HEAD_DIM:HEAD_DIM + 1] if hh == 0 else pv[:, 0:1]
            outs.append(pv / denom)
        results.append(jnp.where(first, outs[0], outs[1]))
    return results


def _ctx_attn_kernel(q_ref, k_ref, v_ref, o_ref):
    cols = [slice(hp * LANES, (hp + 1) * LANES) for hp in range(NA_HEADS // HEADS_PER_STEP)]
    outs = _pair_attention([(q_ref[:, sl], [KeyValues(k_ref[:, sl], v_ref[:, sl])]) for sl in cols])
    for sl, o in zip(cols, outs):
        o_ref[:, sl] = o.astype(bf16)


def _ctx_attention(proj):
    col = IN_WIDTH // NA_WIDTH - 3
    return pl.pallas_call(
        _ctx_attn_kernel,
        grid=(BATCH,),
        in_specs=[pl.BlockSpec((SEQ, NA_WIDTH), lambda b: (b, col)),
                  pl.BlockSpec((SEQ, NA_WIDTH), lambda b: (b, col + 1)),
                  pl.BlockSpec((SEQ, NA_WIDTH), lambda b: (b, col + 2))],
        out_specs=pl.BlockSpec((SEQ, NA_WIDTH), lambda b: (b, 0)),
        out_shape=jax.ShapeDtypeStruct((T_P, NA_WIDTH), bf16),
        compiler_params=_cp(("arbitrary",)),
    )(proj, proj, proj)


def _win_start(qr):
    return min(max(qr - WIN_ROWS // 2, 0), GRID_ROWS - WIN_ROWS)


N_ROFF = 2 * WIN_ROWS - 1
N_COFF = 2 * WIN_COLS - 1


def _bias_kernel(rpb_ref, o_ref):
    base = (pl.program_id(0) * NA_HEADS + pl.program_id(1)) * (N_ROFF * N_COFF)
    qc = lax.broadcasted_iota(jnp.int32, (GRID_W, GRID_W), 0)
    kc = lax.broadcasted_iota(jnp.int32, (GRID_W, GRID_W), 1)
    coff = kc - qc + (WIN_COLS - 1)
    cs = jnp.clip(qc - WIN_COLS // 2, 0, GRID_W - WIN_COLS)
    valid = (kc >= cs) & (kc < cs + WIN_COLS)
    neg = jnp.full((GRID_W, GRID_W), NEG_INF, f32)
    blocks = []
    for a in range(N_ROFF):
        t = jnp.zeros((GRID_W, GRID_W), f32)
        for b in range(N_COFF):
            t = jnp.where(coff == b, rpb_ref[base + a * N_COFF + b], t)
        blocks.append(jnp.where(valid, t, neg))
    for c in range(N_CHUNKS):
        for ql in range(CHUNK_ROWS):
            qr = c * CHUNK_ROWS + ql
            st = _win_start(qr)
            for kl in range(KWIN_ROWS):
                kr = KB0[c] + kl
                blk = blocks[kr - qr + WIN_ROWS - 1] if st <= kr < st + WIN_ROWS else neg
                o_ref[c, ql * GRID_W:(ql + 1) * GRID_W, kl * GRID_W:(kl + 1) * GRID_W] = blk


def _nbr_bias(rpb):
    return pl.pallas_call(
        _bias_kernel,
        grid=(N_EVEN, NA_HEADS),
        in_specs=[pl.BlockSpec(memory_space=pltpu.SMEM)],
        out_specs=pl.BlockSpec((None, None, N_CHUNKS, CHUNK_Q, KWIN), lambda i, h: (i, h, 0, 0, 0)),
        out_shape=jax.ShapeDtypeStruct((N_EVEN, NA_HEADS, N_CHUNKS, CHUNK_Q, KWIN), f32),
        compiler_params=_cp(("arbitrary", "arbitrary")),
    )(rpb.reshape(-1))


def _nbr_attn_kernel(q_ref, k_ref, v_ref, kc_ref, vc_ref, bias_ref, o_ref):
    ctx = KeyValues(kc_ref[...].reshape(LANES, PAST_LEN).astype(bf16),
                    vc_ref[...].reshape(LANES, PAST_LEN).astype(bf16), feature_major=True)
    for c0 in range(0, N_CHUNKS, CHUNKS_PER_GROUP):
        problems = []
        for c in range(c0, c0 + CHUNKS_PER_GROUP):
            rows = slice(c * CHUNK_Q, (c + 1) * CHUNK_Q)
            n_keys = KW_ROWS[c] * GRID_W
            win = slice(KB0[c] * GRID_W, KB0[c] * GRID_W + n_keys)
            bias = [bias_ref[hh, c, :, :n_keys] for hh in range(HEADS_PER_STEP)]
            problems.append((q_ref[rows, :], [KeyValues(k_ref[win, :], v_ref[win, :], bias), ctx]))
        for c, o in zip(range(c0, c0 + CHUNKS_PER_GROUP), _pair_attention(problems)):
            o_ref[c * CHUNK_Q:(c + 1) * CHUNK_Q, :] = o.astype(bf16)


def _nbr_attention(proj, cache_k, cache_v, bias, li):
    qcol = 3 * CONV_CH // LANES
    ncol = NA_WIDTH // LANES
    hp_steps = NA_HEADS // HEADS_PER_STEP

    def col_spec(which):
        return pl.BlockSpec((DEC_SEQ, LANES), lambda hp, b: (NTP + b, qcol + which * ncol + hp))

    ctx_spec = pl.BlockSpec((None, None, HEADS_PER_STEP, HEAD_DIM, PAST_LEN),
                            lambda hp, b: (b, li, hp, 0, 0))
    return pl.pallas_call(
        _nbr_attn_kernel,
        grid=(hp_steps, DEC_BATCH),
        in_specs=[col_spec(0), col_spec(1), col_spec(2), ctx_spec, ctx_spec,
                  pl.BlockSpec((None, HEADS_PER_STEP, N_CHUNKS, CHUNK_Q, KWIN),
                               lambda hp, b: (li, hp, 0, 0, 0))],
        out_specs=pl.BlockSpec((DEC_SEQ, LANES), lambda hp, b: (b, hp)),
        out_shape=jax.ShapeDtypeStruct((T_S, NA_WIDTH), bf16),
        compiler_params=_cp(("arbitrary", "arbitrary")),
    )(proj, proj, proj, cache_k, cache_v, bias)


def _mixout_kernel(*refs, n_x):
    x_refs, (a_ref, ybp_ref, ybs_ref, g1_ref, cw_ref, w_ref, o_ref) = refs[:n_x], refs[n_x:]
    t = pl.program_id(0)
    a_b = a_ref[:, 0:CONV_CH].astype(f32)
    a_c = a_ref[:, CONV_CH:2 * CONV_CH].astype(f32)
    a_x = a_ref[:, 2 * CONV_CH:3 * CONV_CH].astype(f32)
    u = a_c * a_x
    r = lax.broadcasted_iota(jnp.int32, (TM, 1), 0)
    pos = jnp.where(t < NTP, r % SEQ, r)
    last = jnp.where(t < NTP, SEQ - 1, DEC_SEQ - 1)
    u_prev = jnp.where(pos == 0, 0.0, pltpu.roll(u, 1, axis=0))
    u_next = jnp.where(pos == last, 0.0, pltpu.roll(u, TM - 1, axis=0))
    y_a = a_b * (u_prev * cw_ref[0:1, :] + u * cw_ref[1:2, :] + u_next * cw_ref[2:3, :])
    y_b = jnp.where(t < NTP, ybp_ref[...], ybs_ref[...])
    y = (_dot(y_a.astype(bf16), w_ref[0:CONV_CH, :].astype(bf16))
         + _dot(y_b, w_ref[CONV_CH:, :].astype(bf16)))
    o_ref[...] = _load_x(t, x_refs) + g1_ref[0] * y


def _mixout(proj, yb_p, yb_s, x, mod, conv_w, w_out, layer):
    li = layer // 2
    xs = x if isinstance(x, tuple) else (x,)
    return pl.pallas_call(
        functools.partial(_mixout_kernel, n_x=len(xs)),
        grid=(NT,),
        in_specs=_x_specs(x) + [
            pl.BlockSpec((TM, 3 * CONV_CH), lambda t: (t, 0)),
            pl.BlockSpec((TM, NA_WIDTH), lambda t: (jnp.minimum(t, NTP - 1), 0)),
            pl.BlockSpec((TM, NA_WIDTH), lambda t: (jnp.maximum(t - NTP, 0), 0)),
            _mod_spec(layer, 2),
            pl.BlockSpec((None, 3, CONV_CH), lambda t: (li, 0, 0)),
            pl.BlockSpec((None, D, D), lambda t: (li, 0, 0))],
        out_specs=pl.BlockSpec((TM, D), lambda t: (t, 0)),
        out_shape=jax.ShapeDtypeStruct((T, D), f32),
        compiler_params=_cp(("arbitrary",)),
    )(*xs, proj, yb_p, yb_s, mod, conv_w, w_out)


FFN_TILES = 2


def _ffn_kernel(x_ref, g_ref, *refs):
    mods, (w1_ref, w3_ref, w2_ref, o_ref, h_ref) = refs[:3 * FFN_TILES], refs[3 * FFN_TILES:]
    j = pl.program_id(1)
    last = pl.num_programs(1) - 1
    parts = [(slice(n * TM, (n + 1) * TM), mods[3 * n:3 * n + 3]) for n in range(FFN_TILES)]

    def hidden_step(h):
        gate = jax.nn.silu(_dot(h, w1_ref[...].astype(bf16))) * _dot(h, w3_ref[...].astype(bf16))
        return _dot(gate.astype(bf16), w2_ref[...].astype(bf16))

    @pl.when(j == 0)
    def _():
        h = jnp.concatenate([_norm_mod(x_ref[rows, :], g_ref[0], sc_ref[0], sh_ref[0]).astype(bf16)
                             for rows, (sc_ref, sh_ref, _) in parts], axis=0)
        h_ref[...] = h
        o_ref[...] = hidden_step(h)

    @pl.when((j > 0) & (j < last))
    def _():
        o_ref[...] += hidden_step(h_ref[...])

    @pl.when(j == last)
    def _():
        acc = o_ref[...] + hidden_step(h_ref[...])
        for rows, (_, _, g2_ref) in parts:
            o_ref[rows, :] = x_ref[rows, :] + g2_ref[0] * acc[rows, :]


def _dense_ffn(x, mod, norm_g, w1, w3, w2, layer):
    li = layer // 2
    tf = TF_DENSE
    rows = FFN_TILES * TM

    def mod_spec(part, n):
        return pl.BlockSpec((1, 1, D), lambda i, j: (
            (layer * MOD_ROWS + _mod_index(i * FFN_TILES + n)) * 6 + part, 0, 0))

    mod_specs = [mod_spec(part, n) for n in range(FFN_TILES) for part in (4, 3, 5)]
    return pl.pallas_call(
        _ffn_kernel,
        grid=(T // rows, FFN_DENSE // tf),
        in_specs=[pl.BlockSpec((rows, D), lambda i, j: (i, 0)),
                  pl.BlockSpec((1, 1, D), lambda i, j: (layer, 0, 0))] + mod_specs + [
                  pl.BlockSpec((None, D, tf), lambda i, j: (li, 0, j)),
                  pl.BlockSpec((None, D, tf), lambda i, j: (li, 0, j)),
                  pl.BlockSpec((None, tf, D), lambda i, j: (li, j, 0))],
        out_specs=pl.BlockSpec((rows, D), lambda i, j: (i, 0)),
        out_shape=jax.ShapeDtypeStruct((T, D), f32),
        scratch_shapes=[pltpu.VMEM((rows, D), bf16)],
        compiler_params=_cp(("arbitrary", "arbitrary")),
    )(x, norm_g, *([mod] * len(mod_specs)), w1, w3, w2)


GROUP_CH = D // FOURIER_GROUPS


def _dft_mats(n):
    k = np.arange(n, dtype=np.int64)
    ang = 2.0 * np.pi * ((k[:, None] * k[None, :]) % n).astype(np.float64) / n
    return np.cos(ang).astype(np.float32), np.sin(ang).astype(np.float32)


def _fourier_kernel(x_ref, g_ref, sc_ref, sh_ref, g1_ref, cs_ref, ss_ref, cl_ref, sl_ref, wf_ref,
                    o_ref, f_ref):
    t = pl.program_id(0)
    h = _norm_mod(x_ref[...], g_ref[0], sc_ref[0], sh_ref[0]).astype(bf16)
    cs = cs_ref[...].astype(bf16)
    ss = ss_ref[...].astype(bf16)
    ys, zs = [], []
    for g in range(FOURIER_GROUPS):
        hg = h[:, g * GROUP_CH:(g + 1) * GROUP_CH]
        ys.append(_dot(hg, cs))
        zs.append(_dot(hg, ss))
    y = jnp.concatenate(ys, axis=-1).astype(bf16)
    z = jnp.concatenate(zs, axis=-1).astype(bf16)

    @pl.when(t < NTP)
    def _():
        for s in range(TM // SEQ):
            rows = slice(s * SEQ, (s + 1) * SEQ)
            f = _dot(cs, y[rows]) - _dot(ss, z[rows])
            f_ref[rows, :] = (f * ((SEQ * GROUP_CH) ** -0.5)).astype(bf16)

    @pl.when(t >= NTP)
    def _():
        f = _dot(cl_ref[...].astype(bf16), y) - _dot(sl_ref[...].astype(bf16), z)
        f_ref[...] = (f * ((DEC_SEQ * GROUP_CH) ** -0.5)).astype(bf16)

    o_ref[...] = x_ref[...] + g1_ref[0] * _dot(f_ref[...], wf_ref[...].astype(bf16))


def _fourier(x, mod, norm_g, fourier_w, layer):
    li = layer // 2
    c_s, s_s = _dft_mats(SEQ)
    c_l, s_l = _dft_mats(DEC_SEQ)
    const = lambda shape: pl.BlockSpec(shape, lambda t: (0, 0))
    return pl.pallas_call(
        _fourier_kernel,
        grid=(NT,),
        in_specs=[pl.BlockSpec((TM, D), lambda t: (t, 0)),
                  pl.BlockSpec((1, 1, D), lambda t: (layer, 0, 0)),
                  _mod_spec(layer, 1), _mod_spec(layer, 0), _mod_spec(layer, 2),
                  const((SEQ, SEQ)), const((SEQ, SEQ)),
                  const((DEC_SEQ, DEC_SEQ)), const((DEC_SEQ, DEC_SEQ)),
                  pl.BlockSpec((None, D, D), lambda t: (li, 0, 0))],
        out_specs=pl.BlockSpec((TM, D), lambda t: (t, 0)),
        out_shape=jax.ShapeDtypeStruct((T, D), f32),
        scratch_shapes=[pltpu.VMEM((TM, D), bf16)],
        compiler_params=_cp(("arbitrary",)),
    )(x, norm_g, mod, mod, mod, jnp.asarray(c_s), jnp.asarray(s_s), jnp.asarray(c_l), jnp.asarray(s_l),
      fourier_w)


M_E0, M_E1, M_R0, M_R1, M_W0, M_W1 = range(6)
ROW_UNROLL = 8


META_ROWS = 8


def _router_kernel(x_ref, g_ref, sc_ref, sh_ref, rw_ref, rb_ref, h_ref, meta_ref, meta_t_ref, cnt_ref,
                   carry_ref):
    t = pl.program_id(0)

    @pl.when(t == 0)
    def _():
        carry_ref[...] = jnp.zeros_like(carry_ref)

    h = _norm_mod(x_ref[...], g_ref[0], sc_ref[0], sh_ref[0])
    h_ref[...] = h
    h_hi = h.astype(bf16)
    h_lo = (h - h_hi.astype(f32)).astype(bf16)
    hi_terms = _dot(h_hi, rw_ref[...])
    lo_term = _dot(h_lo, rw_ref[:, :LANES])
    logits = ((hi_terms[:, :LANES] + lo_term) + hi_terms[:, LANES:]) + rb_ref[...]
    lane = lax.broadcasted_iota(jnp.int32, (TM, LANES), 1)
    m1 = jnp.max(logits, axis=-1, keepdims=True)
    i1 = jnp.min(jnp.where(logits == m1, lane, LANES), axis=-1, keepdims=True)
    rest = jnp.where(lane == i1, -jnp.inf, logits)
    m2 = jnp.max(rest, axis=-1, keepdims=True)
    i2 = jnp.min(jnp.where(rest == m2, lane, LANES), axis=-1, keepdims=True)
    e = jnp.exp(m2 - m1)
    w0 = 1.0 / (1.0 + e)
    w1 = e / (1.0 + e)
    oh0 = (lane == i1).astype(f32)
    oh1 = (lane == i2).astype(f32)
    oh = oh0 + oh1
    row = lax.broadcasted_iota(jnp.int32, (TM, TM), 0)
    col = lax.broadcasted_iota(jnp.int32, (TM, TM), 1)
    before = jnp.where(col < row, 1.0, 0.0).astype(bf16)
    base = carry_ref[...] + _dot(before, oh.astype(bf16))
    r0 = jnp.sum(oh0 * base, axis=-1, keepdims=True)
    r1 = jnp.sum(oh1 * base, axis=-1, keepdims=True)
    carry_ref[...] += jnp.sum(oh, axis=0, keepdims=True)
    rec = jnp.zeros((TM, LANES), f32)
    for idx, val in ((M_E0, i1.astype(f32)), (M_E1, i2.astype(f32)), (M_R0, r0), (M_R1, r1),
                     (M_W0, w0), (M_W1, w1)):
        rec = jnp.where(lane == idx, val, rec)
    meta_ref[...] = rec
    meta_t_ref[...] = rec.T[:META_ROWS, :]
    cnt_ref[...] = carry_ref[...]


def _router(x, mod, norm_g, router_w, router_b, layer):
    li = layer // 2
    rw = jnp.pad(router_w[li], ((0, 0), (0, LANES - N_EXPERTS)))
    rw_hi = rw.astype(bf16)
    rw_lo = (rw - rw_hi.astype(f32)).astype(bf16)
    rw = jnp.concatenate([rw_hi, rw_lo], axis=1)
    rb = jnp.pad(router_b[li], (0, LANES - N_EXPERTS), constant_values=NEG_INF).reshape(1, LANES)
    return pl.pallas_call(
        _router_kernel,
        grid=(NT,),
        in_specs=[pl.BlockSpec((TM, D), lambda t: (t, 0)),
                  pl.BlockSpec((1, 1, D), lambda t: (layer, 0, 0)),
                  _mod_spec(layer, 4), _mod_spec(layer, 3),
                  pl.BlockSpec((D, 2 * LANES), lambda t: (0, 0)),
                  pl.BlockSpec((1, LANES), lambda t: (0, 0))],
        out_specs=[pl.BlockSpec((TM, D), lambda t: (t, 0)),
                   pl.BlockSpec((TM, LANES), lambda t: (t, 0)),
                   pl.BlockSpec((META_ROWS, TM), lambda t: (0, t)),
                   pl.BlockSpec((1, LANES), lambda t: (0, 0))],
        out_shape=[jax.ShapeDtypeStruct((T, D), f32),
                   jax.ShapeDtypeStruct((T, LANES), f32),
                   jax.ShapeDtypeStruct((META_ROWS, T), f32),
                   jax.ShapeDtypeStruct((1, LANES), f32)],
        scratch_shapes=[pltpu.VMEM((1, LANES), f32)],
        compiler_params=_cp(("arbitrary",)),
    )(x, norm_g, mod, mod, rw, rb)


def _row_copy(src_ref, src_row, dst_ref, dst_row, sem):
    return pltpu.make_async_copy(src_ref.at[pl.ds(src_row, 1)], dst_ref.at[pl.ds(dst_row, 1)], sem)


ZERO_BLOCK = 8


def _dispatch_kernel(lo_ref, hi_ref, nu_ref, dest_ref, h_ref, xg_ref, zero_ref, sem, zsem):
    @pl.when(pl.program_id(0) == 0)
    def _():
        zero_ref[...] = jnp.zeros_like(zero_ref)

        def row_zero(r):
            return _row_copy(zero_ref, 0, xg_ref, r, zsem)

        def block_zero(b):
            return pltpu.make_async_copy(zero_ref.at[pl.ds(0, ZERO_BLOCK)],
                                         xg_ref.at[pl.ds(pl.multiple_of(b * ZERO_BLOCK, ZERO_BLOCK), ZERO_BLOCK)],
                                         zsem)

        def tile_zero(tile):
            return pltpu.make_async_copy(zero_ref, xg_ref.at[pl.ds(pl.multiple_of(tile * TM_E, TM_E), TM_E)],
                                         zsem)

        def each(lo, hi, copy, wait):
            def step(i, c):
                if wait:
                    copy(i).wait()
                else:
                    copy(i).start()
                return c
            lax.fori_loop(lo, hi, step, 0)

        for wait in (False, True):
            for e in range(N_EXPERTS):
                lo, hi = lo_ref[e], hi_ref[e]
                aligned = jnp.minimum((lo + ZERO_BLOCK - 1) // ZERO_BLOCK * ZERO_BLOCK, hi)
                each(lo, aligned, row_zero, wait)
                each(aligned // ZERO_BLOCK, hi // ZERO_BLOCK, block_zero, wait)
            each(nu_ref[0], NT_E, tile_zero, wait)

    def issue(g, c):
        for u in range(ROW_UNROLL):
            r = g * ROW_UNROLL + u
            for k in range(2):
                _row_copy(h_ref, r, xg_ref, dest_ref[0, 0, k * TM + r], sem).start()
        return c

    lax.fori_loop(0, TM // ROW_UNROLL, issue, 0)
    for k in range(2):
        pltpu.make_async_copy(h_ref, xg_ref.at[pl.ds(0, TM)], sem).wait()


def _dispatch(pad_lo, pad_hi, n_used, dest, h):
    grid_spec = pltpu.PrefetchScalarGridSpec(
        num_scalar_prefetch=3,
        grid=(NT,),
        in_specs=[pl.BlockSpec((1, 1, 2 * TM), lambda t, *_: (t, 0, 0), memory_space=pltpu.SMEM),
                  pl.BlockSpec((TM, D), lambda t, *_: (t, 0))],
        out_specs=pl.BlockSpec(memory_space=pl.ANY),
        scratch_shapes=[pltpu.VMEM((TM_E, D), f32), pltpu.SemaphoreType.DMA(()), pltpu.SemaphoreType.DMA(())],
    )
    return pl.pallas_call(
        _dispatch_kernel,
        grid_spec=grid_spec,
        out_shape=jax.ShapeDtypeStruct((R_E, D), f32),
        compiler_params=_cp(("arbitrary",)),
    )(pad_lo, pad_hi, n_used, dest, h)


NJ_EXP = FFN_EXPERT // TF_EXP
ROW_CLASSES = tuple(TM_E * q // 4 for q in (1, 2, 3, 4))


def _expert_kernel(te_ref, nu_ref, rows_ref, xg_ref, w1_ref, w3_ref, w2_ref, y_ref, h_ref):
    t = pl.program_id(0)
    j = pl.program_id(1)
    n_rows = rows_ref[t]

    def hidden_step(h):
        gate = jax.nn.silu(_dot(h, w1_ref[...].astype(bf16))) * _dot(h, w3_ref[...].astype(bf16))
        return _dot(gate.astype(bf16), w2_ref[...].astype(bf16))

    @pl.when((j == 0) & (n_rows == 0))
    def _():
        y_ref[...] = jnp.zeros_like(y_ref)

    for below, m in zip((0,) + ROW_CLASSES, ROW_CLASSES):
        in_class = (n_rows > below) & (n_rows <= m)

        @pl.when(in_class & (j == 0))
        def _():
            h = xg_ref[:m, :].astype(bf16)
            h_ref[:m, :] = h
            y_ref[:m, :] = hidden_step(h)
            if m < TM_E:
                y_ref[m:, :] = jnp.zeros((TM_E - m, D), f32)

        @pl.when(in_class & (j > 0))
        def _():
            y_ref[:m, :] += hidden_step(h_ref[:m, :])


def _experts(tile_expert, n_used, tile_rows, xg, w1, w3, w2, layer):
    li = layer // 2
    tf = TF_EXP
    nj = NJ_EXP

    def jj(t, j, nu):
        return jnp.where(t < nu[0], j, nj - 1)

    def tt(t, nu):
        return jnp.minimum(t, jnp.maximum(nu[0] - 1, 0))

    grid_spec = pltpu.PrefetchScalarGridSpec(
        num_scalar_prefetch=3,
        grid=(NT_E, nj),
        in_specs=[pl.BlockSpec((TM_E, D), lambda t, j, te, nu, nr: (tt(t, nu), 0)),
                  pl.BlockSpec((None, None, D, tf), lambda t, j, te, nu, nr: (li, te[t], 0, jj(t, j, nu))),
                  pl.BlockSpec((None, None, D, tf), lambda t, j, te, nu, nr: (li, te[t], 0, jj(t, j, nu))),
                  pl.BlockSpec((None, None, tf, D), lambda t, j, te, nu, nr: (li, te[t], jj(t, j, nu), 0))],
        out_specs=pl.BlockSpec((TM_E, D), lambda t, j, te, nu, nr: (t, 0)),
        scratch_shapes=[pltpu.VMEM((TM_E, D), bf16)],
    )
    return pl.pallas_call(
        _expert_kernel,
        grid_spec=grid_spec,
        out_shape=jax.ShapeDtypeStruct((R_E, D), f32),
        compiler_params=_cp(("arbitrary", "arbitrary")),
    )(tile_expert, n_used, tile_rows, xg, w1, w3, w2)


def _combine_kernel(dest_ref, dest_next_ref, x_ref, g2_ref, meta_ref, y_ref, *rest, final):
    if final:
        fg_ref, op_ref, os_ref, buf_ref, sems = rest
    else:
        o_ref, buf_ref, sems = rest
    t = pl.program_id(0)
    slot = t % 2

    def gather(rows_ref, s):
        def issue(g, c):
            for u in range(ROW_UNROLL):
                r = g * ROW_UNROLL + u
                for k in range(2):
                    _row_copy(y_ref, rows_ref[0, 0, k * TM + r], buf_ref.at[s, k], r, sems.at[s]).start()
            return c
        lax.fori_loop(0, TM // ROW_UNROLL, issue, 0)

    @pl.when(t == 0)
    def _():
        gather(dest_ref, 0)

    @pl.when(t + 1 < NT)
    def _():
        gather(dest_next_ref, 1 - slot)

    for k in range(2):
        pltpu.make_async_copy(y_ref.at[pl.ds(0, TM)], buf_ref.at[slot, k], sems.at[slot]).wait()
    w0 = meta_ref[:, M_W0:M_W0 + 1]
    w1 = meta_ref[:, M_W1:M_W1 + 1]
    out = x_ref[...] + g2_ref[0] * (w0 * buf_ref[slot, 0] + w1 * buf_ref[slot, 1])
    if not final:
        o_ref[...] = out
        return
    normed = (out * lax.rsqrt(jnp.mean(out * out, axis=-1, keepdims=True) + RMS_EPS)) * fg_ref[...]

    @pl.when(t < NTP)
    def _():
        op_ref[...] = normed

    @pl.when(t >= NTP)
    def _():
        os_ref[...] = normed


def _combine(dest, x, mod, meta, y, layer, final_g=None):
    final = final_g is not None
    in_specs = [pl.BlockSpec((1, 1, 2 * TM), lambda t: (t, 0, 0), memory_space=pltpu.SMEM),
                pl.BlockSpec((1, 1, 2 * TM), lambda t: (jnp.minimum(t + 1, NT - 1), 0, 0),
                             memory_space=pltpu.SMEM),
                pl.BlockSpec((TM, D), lambda t: (t, 0)),
                _mod_spec(layer, 5),
                pl.BlockSpec((TM, LANES), lambda t: (t, 0)),
                pl.BlockSpec(memory_space=pl.ANY)]
    args = [dest, dest, x, mod, meta, y]
    if final:
        in_specs.append(pl.BlockSpec((1, D), lambda t: (0, 0)))
        args.append(final_g.reshape(1, D))
        out_specs = [pl.BlockSpec((TM, D), lambda t: (jnp.minimum(t, NTP - 1), 0)),
                     pl.BlockSpec((TM, D), lambda t: (jnp.maximum(t - NTP, 0), 0))]
        out_shape = [jax.ShapeDtypeStruct((T_P, D), f32), jax.ShapeDtypeStruct((T_S, D), f32)]
    else:
        out_specs = pl.BlockSpec((TM, D), lambda t: (t, 0))
        out_shape = jax.ShapeDtypeStruct((T, D), f32)
    return pl.pallas_call(
        functools.partial(_combine_kernel, final=final),
        grid=(NT,),
        in_specs=in_specs,
        out_specs=out_specs,
        out_shape=out_shape,
        scratch_shapes=[pltpu.VMEM((2, 2, TM, D), f32), pltpu.SemaphoreType.DMA((2,))],
        compiler_params=_cp(("arbitrary",)),
    )(*args)


def _moe(x, mod, norm_g, router_w, router_b, w1, w3, w2, layer, final_g=None):
    h_rows, meta, meta_t, counts = _router(x, mod, norm_g, router_w, router_b, layer)
    cnt = counts[0, :N_EXPERTS].astype(jnp.int32)
    padded = ((cnt + TM_E - 1) // TM_E) * TM_E
    ends = jnp.cumsum(padded)
    starts = ends - padded
    experts = meta_t[M_E0:M_E1 + 1].astype(jnp.int32)
    ranks = meta_t[M_R0:M_R1 + 1].astype(jnp.int32)
    start_of = functools.reduce(lambda acc, e: jnp.where(experts == e, starts[e], acc), range(N_EXPERTS), 0)
    dest = start_of + ranks
    dest = dest.reshape(2, NT, TM).transpose(1, 0, 2).reshape(NT, 1, 2 * TM)
    n_used = (ends[-1] // TM_E).astype(jnp.int32).reshape(1)
    tile_start = jnp.minimum(jnp.arange(NT_E, dtype=jnp.int32), n_used[0] - 1) * TM_E
    tile_expert = jnp.sum((tile_start[:, None] >= ends[None, :]).astype(jnp.int32), axis=1)
    tile_expert = jnp.minimum(tile_expert, N_EXPERTS - 1).astype(jnp.int32)
    tile_ids = jnp.arange(NT_E, dtype=jnp.int32)
    tile_rows = jnp.clip((starts + cnt)[tile_expert] - tile_ids * TM_E, 0, TM_E)
    tile_rows = jnp.where(tile_ids < n_used[0], tile_rows, 0).astype(jnp.int32)
    xg = _dispatch((starts + cnt).astype(jnp.int32), ends.astype(jnp.int32), n_used, dest, h_rows)
    y = _experts(tile_expert, n_used, tile_rows, xg, w1, w3, w2, layer)
    return _combine(dest, x, mod, meta, y, layer, final_g)


def kernel(x_prompt, x_sample, c, cache_k, cache_v, c_ctx, ada_w, ada_b, norm1_g, norm2_g, w_in, conv_w, rpb,
           w_out, ffn_w1, ffn_w3, ffn_w2, fourier_w, router_w, router_b, moe_w1, moe_w3, moe_w2, final_g):
    cvecs = jnp.concatenate([c_ctx[None, :], c, jnp.zeros((MOD_ROWS - 1 - DEC_BATCH, D), f32)], axis=0)
    mod = _modulation(cvecs, ada_w, ada_b).reshape(DEPTH * MOD_ROWS * 6, 1, D)
    n1 = norm1_g.reshape(DEPTH, 1, D)
    n2 = norm2_g.reshape(DEPTH, 1, D)
    bias = _nbr_bias(rpb)
    cache_kt = jnp.swapaxes(cache_k, -1, -2)
    cache_vt = jnp.swapaxes(cache_v, -1, -2)
    x = (x_prompt.reshape(T_P, D), x_sample.reshape(T_S, D))
    caches = [jnp.zeros((BATCH, N_EVEN, NA_HEADS, HEAD_DIM, SEQ), f32) for _ in range(2)]
    for layer in range(DEPTH):
        li = layer // 2
        if layer % 2 == 0:
            proj, *caches = _inproj(x, mod, n1, w_in, layer, caches)
            yb_p = _ctx_attention(proj)
            yb_s = _nbr_attention(proj, cache_kt, cache_vt, bias, li)
            x = _mixout(proj, yb_p, yb_s, x, mod, conv_w, w_out, layer)
            x = _dense_ffn(x, mod, n2, ffn_w1, ffn_w3, ffn_w2, layer)
        else:
            x = _fourier(x, mod, n1, fourier_w, layer)
            last = layer == DEPTH - 1
            x = _moe(x, mod, n2, router_w, router_b, moe_w1, moe_w3, moe_w2, layer, final_g if last else None)
    y_prompt, y_sample = x
    new_kt, new_vt = caches
    return (y_prompt.reshape(BATCH, SEQ, D), y_sample.reshape(DEC_BATCH, DEC_SEQ, D),
            jnp.swapaxes(new_kt, -1, -2), jnp.swapaxes(new_vt, -1, -2))
```

```python
import functools
from typing import Any, NamedTuple

import numpy as np
import jax
import jax.numpy as jnp
from jax import lax
from jax.experimental import pallas as pl
from jax.experimental.pallas import tpu as pltpu

f32 = jnp.float32
bf16 = jnp.bfloat16

D = 1024
BATCH = 16
SEQ = 256
DEPTH = 4
DEC_BATCH = 8
DEC_SEQ = 1024
PAST_LEN = 512
GRID_W = 64
CONV_CH = 512
NA_HEADS = 8
HEAD_DIM = 64
NA_WIDTH = 512
WIN_ROWS = 8
WIN_COLS = 16
IN_WIDTH = 3072
FOURIER_GROUPS = 4
FFN_DENSE = 2816
N_EXPERTS = 8
FFN_EXPERT = 3584
N_EVEN = 2
RMS_EPS = 1e-6
NEG_INF = -1e30
SCALE = HEAD_DIM ** -0.5

LANES = 128
TM = 1024
T_P = BATCH * SEQ
T_S = DEC_BATCH * DEC_SEQ
T = T_P + T_S
NT = T // TM
NTP = T_P // TM
MOD_ROWS = 16
GRID_ROWS = DEC_SEQ // GRID_W
CHUNK_ROWS = 4
N_CHUNKS = GRID_ROWS // CHUNK_ROWS
CHUNK_Q = CHUNK_ROWS * GRID_W
KWIN_ROWS = 12
KWIN = KWIN_ROWS * GRID_W
KB0 = (0, 0, 4, 8)
KW_ROWS = (8, 12, 12, 8)
CHUNKS_PER_GROUP = 4
TF_DENSE = 256
TF_EXP = 512
TM_E = 1024
N_ASSIGN = 2 * T
NT_E = N_ASSIGN // TM_E + N_EXPERTS
R_E = NT_E * TM_E
VMEM_LIMIT = 56 * 1024 * 1024


def _cp(sem, vmem=VMEM_LIMIT):
    return pltpu.CompilerParams(dimension_semantics=sem, vmem_limit_bytes=vmem)


def _mod_index(t):
    return jnp.where(t < NTP, 0, t - (NTP - 1))


def _mod_spec(layer, part):
    def index(t, *_):
        return ((layer * MOD_ROWS + _mod_index(t)) * 6 + part, 0, 0)
    return pl.BlockSpec((1, 1, D), index)


def _norm_mod(x, g, sc, sh):
    y = x * lax.rsqrt(jnp.mean(x * x, axis=-1, keepdims=True) + RMS_EPS)
    return (y * g) * (1 + sc) + sh


def _dot(a, b):
    return jnp.dot(a, b, preferred_element_type=f32)


def _dot_nt(a, b):
    return lax.dot_general(a, b, (((1,), (1,)), ((), ())), preferred_element_type=f32)


def _mod_kernel(cv_ref, w_ref, b_ref, o_ref):
    s = jax.nn.silu(cv_ref[...]).astype(bf16)
    o_ref[0] = _dot(s, w_ref[0].astype(bf16)) + b_ref[0]


TN_IN = IN_WIDTH // 2


def _x_specs(x):
    if isinstance(x, tuple):
        return [pl.BlockSpec((TM, D), lambda i, *_: (jnp.minimum(i, NTP - 1), 0)),
                pl.BlockSpec((TM, D), lambda i, *_: (jnp.maximum(i - NTP, 0), 0))]
    return [pl.BlockSpec((TM, D), lambda i, *_: (i, 0))]


def _load_x(i, x_refs):
    if len(x_refs) == 2:
        return jnp.where(i < NTP, x_refs[0][...], x_refs[1][...])
    return x_refs[0][...]


def _inproj_kernel(*refs, n_x, n_alias):
    x_refs, refs = refs[:n_x], refs[n_x + n_alias:]
    g_ref, sc_ref, sh_ref, w_ref, proj_ref, kt_ref, vt_ref, h_ref = refs
    i = pl.program_id(0)
    j = pl.program_id(1)

    def project(h):
        acc = _dot(h, w_ref[...].astype(bf16))
        proj_ref[...] = acc.astype(bf16)
        return acc

    @pl.when(j == 0)
    def _():
        h = _norm_mod(_load_x(i, x_refs), g_ref[0], sc_ref[0], sh_ref[0]).astype(bf16)
        h_ref[...] = h
        project(h)

    @pl.when((j == 1) & (i >= NTP))
    def _():
        project(h_ref[...])

    @pl.when((j == 1) & (i < NTP))
    def _():
        acc = project(h_ref[...])
        for ref, col0 in ((kt_ref, NA_WIDTH), (vt_ref, 2 * NA_WIDTH)):
            for s in range(TM // SEQ):
                t = acc[s * SEQ:(s + 1) * SEQ, col0:col0 + NA_WIDTH].T
                for hd in range(NA_HEADS):
                    ref[s, hd] = t[hd * HEAD_DIM:(hd + 1) * HEAD_DIM, :]


def _inproj(x, mod, norm_g, w_in, layer, caches):
    li = layer // 2
    tn = TN_IN
    seqs = TM // SEQ
    xs = x if isinstance(x, tuple) else (x,)
    cache_spec = pl.BlockSpec((seqs, None, NA_HEADS, HEAD_DIM, SEQ),
                              lambda i, j: (jnp.minimum(i, NTP - 1), li, 0, 0, 0))
    cache_shape = jax.ShapeDtypeStruct((BATCH, N_EVEN, NA_HEADS, HEAD_DIM, SEQ), f32)
    aliased = tuple(caches)
    return pl.pallas_call(
        functools.partial(_inproj_kernel, n_x=len(xs), n_alias=len(aliased)),
        grid=(NT, IN_WIDTH // tn),
        in_specs=_x_specs(x) + [pl.BlockSpec(memory_space=pl.ANY)] * len(aliased) + [
            pl.BlockSpec((1, 1, D), lambda i, j: (layer, 0, 0)),
            _mod_spec(layer, 1), _mod_spec(layer, 0),
            pl.BlockSpec((None, D, tn), lambda i, j: (li, 0, j))],
        out_specs=[pl.BlockSpec((TM, tn), lambda i, j: (i, j)), cache_spec, cache_spec],
        out_shape=[jax.ShapeDtypeStruct((T, IN_WIDTH), bf16), cache_shape, cache_shape],
        scratch_shapes=[pltpu.VMEM((TM, D), bf16)],
        input_output_aliases={len(xs) + n: 1 + n for n in range(len(aliased))},
        compiler_params=_cp(("arbitrary", "arbitrary")),
    )(*xs, *aliased, norm_g, mod, mod, w_in)


HEADS_PER_STEP = LANES // HEAD_DIM


class KeyValues(NamedTuple):
    k: jax.Array
    v: jax.Array
    bias: Any = None
    feature_major: bool = False


def _pair_attention(problems):
    first = lax.broadcasted_iota(jnp.int32, (1, LANES), 1) < HEAD_DIM
    mine = (first, jnp.logical_not(first))
    first_t = lax.broadcasted_iota(jnp.int32, (LANES, 1), 0) < HEAD_DIM
    mine_t = (first_t, jnp.logical_not(first_t))
    scores = []
    for q, parts in problems:
        q = q * SCALE
        for hh in range(HEADS_PER_STEP):
            qh = jnp.where(mine[hh], q, 0)
            ss = []
            for part in parts:
                s = _dot(qh, part.k) if part.feature_major else _dot_nt(qh, part.k)
                ss.append(s if part.bias is None else s + part.bias[hh])
            scores.append(ss)
    probs = []
    for ss in scores:
        m = functools.reduce(jnp.maximum, [jnp.max(s, axis=-1, keepdims=True) for s in ss])
        probs.append([jnp.exp((s - m).astype(bf16)) for s in ss])

    def p_times_v(p, part, hh):
        if part.feature_major:
            return _dot_nt(p, jnp.where(mine_t[hh], part.v, 1))
        return _dot(p, jnp.where(mine[hh], part.v, 1))

    results = []
    for n, (_, parts) in enumerate(problems):
        outs = []
        for hh in range(HEADS_PER_STEP):
            pv = functools.reduce(jnp.add, [p_times_v(p, part, hh)
                                            for p, part in zip(probs[n * HEADS_PER_STEP + hh], parts)])
            denom = pv[:, HEAD_DIM:HEAD_DIM + 1] if hh == 0 else pv[:, 0:1]
            outs.append(pv / denom)
        results.append(jnp.where(first, outs[0], outs[1]))
    return results


def _ctx_attn_kernel(q_ref, k_ref, v_ref, o_ref):
    cols = [slice(hp * LANES, (hp + 1) * LANES) for hp in range(NA_HEADS // HEADS_PER_STEP)]
    outs = _pair_attention([(q_ref[:, sl], [KeyValues(k_ref[:, sl], v_ref[:, sl])]) for sl in cols])
    for sl, o in zip(cols, outs):
        o_ref[:, sl] = o.astype(bf16)


def _ctx_attention(proj):
    col = IN_WIDTH // NA_WIDTH - 3
    return pl.pallas_call(
        _ctx_attn_kernel,
        grid=(BATCH,),
        in_specs=[pl.BlockSpec((SEQ, NA_WIDTH), lambda b: (b, col)),
                  pl.BlockSpec((SEQ, NA_WIDTH), lambda b: (b, col + 1)),
                  pl.BlockSpec((SEQ, NA_WIDTH), lambda b: (b, col + 2))],
        out_specs=pl.BlockSpec((SEQ, NA_WIDTH), lambda b: (b, 0)),
        out_shape=jax.ShapeDtypeStruct((T_P, NA_WIDTH), bf16),
        compiler_params=_cp(("arbitrary",)),
    )(proj, proj, proj)


def _win_start(qr):
    return min(max(qr - WIN_ROWS // 2, 0), GRID_ROWS - WIN_ROWS)


N_ROFF = 2 * WIN_ROWS - 1
N_COFF = 2 * WIN_COLS - 1


def _bias_kernel(rpb_ref, o_ref):
    base = pl.program_id(0) * (N_ROFF * N_COFF)
    qc = lax.broadcasted_iota(jnp.int32, (GRID_W, GRID_W), 0)
    kc = lax.broadcasted_iota(jnp.int32, (GRID_W, GRID_W), 1)
    coff = kc - qc + (WIN_COLS - 1)
    cs = jnp.clip(qc - WIN_COLS // 2, 0, GRID_W - WIN_COLS)
    valid = (kc >= cs) & (kc < cs + WIN_COLS)
    neg = jnp.full((GRID_W, GRID_W), NEG_INF, f32)
    blocks = []
    for a in range(N_ROFF):
        t = jnp.zeros((GRID_W, GRID_W), f32)
        for b in range(N_COFF):
            t = jnp.where(coff == b, rpb_ref[base + a * N_COFF + b], t)
        blocks.append(jnp.where(valid, t, neg))
    for c in range(N_CHUNKS):
        for ql in range(CHUNK_ROWS):
            qr = c * CHUNK_ROWS + ql
            st = _win_start(qr)
            for kl in range(KWIN_ROWS):
                kr = KB0[c] + kl
                blk = blocks[kr - qr + WIN_ROWS - 1] if st <= kr < st + WIN_ROWS else neg
                o_ref[c, ql * GRID_W:(ql + 1) * GRID_W, kl * GRID_W:(kl + 1) * GRID_W] = blk


MOD_TN = 6 * D // 4
PREP_STEPS = DEPTH * (6 * D // MOD_TN)
assert PREP_STEPS == N_EVEN * NA_HEADS


def _prep_kernel(cv_ref, w_ref, b_ref, rpb_ref, mod_ref, bias_ref):
    _mod_kernel(cv_ref, w_ref, b_ref, mod_ref)
    _bias_kernel(rpb_ref, bias_ref)


def _prepare(cvecs, ada_w, ada_b, rpb):
    per_layer = 6 * D // MOD_TN
    return pl.pallas_call(
        _prep_kernel,
        grid=(PREP_STEPS,),
        in_specs=[pl.BlockSpec((MOD_ROWS, D), lambda s: (0, 0)),
                  pl.BlockSpec((1, D, MOD_TN), lambda s: (s // per_layer, 0, s % per_layer)),
                  pl.BlockSpec((1, 1, MOD_TN), lambda s: (s // per_layer, 0, s % per_layer)),
                  pl.BlockSpec(memory_space=pltpu.SMEM)],
        out_specs=[pl.BlockSpec((1, MOD_ROWS, MOD_TN), lambda s: (s // per_layer, 0, s % per_layer)),
                   pl.BlockSpec((None, None, N_CHUNKS, CHUNK_Q, KWIN),
                                lambda s: (s // NA_HEADS, s % NA_HEADS, 0, 0, 0))],
        out_shape=[jax.ShapeDtypeStruct((DEPTH, MOD_ROWS, 6 * D), f32),
                   jax.ShapeDtypeStruct((N_EVEN, NA_HEADS, N_CHUNKS, CHUNK_Q, KWIN), f32)],
        compiler_params=_cp(("arbitrary",)),
    )(cvecs, ada_w, ada_b.reshape(DEPTH, 1, 6 * D), rpb.reshape(-1))


def _nbr_attn_kernel(q_ref, k_ref, v_ref, kc_ref, vc_ref, bias_ref, o_ref):
    ctx = KeyValues(kc_ref[...].reshape(LANES, PAST_LEN).astype(bf16),
                    vc_ref[...].reshape(LANES, PAST_LEN).astype(bf16), feature_major=True)
    for c0 in range(0, N_CHUNKS, CHUNKS_PER_GROUP):
        problems = []
        for c in range(c0, c0 + CHUNKS_PER_GROUP):
            rows = slice(c * CHUNK_Q, (c + 1) * CHUNK_Q)
            n_keys = KW_ROWS[c] * GRID_W
            win = slice(KB0[c] * GRID_W, KB0[c] * GRID_W + n_keys)
            bias = [bias_ref[hh, c, :, :n_keys] for hh in range(HEADS_PER_STEP)]
            problems.append((q_ref[rows, :], [KeyValues(k_ref[win, :], v_ref[win, :], bias), ctx]))
        for c, o in zip(range(c0, c0 + CHUNKS_PER_GROUP), _pair_attention(problems)):
            o_ref[c * CHUNK_Q:(c + 1) * CHUNK_Q, :] = o.astype(bf16)


def _nbr_attention(proj, cache_k, cache_v, bias, li):
    qcol = 3 * CONV_CH // LANES
    ncol = NA_WIDTH // LANES
    hp_steps = NA_HEADS // HEADS_PER_STEP

    def col_spec(which):
        return pl.BlockSpec((DEC_SEQ, LANES), lambda hp, b: (NTP + b, qcol + which * ncol + hp))

    ctx_spec = pl.BlockSpec((None, None, HEADS_PER_STEP, HEAD_DIM, PAST_LEN),
                            lambda hp, b: (b, li, hp, 0, 0))
    return pl.pallas_call(
        _nbr_attn_kernel,
        grid=(hp_steps, DEC_BATCH),
        in_specs=[col_spec(0), col_spec(1), col_spec(2), ctx_spec, ctx_spec,
                  pl.BlockSpec((None, HEADS_PER_STEP, N_CHUNKS, CHUNK_Q, KWIN),
                               lambda hp, b: (li, hp, 0, 0, 0))],
        out_specs=pl.BlockSpec((DEC_SEQ, LANES), lambda hp, b: (b, hp)),
        out_shape=jax.ShapeDtypeStruct((T_S, NA_WIDTH), bf16),
        compiler_params=_cp(("arbitrary", "arbitrary")),
    )(proj, proj, proj, cache_k, cache_v, bias)


def _mixout_kernel(*refs, n_x):
    x_refs, (a_ref, ybp_ref, ybs_ref, g1_ref, cw_ref, w_ref, o_ref) = refs[:n_x], refs[n_x:]
    t = pl.program_id(0)
    a_b = a_ref[:, 0:CONV_CH].astype(f32)
    a_c = a_ref[:, CONV_CH:2 * CONV_CH].astype(f32)
    a_x = a_ref[:, 2 * CONV_CH:3 * CONV_CH].astype(f32)
    u = a_c * a_x
    r = lax.broadcasted_iota(jnp.int32, (TM, 1), 0)
    pos = jnp.where(t < NTP, r % SEQ, r)
    last = jnp.where(t < NTP, SEQ - 1, DEC_SEQ - 1)
    u_prev = jnp.where(pos == 0, 0.0, pltpu.roll(u, 1, axis=0))
    u_next = jnp.where(pos == last, 0.0, pltpu.roll(u, TM - 1, axis=0))
    y_a = a_b * (u_prev * cw_ref[0:1, :] + u * cw_ref[1:2, :] + u_next * cw_ref[2:3, :])
    y_b = jnp.where(t < NTP, ybp_ref[...], ybs_ref[...])
    y = (_dot(y_a.astype(bf16), w_ref[0:CONV_CH, :].astype(bf16))
         + _dot(y_b, w_ref[CONV_CH:, :].astype(bf16)))
    o_ref[...] = _load_x(t, x_refs) + g1_ref[0] * y


def _mixout(proj, yb_p, yb_s, x, mod, conv_w, w_out, layer):
    li = layer // 2
    xs = x if isinstance(x, tuple) else (x,)
    return pl.pallas_call(
        functools.partial(_mixout_kernel, n_x=len(xs)),
        grid=(NT,),
        in_specs=_x_specs(x) + [
            pl.BlockSpec((TM, 3 * CONV_CH), lambda t: (t, 0)),
            pl.BlockSpec((TM, NA_WIDTH), lambda t: (jnp.minimum(t, NTP - 1), 0)),
            pl.BlockSpec((TM, NA_WIDTH), lambda t: (jnp.maximum(t - NTP, 0), 0)),
            _mod_spec(layer, 2),
            pl.BlockSpec((None, 3, CONV_CH), lambda t: (li, 0, 0)),
            pl.BlockSpec((None, D, D), lambda t: (li, 0, 0))],
        out_specs=pl.BlockSpec((TM, D), lambda t: (t, 0)),
        out_shape=jax.ShapeDtypeStruct((T, D), f32),
        compiler_params=_cp(("arbitrary",)),
    )(*xs, proj, yb_p, yb_s, mod, conv_w, w_out)


FFN_TILES = 2


def _ffn_kernel(x_ref, g_ref, *refs):
    mods, (w1_ref, w3_ref, w2_ref, o_ref, h_ref) = refs[:3 * FFN_TILES], refs[3 * FFN_TILES:]
    j = pl.program_id(1)
    last = pl.num_programs(1) - 1
    parts = [(slice(n * TM, (n + 1) * TM), mods[3 * n:3 * n + 3]) for n in range(FFN_TILES)]

    def hidden_step(h):
        gate = jax.nn.silu(_dot(h, w1_ref[...].astype(bf16))) * _dot(h, w3_ref[...].astype(bf16))
        return _dot(gate.astype(bf16), w2_ref[...].astype(bf16))

    @pl.when(j == 0)
    def _():
        h = jnp.concatenate([_norm_mod(x_ref[rows, :], g_ref[0], sc_ref[0], sh_ref[0]).astype(bf16)
                             for rows, (sc_ref, sh_ref, _) in parts], axis=0)
        h_ref[...] = h
        o_ref[...] = hidden_step(h)

    @pl.when((j > 0) & (j < last))
    def _():
        o_ref[...] += hidden_step(h_ref[...])

    @pl.when(j == last)
    def _():
        acc = o_ref[...] + hidden_step(h_ref[...])
        for rows, (_, _, g2_ref) in parts:
            o_ref[rows, :] = x_ref[rows, :] + g2_ref[0] * acc[rows, :]


def _dense_ffn(x, mod, norm_g, w1, w3, w2, layer):
    li = layer // 2
    tf = TF_DENSE
    rows = FFN_TILES * TM

    def mod_spec(part, n):
        return pl.BlockSpec((1, 1, D), lambda i, j: (
            (layer * MOD_ROWS + _mod_index(i * FFN_TILES + n)) * 6 + part, 0, 0))

    mod_specs = [mod_spec(part, n) for n in range(FFN_TILES) for part in (4, 3, 5)]
    return pl.pallas_call(
        _ffn_kernel,
        grid=(T // rows, FFN_DENSE // tf),
        in_specs=[pl.BlockSpec((rows, D), lambda i, j: (i, 0)),
                  pl.BlockSpec((1, 1, D), lambda i, j: (layer, 0, 0))] + mod_specs + [
                  pl.BlockSpec((None, D, tf), lambda i, j: (li, 0, j)),
                  pl.BlockSpec((None, D, tf), lambda i, j: (li, 0, j)),
                  pl.BlockSpec((None, tf, D), lambda i, j: (li, j, 0))],
        out_specs=pl.BlockSpec((rows, D), lambda i, j: (i, 0)),
        out_shape=jax.ShapeDtypeStruct((T, D), f32),
        scratch_shapes=[pltpu.VMEM((rows, D), bf16)],
        compiler_params=_cp(("arbitrary", "arbitrary")),
    )(x, norm_g, *([mod] * len(mod_specs)), w1, w3, w2)


GROUP_CH = D // FOURIER_GROUPS


def _dft_mats(n):
    k = np.arange(n, dtype=np.int64)
    ang = 2.0 * np.pi * ((k[:, None] * k[None, :]) % n).astype(np.float64) / n
    return np.cos(ang).astype(np.float32), np.sin(ang).astype(np.float32)


def _fourier_kernel(x_ref, g_ref, sc_ref, sh_ref, g1_ref, cs_ref, ss_ref, cl_ref, sl_ref, wf_ref,
                    o_ref, f_ref):
    t = pl.program_id(0)
    h = _norm_mod(x_ref[...], g_ref[0], sc_ref[0], sh_ref[0]).astype(bf16)
    cs = cs_ref[...].astype(bf16)
    ss = ss_ref[...].astype(bf16)
    ys, zs = [], []
    for g in range(FOURIER_GROUPS):
        hg = h[:, g * GROUP_CH:(g + 1) * GROUP_CH]
        ys.append(_dot(hg, cs))
        zs.append(_dot(hg, ss))
    y = jnp.concatenate(ys, axis=-1).astype(bf16)
    z = jnp.concatenate(zs, axis=-1).astype(bf16)

    @pl.when(t < NTP)
    def _():
        for s in range(TM // SEQ):
            rows = slice(s * SEQ, (s + 1) * SEQ)
            f = _dot(cs, y[rows]) - _dot(ss, z[rows])
            f_ref[rows, :] = (f * ((SEQ * GROUP_CH) ** -0.5)).astype(bf16)

    @pl.when(t >= NTP)
    def _():
        f = _dot(cl_ref[...].astype(bf16), y) - _dot(sl_ref[...].astype(bf16), z)
        f_ref[...] = (f * ((DEC_SEQ * GROUP_CH) ** -0.5)).astype(bf16)

    o_ref[...] = x_ref[...] + g1_ref[0] * _dot(f_ref[...], wf_ref[...].astype(bf16))


def _fourier(x, mod, norm_g, fourier_w, layer):
    li = layer // 2
    c_s, s_s = _dft_mats(SEQ)
    c_l, s_l = _dft_mats(DEC_SEQ)
    const = lambda shape: pl.BlockSpec(shape, lambda t: (0, 0))
    return pl.pallas_call(
        _fourier_kernel,
        grid=(NT,),
        in_specs=[pl.BlockSpec((TM, D), lambda t: (t, 0)),
                  pl.BlockSpec((1, 1, D), lambda t: (layer, 0, 0)),
                  _mod_spec(layer, 1), _mod_spec(layer, 0), _mod_spec(layer, 2),
                  const((SEQ, SEQ)), const((SEQ, SEQ)),
                  const((DEC_SEQ, DEC_SEQ)), const((DEC_SEQ, DEC_SEQ)),
                  pl.BlockSpec((None, D, D), lambda t: (li, 0, 0))],
        out_specs=pl.BlockSpec((TM, D), lambda t: (t, 0)),
        out_shape=jax.ShapeDtypeStruct((T, D), f32),
        scratch_shapes=[pltpu.VMEM((TM, D), bf16)],
        compiler_params=_cp(("arbitrary",)),
    )(x, norm_g, mod, mod, mod, jnp.asarray(c_s), jnp.asarray(s_s), jnp.asarray(c_l), jnp.asarray(s_l),
      fourier_w)


M_E0, M_E1, M_R0, M_R1, M_W0, M_W1 = range(6)
ROW_UNROLL = 8


META_ROWS = 8


def _router_kernel(x_ref, g_ref, sc_ref, sh_ref, rw_ref, rb_ref, h_ref, meta_ref, meta_t_ref, cnt_ref,
                   carry_ref):
    t = pl.program_id(0)

    @pl.when(t == 0)
    def _():
        carry_ref[...] = jnp.zeros_like(carry_ref)

    h = _norm_mod(x_ref[...], g_ref[0], sc_ref[0], sh_ref[0])
    h_ref[...] = h
    h_hi = h.astype(bf16)
    h_lo = (h - h_hi.astype(f32)).astype(bf16)
    hi_terms = _dot(h_hi, rw_ref[...])
    lo_term = _dot(h_lo, rw_ref[:, :LANES])
    logits = ((hi_terms[:, :LANES] + lo_term) + hi_terms[:, LANES:]) + rb_ref[...]
    lane = lax.broadcasted_iota(jnp.int32, (TM, LANES), 1)
    m1 = jnp.max(logits, axis=-1, keepdims=True)
    i1 = jnp.min(jnp.where(logits == m1, lane, LANES), axis=-1, keepdims=True)
    rest = jnp.where(lane == i1, -jnp.inf, logits)
    m2 = jnp.max(rest, axis=-1, keepdims=True)
    i2 = jnp.min(jnp.where(rest == m2, lane, LANES), axis=-1, keepdims=True)
    e = jnp.exp(m2 - m1)
    w0 = 1.0 / (1.0 + e)
    w1 = e / (1.0 + e)
    oh0 = (lane == i1).astype(f32)
    oh1 = (lane == i2).astype(f32)
    oh = oh0 + oh1
    row = lax.broadcasted_iota(jnp.int32, (TM, TM), 0)
    col = lax.broadcasted_iota(jnp.int32, (TM, TM), 1)
    before = jnp.where(col < row, 1.0, 0.0).astype(bf16)
    base = carry_ref[...] + _dot(before, oh.astype(bf16))
    r0 = jnp.sum(oh0 * base, axis=-1, keepdims=True)
    r1 = jnp.sum(oh1 * base, axis=-1, keepdims=True)
    carry_ref[...] += jnp.sum(oh, axis=0, keepdims=True)
    rec = jnp.zeros((TM, LANES), f32)
    for idx, val in ((M_E0, i1.astype(f32)), (M_E1, i2.astype(f32)), (M_R0, r0), (M_R1, r1),
                     (M_W0, w0), (M_W1, w1)):
        rec = jnp.where(lane == idx, val, rec)
    meta_ref[...] = rec
    meta_t_ref[...] = rec.T[:META_ROWS, :]
    cnt_ref[...] = carry_ref[...]


def _router(x, mod, norm_g, router_w, router_b, layer):
    li = layer // 2
    rw = jnp.pad(router_w[li], ((0, 0), (0, LANES - N_EXPERTS)))
    rw_hi = rw.astype(bf16)
    rw_lo = (rw - rw_hi.astype(f32)).astype(bf16)
    rw = jnp.concatenate([rw_hi, rw_lo], axis=1)
    rb = jnp.pad(router_b[li], (0, LANES - N_EXPERTS), constant_values=NEG_INF).reshape(1, LANES)
    return pl.pallas_call(
        _router_kernel,
        grid=(NT,),
        in_specs=[pl.BlockSpec((TM, D), lambda t: (t, 0)),
                  pl.BlockSpec((1, 1, D), lambda t: (layer, 0, 0)),
                  _mod_spec(layer, 4), _mod_spec(layer, 3),
                  pl.BlockSpec((D, 2 * LANES), lambda t: (0, 0)),
                  pl.BlockSpec((1, LANES), lambda t: (0, 0))],
        out_specs=[pl.BlockSpec((TM, D), lambda t: (t, 0)),
                   pl.BlockSpec((TM, LANES), lambda t: (t, 0)),
                   pl.BlockSpec((META_ROWS, TM), lambda t: (0, t)),
                   pl.BlockSpec((1, LANES), lambda t: (0, 0))],
        out_shape=[jax.ShapeDtypeStruct((T, D), f32),
                   jax.ShapeDtypeStruct((T, LANES), f32),
                   jax.ShapeDtypeStruct((META_ROWS, T), f32),
                   jax.ShapeDtypeStruct((1, LANES), f32)],
        scratch_shapes=[pltpu.VMEM((1, LANES), f32)],
        compiler_params=_cp(("arbitrary",)),
    )(x, norm_g, mod, mod, rw, rb)


def _row_copy(src_ref, src_row, dst_ref, dst_row, sem):
    return pltpu.make_async_copy(src_ref.at[pl.ds(src_row, 1)], dst_ref.at[pl.ds(dst_row, 1)], sem)


ZERO_BLOCK = 8


def _dispatch_kernel(lo_ref, hi_ref, nu_ref, dest_ref, h_ref, xg_ref, zero_ref, sem, zsem):
    @pl.when(pl.program_id(0) == 0)
    def _():
        zero_ref[...] = jnp.zeros_like(zero_ref)

        def row_zero(r):
            return _row_copy(zero_ref, 0, xg_ref, r, zsem)

        def block_zero(b):
            return pltpu.make_async_copy(zero_ref.at[pl.ds(0, ZERO_BLOCK)],
                                         xg_ref.at[pl.ds(pl.multiple_of(b * ZERO_BLOCK, ZERO_BLOCK), ZERO_BLOCK)],
                                         zsem)

        def tile_zero(tile):
            return pltpu.make_async_copy(zero_ref, xg_ref.at[pl.ds(pl.multiple_of(tile * TM_E, TM_E), TM_E)],
                                         zsem)

        def each(lo, hi, copy, wait):
            def step(i, c):
                if wait:
                    copy(i).wait()
                else:
                    copy(i).start()
                return c
            lax.fori_loop(lo, hi, step, 0)

        for wait in (False, True):
            for e in range(N_EXPERTS):
                lo, hi = lo_ref[e], hi_ref[e]
                aligned = jnp.minimum((lo + ZERO_BLOCK - 1) // ZERO_BLOCK * ZERO_BLOCK, hi)
                each(lo, aligned, row_zero, wait)
                each(aligned // ZERO_BLOCK, hi // ZERO_BLOCK, block_zero, wait)
            each(nu_ref[0], NT_E, tile_zero, wait)

    def issue(g, c):
        for u in range(ROW_UNROLL):
            r = g * ROW_UNROLL + u
            for k in range(2):
                _row_copy(h_ref, r, xg_ref, dest_ref[0, 0, k * TM + r], sem).start()
        return c

    lax.fori_loop(0, TM // ROW_UNROLL, issue, 0)
    for k in range(2):
        pltpu.make_async_copy(h_ref, xg_ref.at[pl.ds(0, TM)], sem).wait()


def _dispatch(pad_lo, pad_hi, n_used, dest, h):
    grid_spec = pltpu.PrefetchScalarGridSpec(
        num_scalar_prefetch=3,
        grid=(NT,),
        in_specs=[pl.BlockSpec((1, 1, 2 * TM), lambda t, *_: (t, 0, 0), memory_space=pltpu.SMEM),
                  pl.BlockSpec((TM, D), lambda t, *_: (t, 0))],
        out_specs=pl.BlockSpec(memory_space=pl.ANY),
        scratch_shapes=[pltpu.VMEM((TM_E, D), f32), pltpu.SemaphoreType.DMA(()), pltpu.SemaphoreType.DMA(())],
    )
    return pl.pallas_call(
        _dispatch_kernel,
        grid_spec=grid_spec,
        out_shape=jax.ShapeDtypeStruct((R_E, D), f32),
        compiler_params=_cp(("arbitrary",)),
    )(pad_lo, pad_hi, n_used, dest, h)


NJ_EXP = FFN_EXPERT // TF_EXP
ROW_CLASSES = tuple(TM_E * q // 4 for q in (1, 2, 3, 4))


def _expert_kernel(te_ref, nu_ref, rows_ref, xg_ref, w1_ref, w3_ref, w2_ref, y_ref, h_ref):
    t = pl.program_id(0)
    j = pl.program_id(1)
    n_rows = rows_ref[t]

    def hidden_step(h):
        gate = jax.nn.silu(_dot(h, w1_ref[...].astype(bf16))) * _dot(h, w3_ref[...].astype(bf16))
        return _dot(gate.astype(bf16), w2_ref[...].astype(bf16))

    @pl.when((j == 0) & (n_rows == 0))
    def _():
        y_ref[...] = jnp.zeros_like(y_ref)

    for below, m in zip((0,) + ROW_CLASSES, ROW_CLASSES):
        in_class = (n_rows > below) & (n_rows <= m)

        @pl.when(in_class & (j == 0))
        def _():
            h = xg_ref[:m, :].astype(bf16)
            h_ref[:m, :] = h
            y_ref[:m, :] = hidden_step(h)
            if m < TM_E:
                y_ref[m:, :] = jnp.zeros((TM_E - m, D), f32)

        @pl.when(in_class & (j > 0))
        def _():
            y_ref[:m, :] += hidden_step(h_ref[:m, :])


def _experts(tile_expert, n_used, tile_rows, xg, w1, w3, w2, layer):
    li = layer // 2
    tf = TF_EXP
    nj = NJ_EXP

    def jj(t, j, nu):
        return jnp.where(t < nu[0], j, nj - 1)

    def tt(t, nu):
        return jnp.minimum(t, jnp.maximum(nu[0] - 1, 0))

    grid_spec = pltpu.PrefetchScalarGridSpec(
        num_scalar_prefetch=3,
        grid=(NT_E, nj),
        in_specs=[pl.BlockSpec((TM_E, D), lambda t, j, te, nu, nr: (tt(t, nu), 0)),
                  pl.BlockSpec((None, None, D, tf), lambda t, j, te, nu, nr: (li, te[t], 0, jj(t, j, nu))),
                  pl.BlockSpec((None, None, D, tf), lambda t, j, te, nu, nr: (li, te[t], 0, jj(t, j, nu))),
                  pl.BlockSpec((None, None, tf, D), lambda t, j, te, nu, nr: (li, te[t], jj(t, j, nu), 0))],
        out_specs=pl.BlockSpec((TM_E, D), lambda t, j, te, nu, nr: (t, 0)),
        scratch_shapes=[pltpu.VMEM((TM_E, D), bf16)],
    )
    return pl.pallas_call(
        _expert_kernel,
        grid_spec=grid_spec,
        out_shape=jax.ShapeDtypeStruct((R_E, D), f32),
        compiler_params=_cp(("arbitrary", "arbitrary")),
    )(tile_expert, n_used, tile_rows, xg, w1, w3, w2)


def _combine_kernel(dest_ref, dest_next_ref, x_ref, g2_ref, meta_ref, y_ref, *rest, final):
    if final:
        fg_ref, op_ref, os_ref, buf_ref, sems = rest
    else:
        o_ref, buf_ref, sems = rest
    t = pl.program_id(0)
    slot = t % 2

    def gather(rows_ref, s):
        def issue(g, c):
            for u in range(ROW_UNROLL):
                r = g * ROW_UNROLL + u
                for k in range(2):
                    _row_copy(y_ref, rows_ref[0, 0, k * TM + r], buf_ref.at[s, k], r, sems.at[s]).start()
            return c
        lax.fori_loop(0, TM // ROW_UNROLL, issue, 0)

    @pl.when(t == 0)
    def _():
        gather(dest_ref, 0)

    @pl.when(t + 1 < NT)
    def _():
        gather(dest_next_ref, 1 - slot)

    for k in range(2):
        pltpu.make_async_copy(y_ref.at[pl.ds(0, TM)], buf_ref.at[slot, k], sems.at[slot]).wait()
    w0 = meta_ref[:, M_W0:M_W0 + 1]
    w1 = meta_ref[:, M_W1:M_W1 + 1]
    out = x_ref[...] + g2_ref[0] * (w0 * buf_ref[slot, 0] + w1 * buf_ref[slot, 1])
    if not final:
        o_ref[...] = out
        return
    normed = (out * lax.rsqrt(jnp.mean(out * out, axis=-1, keepdims=True) + RMS_EPS)) * fg_ref[...]

    @pl.when(t < NTP)
    def _():
        op_ref[...] = normed

    @pl.when(t >= NTP)
    def _():
        os_ref[...] = normed


def _combine(dest, x, mod, meta, y, layer, final_g=None):
    final = final_g is not None
    in_specs = [pl.BlockSpec((1, 1, 2 * TM), lambda t: (t, 0, 0), memory_space=pltpu.SMEM),
                pl.BlockSpec((1, 1, 2 * TM), lambda t: (jnp.minimum(t + 1, NT - 1), 0, 0),
                             memory_space=pltpu.SMEM),
                pl.BlockSpec((TM, D), lambda t: (t, 0)),
                _mod_spec(layer, 5),
                pl.BlockSpec((TM, LANES), lambda t: (t, 0)),
                pl.BlockSpec(memory_space=pl.ANY)]
    args = [dest, dest, x, mod, meta, y]
    if final:
        in_specs.append(pl.BlockSpec((1, D), lambda t: (0, 0)))
        args.append(final_g.reshape(1, D))
        out_specs = [pl.BlockSpec((TM, D), lambda t: (jnp.minimum(t, NTP - 1), 0)),
                     pl.BlockSpec((TM, D), lambda t: (jnp.maximum(t - NTP, 0), 0))]
        out_shape = [jax.ShapeDtypeStruct((T_P, D), f32), jax.ShapeDtypeStruct((T_S, D), f32)]
    else:
        out_specs = pl.BlockSpec((TM, D), lambda t: (t, 0))
        out_shape = jax.ShapeDtypeStruct((T, D), f32)
    return pl.pallas_call(
        functools.partial(_combine_kernel, final=final),
        grid=(NT,),
        in_specs=in_specs,
        out_specs=out_specs,
        out_shape=out_shape,
        scratch_shapes=[pltpu.VMEM((2, 2, TM, D), f32), pltpu.SemaphoreType.DMA((2,))],
        compiler_params=_cp(("arbitrary",)),
    )(*args)


def _moe(x, mod, norm_g, router_w, router_b, w1, w3, w2, layer, final_g=None):
    h_rows, meta, meta_t, counts = _router(x, mod, norm_g, router_w, router_b, layer)
    cnt = counts[0, :N_EXPERTS].astype(jnp.int32)
    padded = ((cnt + TM_E - 1) // TM_E) * TM_E
    ends = jnp.cumsum(padded)
    starts = ends - padded
    experts = meta_t[M_E0:M_E1 + 1].astype(jnp.int32)
    ranks = meta_t[M_R0:M_R1 + 1].astype(jnp.int32)
    start_of = functools.reduce(lambda acc, e: jnp.where(experts == e, starts[e], acc), range(N_EXPERTS), 0)
    dest = start_of + ranks
    dest = dest.reshape(2, NT, TM).transpose(1, 0, 2).reshape(NT, 1, 2 * TM)
    n_used = (ends[-1] // TM_E).astype(jnp.int32).reshape(1)
    tile_start = jnp.minimum(jnp.arange(NT_E, dtype=jnp.int32), n_used[0] - 1) * TM_E
    tile_expert = jnp.sum((tile_start[:, None] >= ends[None, :]).astype(jnp.int32), axis=1)
    tile_expert = jnp.minimum(tile_expert, N_EXPERTS - 1).astype(jnp.int32)
    tile_ids = jnp.arange(NT_E, dtype=jnp.int32)
    tile_rows = jnp.clip((starts + cnt)[tile_expert] - tile_ids * TM_E, 0, TM_E)
    tile_rows = jnp.where(tile_ids < n_used[0], tile_rows, 0).astype(jnp.int32)
    xg = _dispatch((starts + cnt).astype(jnp.int32), ends.astype(jnp.int32), n_used, dest, h_rows)
    y = _experts(tile_expert, n_used, tile_rows, xg, w1, w3, w2, layer)
    return _combine(dest, x, mod, meta, y, layer, final_g)


def kernel(x_prompt, x_sample, c, cache_k, cache_v, c_ctx, ada_w, ada_b, norm1_g, norm2_g, w_in, conv_w, rpb,
           w_out, ffn_w1, ffn_w3, ffn_w2, fourier_w, router_w, router_b, moe_w1, moe_w3, moe_w2, final_g):
    cvecs = jnp.concatenate([c_ctx[None, :], c, jnp.zeros((MOD_ROWS - 1 - DEC_BATCH, D), f32)], axis=0)
    mod, bias = _prepare(cvecs, ada_w, ada_b, rpb)
    mod = mod.reshape(DEPTH * MOD_ROWS * 6, 1, D)
    n1 = norm1_g.reshape(DEPTH, 1, D)
    n2 = norm2_g.reshape(DEPTH, 1, D)
    cache_kt = jnp.swapaxes(cache_k, -1, -2)
    cache_vt = jnp.swapaxes(cache_v, -1, -2)
    x = (x_prompt.reshape(T_P, D), x_sample.reshape(T_S, D))
    caches = [jnp.zeros((BATCH, N_EVEN, NA_HEADS, HEAD_DIM, SEQ), f32) for _ in range(2)]
    for layer in range(DEPTH):
        li = layer // 2
        if layer % 2 == 0:
            proj, *caches = _inproj(x, mod, n1, w_in, layer, caches)
            yb_p = _ctx_attention(proj)
            yb_s = _nbr_attention(proj, cache_kt, cache_vt, bias, li)
            x = _mixout(proj, yb_p, yb_s, x, mod, conv_w, w_out, layer)
            x = _dense_ffn(x, mod, n2, ffn_w1, ffn_w3, ffn_w2, layer)
        else:
            x = _fourier(x, mod, n1, fourier_w, layer)
            last = layer == DEPTH - 1
            x = _moe(x, mod, n2, router_w, router_b, moe_w1, moe_w3, moe_w2, layer, final_g if last else None)
    y_prompt, y_sample = x
    new_kt, new_vt = caches
    return (y_prompt.reshape(BATCH, SEQ, D), y_sample.reshape(DEC_BATCH, DEC_SEQ, D),
            jnp.swapaxes(new_kt, -1, -2), jnp.swapaxes(new_vt, -1, -2))
```

```python
import functools
from typing import Any, NamedTuple

import numpy as np
import jax
import jax.numpy as jnp
from jax import lax
from jax.experimental import pallas as pl
from jax.experimental.pallas import tpu as pltpu

f32 = jnp.float32
bf16 = jnp.bfloat16

D = 1024
BATCH = 16
SEQ = 256
DEPTH = 4
DEC_BATCH = 8
DEC_SEQ = 1024
PAST_LEN = 512
GRID_W = 64
CONV_CH = 512
NA_HEADS = 8
HEAD_DIM = 64
NA_WIDTH = 512
WIN_ROWS = 8
WIN_COLS = 16
IN_WIDTH = 3072
FOURIER_GROUPS = 4
FFN_DENSE = 2816
N_EXPERTS = 8
FFN_EXPERT = 3584
N_EVEN = 2
RMS_EPS = 1e-6
NEG_INF = -1e30
SCALE = HEAD_DIM ** -0.5

LANES = 128
TM = 1024
T_P = BATCH * SEQ
T_S = DEC_BATCH * DEC_SEQ
T = T_P + T_S
NT = T // TM
NTP = T_P // TM
MOD_ROWS = 16
GRID_ROWS = DEC_SEQ // GRID_W
CHUNK_ROWS = 4
N_CHUNKS = GRID_ROWS // CHUNK_ROWS
CHUNK_Q = CHUNK_ROWS * GRID_W
KWIN_ROWS = 12
KWIN = KWIN_ROWS * GRID_W
KB0 = (0, 0, 4, 8)
KW_ROWS = (8, 12, 12, 8)
CHUNKS_PER_GROUP = 4
TF_DENSE = 256
TF_EXP = 512
TM_E = 1024
N_ASSIGN = 2 * T
NT_E = N_ASSIGN // TM_E + N_EXPERTS
R_E = NT_E * TM_E
VMEM_LIMIT = 56 * 1024 * 1024


def _cp(sem, vmem=VMEM_LIMIT):
    return pltpu.CompilerParams(dimension_semantics=sem, vmem_limit_bytes=vmem)


def _mod_index(t):
    return jnp.where(t < NTP, 0, t - (NTP - 1))


def _mod_spec(layer, part):
    def index(t, *_):
        return ((layer * MOD_ROWS + _mod_index(t)) * 6 + part, 0, 0)
    return pl.BlockSpec((1, 1, D), index)


def _norm_mod(x, g, sc, sh):
    y = x * lax.rsqrt(jnp.mean(x * x, axis=-1, keepdims=True) + RMS_EPS)
    return (y * g) * (1 + sc) + sh


def _dot(a, b):
    return jnp.dot(a, b, preferred_element_type=f32)


def _dot_nt(a, b):
    return lax.dot_general(a, b, (((1,), (1,)), ((), ())), preferred_element_type=f32)


def _mod_kernel(cv_ref, w_ref, b_ref, o_ref):
    s = jax.nn.silu(cv_ref[...]).astype(bf16)
    o_ref[0] = _dot(s, w_ref[0].astype(bf16)) + b_ref[0]


TN_IN = IN_WIDTH // 2


def _x_specs(x):
    if isinstance(x, tuple):
        return [pl.BlockSpec((TM, D), lambda i, *_: (jnp.minimum(i, NTP - 1), 0)),
                pl.BlockSpec((TM, D), lambda i, *_: (jnp.maximum(i - NTP, 0), 0))]
    return [pl.BlockSpec((TM, D), lambda i, *_: (i, 0))]


def _load_x(i, x_refs):
    if len(x_refs) == 2:
        return jnp.where(i < NTP, x_refs[0][...], x_refs[1][...])
    return x_refs[0][...]


def _inproj_kernel(*refs, n_x, n_alias):
    x_refs, refs = refs[:n_x], refs[n_x + n_alias:]
    g_ref, sc_ref, sh_ref, w_ref, proj_ref, kt_ref, vt_ref, h_ref = refs
    i = pl.program_id(0)
    j = pl.program_id(1)

    def project(h):
        acc = _dot(h, w_ref[...].astype(bf16))
        proj_ref[...] = acc.astype(bf16)
        return acc

    @pl.when(j == 0)
    def _():
        h = _norm_mod(_load_x(i, x_refs), g_ref[0], sc_ref[0], sh_ref[0]).astype(bf16)
        h_ref[...] = h
        project(h)

    @pl.when((j == 1) & (i >= NTP))
    def _():
        project(h_ref[...])

    @pl.when((j == 1) & (i < NTP))
    def _():
        acc = project(h_ref[...])
        for ref, col0 in ((kt_ref, NA_WIDTH), (vt_ref, 2 * NA_WIDTH)):
            for s in range(TM // SEQ):
                t = acc[s * SEQ:(s + 1) * SEQ, col0:col0 + NA_WIDTH].T
                for hd in range(NA_HEADS):
                    ref[s, hd] = t[hd * HEAD_DIM:(hd + 1) * HEAD_DIM, :]


def _inproj(x, mod, norm_g, w_in, layer, caches):
    li = layer // 2
    tn = TN_IN
    seqs = TM // SEQ
    xs = x if isinstance(x, tuple) else (x,)
    cache_spec = pl.BlockSpec((seqs, None, NA_HEADS, HEAD_DIM, SEQ),
                              lambda i, j: (jnp.minimum(i, NTP - 1), li, 0, 0, 0))
    cache_shape = jax.ShapeDtypeStruct((BATCH, N_EVEN, NA_HEADS, HEAD_DIM, SEQ), f32)
    aliased = tuple(caches)
    return pl.pallas_call(
        functools.partial(_inproj_kernel, n_x=len(xs), n_alias=len(aliased)),
        grid=(NT, IN_WIDTH // tn),
        in_specs=_x_specs(x) + [pl.BlockSpec(memory_space=pl.ANY)] * len(aliased) + [
            pl.BlockSpec((1, 1, D), lambda i, j: (layer, 0, 0)),
            _mod_spec(layer, 1), _mod_spec(layer, 0),
            pl.BlockSpec((None, D, tn), lambda i, j: (li, 0, j))],
        out_specs=[pl.BlockSpec((TM, tn), lambda i, j: (i, j)), cache_spec, cache_spec],
        out_shape=[jax.ShapeDtypeStruct((T, IN_WIDTH), bf16), cache_shape, cache_shape],
        scratch_shapes=[pltpu.VMEM((TM, D), bf16)],
        input_output_aliases={len(xs) + n: 1 + n for n in range(len(aliased))},
        compiler_params=_cp(("arbitrary", "arbitrary")),
    )(*xs, *aliased, norm_g, mod, mod, w_in)


HEADS_PER_STEP = LANES // HEAD_DIM


class KeyValues(NamedTuple):
    k: jax.Array
    v: jax.Array
    bias: Any = None
    feature_major: bool = False


def _pair_attention(problems):
    first = lax.broadcasted_iota(jnp.int32, (1, LANES), 1) < HEAD_DIM
    mine = (first, jnp.logical_not(first))
    first_t = lax.broadcasted_iota(jnp.int32, (LANES, 1), 0) < HEAD_DIM
    mine_t = (first_t, jnp.logical_not(first_t))
    scores = []
    for q, parts in problems:
        q = q * SCALE
        for hh in range(HEADS_PER_STEP):
            qh = jnp.where(mine[hh], q, 0)
            ss = []
            for part in parts:
                s = _dot(qh, part.k) if part.feature_major else _dot_nt(qh, part.k)
                ss.append(s if part.bias is None else s + part.bias[hh])
            scores.append(ss)
    probs = []
    for ss in scores:
        m = functools.reduce(jnp.maximum, [jnp.max(s, axis=-1, keepdims=True) for s in ss])
        probs.append([jnp.exp((s - m).astype(bf16)) for s in ss])

    def p_times_v(p, part, hh):
        if part.feature_major:
            return _dot_nt(p, jnp.where(mine_t[hh], part.v, 1))
        return _dot(p, jnp.where(mine[hh], part.v, 1))

    results = []
    for n, (_, parts) in enumerate(problems):
        outs = []
        for hh in range(HEADS_PER_STEP):
            pv = functools.reduce(jnp.add, [p_times_v(p, part, hh)
                                            for p, part in zip(probs[n * HEADS_PER_STEP + hh], parts)])
            denom = pv[:, HEAD_DIM:HEAD_DIM + 1] if hh == 0 else pv[:, 0:1]
            outs.append(pv / denom)
        results.append(jnp.where(first, outs[0], outs[1]))
    return results


def _ctx_attn_kernel(q_ref, k_ref, v_ref, o_ref):
    cols = [slice(hp * LANES, (hp + 1) * LANES) for hp in range(NA_HEADS // HEADS_PER_STEP)]
    outs = _pair_attention([(q_ref[:, sl], [KeyValues(k_ref[:, sl], v_ref[:, sl])]) for sl in cols])
    for sl, o in zip(cols, outs):
        o_ref[:, sl] = o.astype(bf16)


def _ctx_attention(proj):
    col = IN_WIDTH // NA_WIDTH - 3
    return pl.pallas_call(
        _ctx_attn_kernel,
        grid=(BATCH,),
        in_specs=[pl.BlockSpec((SEQ, NA_WIDTH), lambda b: (b, col)),
                  pl.BlockSpec((SEQ, NA_WIDTH), lambda b: (b, col + 1)),
                  pl.BlockSpec((SEQ, NA_WIDTH), lambda b: (b, col + 2))],
        out_specs=pl.BlockSpec((SEQ, NA_WIDTH), lambda b: (b, 0)),
        out_shape=jax.ShapeDtypeStruct((T_P, NA_WIDTH), bf16),
        compiler_params=_cp(("arbitrary",)),
    )(proj, proj, proj)


def _win_start(qr):
    return min(max(qr - WIN_ROWS // 2, 0), GRID_ROWS - WIN_ROWS)


N_ROFF = 2 * WIN_ROWS - 1
N_COFF = 2 * WIN_COLS - 1


def _bias_kernel(rpb_ref, o_ref):
    base = pl.program_id(0) * (N_ROFF * N_COFF)
    qc = lax.broadcasted_iota(jnp.int32, (GRID_W, GRID_W), 0)
    kc = lax.broadcasted_iota(jnp.int32, (GRID_W, GRID_W), 1)
    coff = kc - qc + (WIN_COLS - 1)
    cs = jnp.clip(qc - WIN_COLS // 2, 0, GRID_W - WIN_COLS)
    valid = (kc >= cs) & (kc < cs + WIN_COLS)
    neg = jnp.full((GRID_W, GRID_W), NEG_INF, f32)
    blocks = []
    for a in range(N_ROFF):
        t = jnp.zeros((GRID_W, GRID_W), f32)
        for b in range(N_COFF):
            t = jnp.where(coff == b, rpb_ref[base + a * N_COFF + b], t)
        blocks.append(jnp.where(valid, t, neg))
    for c in range(N_CHUNKS):
        for ql in range(CHUNK_ROWS):
            qr = c * CHUNK_ROWS + ql
            st = _win_start(qr)
            for kl in range(KWIN_ROWS):
                kr = KB0[c] + kl
                blk = blocks[kr - qr + WIN_ROWS - 1] if st <= kr < st + WIN_ROWS else neg
                o_ref[c, ql * GRID_W:(ql + 1) * GRID_W, kl * GRID_W:(kl + 1) * GRID_W] = blk


MOD_TN = 6 * D // 4
PREP_STEPS = DEPTH * (6 * D // MOD_TN)
assert PREP_STEPS == N_EVEN * NA_HEADS


def _prep_kernel(cv_ref, w_ref, b_ref, rpb_ref, mod_ref, bias_ref):
    _mod_kernel(cv_ref, w_ref, b_ref, mod_ref)
    _bias_kernel(rpb_ref, bias_ref)


def _prepare(cvecs, ada_w, ada_b, rpb):
    per_layer = 6 * D // MOD_TN
    return pl.pallas_call(
        _prep_kernel,
        grid=(PREP_STEPS,),
        in_specs=[pl.BlockSpec((MOD_ROWS, D), lambda s: (0, 0)),
                  pl.BlockSpec((1, D, MOD_TN), lambda s: (s // per_layer, 0, s % per_layer)),
                  pl.BlockSpec((1, 1, MOD_TN), lambda s: (s // per_layer, 0, s % per_layer)),
                  pl.BlockSpec(memory_space=pltpu.SMEM)],
        out_specs=[pl.BlockSpec((1, MOD_ROWS, MOD_TN), lambda s: (s // per_layer, 0, s % per_layer)),
                   pl.BlockSpec((None, None, N_CHUNKS, CHUNK_Q, KWIN),
                                lambda s: (s // NA_HEADS, s % NA_HEADS, 0, 0, 0))],
        out_shape=[jax.ShapeDtypeStruct((DEPTH, MOD_ROWS, 6 * D), f32),
                   jax.ShapeDtypeStruct((N_EVEN, NA_HEADS, N_CHUNKS, CHUNK_Q, KWIN), f32)],
        compiler_params=_cp(("arbitrary",)),
    )(cvecs, ada_w, ada_b.reshape(DEPTH, 1, 6 * D), rpb.reshape(-1))


def _nbr_attn_kernel(q_ref, k_ref, v_ref, kc_ref, vc_ref, bias_ref, o_ref):
    ctx = KeyValues(kc_ref[...].reshape(LANES, PAST_LEN).astype(bf16),
                    vc_ref[...].reshape(LANES, PAST_LEN).astype(bf16), feature_major=True)
    for c0 in range(0, N_CHUNKS, CHUNKS_PER_GROUP):
        problems = []
        for c in range(c0, c0 + CHUNKS_PER_GROUP):
            rows = slice(c * CHUNK_Q, (c + 1) * CHUNK_Q)
            n_keys = KW_ROWS[c] * GRID_W
            win = slice(KB0[c] * GRID_W, KB0[c] * GRID_W + n_keys)
            bias = [bias_ref[hh, c, :, :n_keys] for hh in range(HEADS_PER_STEP)]
            problems.append((q_ref[rows, :], [KeyValues(k_ref[win, :], v_ref[win, :], bias), ctx]))
        for c, o in zip(range(c0, c0 + CHUNKS_PER_GROUP), _pair_attention(problems)):
            o_ref[c * CHUNK_Q:(c + 1) * CHUNK_Q, :] = o.astype(bf16)


def _nbr_attention(proj, cache_k, cache_v, bias, li):
    qcol = 3 * CONV_CH // LANES
    ncol = NA_WIDTH // LANES
    hp_steps = NA_HEADS // HEADS_PER_STEP

    def col_spec(which):
        return pl.BlockSpec((DEC_SEQ, LANES), lambda hp, b: (NTP + b, qcol + which * ncol + hp))

    ctx_spec = pl.BlockSpec((None, None, HEADS_PER_STEP, HEAD_DIM, PAST_LEN),
                            lambda hp, b: (b, li, hp, 0, 0))
    return pl.pallas_call(
        _nbr_attn_kernel,
        grid=(hp_steps, DEC_BATCH),
        in_specs=[col_spec(0), col_spec(1), col_spec(2), ctx_spec, ctx_spec,
                  pl.BlockSpec((None, HEADS_PER_STEP, N_CHUNKS, CHUNK_Q, KWIN),
                               lambda hp, b: (li, hp, 0, 0, 0))],
        out_specs=pl.BlockSpec((DEC_SEQ, LANES), lambda hp, b: (b, hp)),
        out_shape=jax.ShapeDtypeStruct((T_S, NA_WIDTH), bf16),
        compiler_params=_cp(("arbitrary", "arbitrary")),
    )(proj, proj, proj, cache_k, cache_v, bias)


def _mixout_kernel(*refs, n_x):
    x_refs, (a_ref, ybp_ref, ybs_ref, g1_ref, cw_ref, w_ref, o_ref) = refs[:n_x], refs[n_x:]
    t = pl.program_id(0)
    a_b = a_ref[:, 0:CONV_CH].astype(f32)
    a_c = a_ref[:, CONV_CH:2 * CONV_CH].astype(f32)
    a_x = a_ref[:, 2 * CONV_CH:3 * CONV_CH].astype(f32)
    u = a_c * a_x
    r = lax.broadcasted_iota(jnp.int32, (TM, 1), 0)
    pos = jnp.where(t < NTP, r % SEQ, r)
    last = jnp.where(t < NTP, SEQ - 1, DEC_SEQ - 1)
    u_prev = jnp.where(pos == 0, 0.0, pltpu.roll(u, 1, axis=0))
    u_next = jnp.where(pos == last, 0.0, pltpu.roll(u, TM - 1, axis=0))
    y_a = a_b * (u_prev * cw_ref[0:1, :] + u * cw_ref[1:2, :] + u_next * cw_ref[2:3, :])
    y_b = jnp.where(t < NTP, ybp_ref[...], ybs_ref[...])
    y = (_dot(y_a.astype(bf16), w_ref[0:CONV_CH, :].astype(bf16))
         + _dot(y_b, w_ref[CONV_CH:, :].astype(bf16)))
    o_ref[...] = _load_x(t, x_refs) + g1_ref[0] * y


def _mixout(proj, yb_p, yb_s, x, mod, conv_w, w_out, layer):
    li = layer // 2
    xs = x if isinstance(x, tuple) else (x,)
    return pl.pallas_call(
        functools.partial(_mixout_kernel, n_x=len(xs)),
        grid=(NT,),
        in_specs=_x_specs(x) + [
            pl.BlockSpec((TM, 3 * CONV_CH), lambda t: (t, 0)),
            pl.BlockSpec((TM, NA_WIDTH), lambda t: (jnp.minimum(t, NTP - 1), 0)),
            pl.BlockSpec((TM, NA_WIDTH), lambda t: (jnp.maximum(t - NTP, 0), 0)),
            _mod_spec(layer, 2),
            pl.BlockSpec((None, 3, CONV_CH), lambda t: (li, 0, 0)),
            pl.BlockSpec((None, D, D), lambda t: (li, 0, 0))],
        out_specs=pl.BlockSpec((TM, D), lambda t: (t, 0)),
        out_shape=jax.ShapeDtypeStruct((T, D), f32),
        compiler_params=_cp(("arbitrary",)),
    )(*xs, proj, yb_p, yb_s, mod, conv_w, w_out)


FFN_TILES = 2


def _ffn_kernel(x_ref, g_ref, *refs):
    mods, (w1_ref, w3_ref, w2_ref, o_ref, h_ref) = refs[:3 * FFN_TILES], refs[3 * FFN_TILES:]
    j = pl.program_id(1)
    last = pl.num_programs(1) - 1
    parts = [(slice(n * TM, (n + 1) * TM), mods[3 * n:3 * n + 3]) for n in range(FFN_TILES)]

    def hidden_step(h):
        gate = jax.nn.silu(_dot(h, w1_ref[...].astype(bf16))) * _dot(h, w3_ref[...].astype(bf16))
        return _dot(gate.astype(bf16), w2_ref[...].astype(bf16))

    @pl.when(j == 0)
    def _():
        h = jnp.concatenate([_norm_mod(x_ref[rows, :], g_ref[0], sc_ref[0], sh_ref[0]).astype(bf16)
                             for rows, (sc_ref, sh_ref, _) in parts], axis=0)
        h_ref[...] = h
        o_ref[...] = hidden_step(h)

    @pl.when((j > 0) & (j < last))
    def _():
        o_ref[...] += hidden_step(h_ref[...])

    @pl.when(j == last)
    def _():
        acc = o_ref[...] + hidden_step(h_ref[...])
        for rows, (_, _, g2_ref) in parts:
            o_ref[rows, :] = x_ref[rows, :] + g2_ref[0] * acc[rows, :]


def _dense_ffn(x, mod, norm_g, w1, w3, w2, layer):
    li = layer // 2
    tf = TF_DENSE
    rows = FFN_TILES * TM

    def mod_spec(part, n):
        return pl.BlockSpec((1, 1, D), lambda i, j: (
            (layer * MOD_ROWS + _mod_index(i * FFN_TILES + n)) * 6 + part, 0, 0))

    mod_specs = [mod_spec(part, n) for n in range(FFN_TILES) for part in (4, 3, 5)]
    return pl.pallas_call(
        _ffn_kernel,
        grid=(T // rows, FFN_DENSE // tf),
        in_specs=[pl.BlockSpec((rows, D), lambda i, j: (i, 0)),
                  pl.BlockSpec((1, 1, D), lambda i, j: (layer, 0, 0))] + mod_specs + [
                  pl.BlockSpec((None, D, tf), lambda i, j: (li, 0, j)),
                  pl.BlockSpec((None, D, tf), lambda i, j: (li, 0, j)),
                  pl.BlockSpec((None, tf, D), lambda i, j: (li, j, 0))],
        out_specs=pl.BlockSpec((rows, D), lambda i, j: (i, 0)),
        out_shape=jax.ShapeDtypeStruct((T, D), f32),
        scratch_shapes=[pltpu.VMEM((rows, D), bf16)],
        compiler_params=_cp(("arbitrary", "arbitrary")),
    )(x, norm_g, *([mod] * len(mod_specs)), w1, w3, w2)


GROUP_CH = D // FOURIER_GROUPS


def _dft_mats(n):
    k = np.arange(n, dtype=np.int64)
    ang = 2.0 * np.pi * ((k[:, None] * k[None, :]) % n).astype(np.float64) / n
    return np.cos(ang).astype(np.float32), np.sin(ang).astype(np.float32)


def _fourier_kernel(x_ref, g_ref, sc_ref, sh_ref, g1_ref, cs_ref, ss_ref, cl_ref, sl_ref, wf_ref,
                    o_ref, f_ref):
    t = pl.program_id(0)
    h = _norm_mod(x_ref[...], g_ref[0], sc_ref[0], sh_ref[0]).astype(bf16)
    cs = cs_ref[...].astype(bf16)
    ss = ss_ref[...].astype(bf16)
    ys, zs = [], []
    for g in range(FOURIER_GROUPS):
        hg = h[:, g * GROUP_CH:(g + 1) * GROUP_CH]
        ys.append(_dot(hg, cs))
        zs.append(_dot(hg, ss))
    y = jnp.concatenate(ys, axis=-1).astype(bf16)
    z = jnp.concatenate(zs, axis=-1).astype(bf16)

    @pl.when(t < NTP)
    def _():
        for s in range(TM // SEQ):
            rows = slice(s * SEQ, (s + 1) * SEQ)
            f = _dot(cs, y[rows]) - _dot(ss, z[rows])
            f_ref[rows, :] = (f * ((SEQ * GROUP_CH) ** -0.5)).astype(bf16)

    @pl.when(t >= NTP)
    def _():
        f = _dot(cl_ref[...].astype(bf16), y) - _dot(sl_ref[...].astype(bf16), z)
        f_ref[...] = (f * ((DEC_SEQ * GROUP_CH) ** -0.5)).astype(bf16)

    o_ref[...] = x_ref[...] + g1_ref[0] * _dot(f_ref[...], wf_ref[...].astype(bf16))


def _fourier(x, mod, norm_g, fourier_w, layer):
    li = layer // 2
    c_s, s_s = _dft_mats(SEQ)
    c_l, s_l = _dft_mats(DEC_SEQ)
    const = lambda shape: pl.BlockSpec(shape, lambda t: (0, 0))
    return pl.pallas_call(
        _fourier_kernel,
        grid=(NT,),
        in_specs=[pl.BlockSpec((TM, D), lambda t: (t, 0)),
                  pl.BlockSpec((1, 1, D), lambda t: (layer, 0, 0)),
                  _mod_spec(layer, 1), _mod_spec(layer, 0), _mod_spec(layer, 2),
                  const((SEQ, SEQ)), const((SEQ, SEQ)),
                  const((DEC_SEQ, DEC_SEQ)), const((DEC_SEQ, DEC_SEQ)),
                  pl.BlockSpec((None, D, D), lambda t: (li, 0, 0))],
        out_specs=pl.BlockSpec((TM, D), lambda t: (t, 0)),
        out_shape=jax.ShapeDtypeStruct((T, D), f32),
        scratch_shapes=[pltpu.VMEM((TM, D), bf16)],
        compiler_params=_cp(("arbitrary",)),
    )(x, norm_g, mod, mod, mod, jnp.asarray(c_s), jnp.asarray(s_s), jnp.asarray(c_l), jnp.asarray(s_l),
      fourier_w)


M_E0, M_E1, M_R0, M_R1, M_W0, M_W1 = range(6)
ROW_UNROLL = 8


META_ROWS = 8


def _router_kernel(x_ref, g_ref, sc_ref, sh_ref, rw_ref, rb_ref, before_ref, h_ref, meta_ref, meta_t_ref,
                   cnt_ref, carry_ref):
    t = pl.program_id(0)

    @pl.when(t == 0)
    def _():
        carry_ref[...] = jnp.zeros_like(carry_ref)

    h = _norm_mod(x_ref[...], g_ref[0], sc_ref[0], sh_ref[0])
    h_ref[...] = h
    h_hi = h.astype(bf16)
    h_lo = (h - h_hi.astype(f32)).astype(bf16)
    hi_terms = _dot(h_hi, rw_ref[...])
    lo_term = _dot(h_lo, rw_ref[:, :LANES])
    logits = ((hi_terms[:, :LANES] + lo_term) + hi_terms[:, LANES:]) + rb_ref[...]
    lane = lax.broadcasted_iota(jnp.int32, (TM, LANES), 1)
    m1 = jnp.max(logits, axis=-1, keepdims=True)
    i1 = jnp.min(jnp.where(logits == m1, lane, LANES), axis=-1, keepdims=True)
    rest = jnp.where(lane == i1, -jnp.inf, logits)
    m2 = jnp.max(rest, axis=-1, keepdims=True)
    i2 = jnp.min(jnp.where(rest == m2, lane, LANES), axis=-1, keepdims=True)
    e = jnp.exp(m2 - m1)
    w0 = 1.0 / (1.0 + e)
    w1 = e / (1.0 + e)
    oh0 = (lane == i1).astype(f32)
    oh1 = (lane == i2).astype(f32)
    oh = oh0 + oh1
    base = carry_ref[...] + _dot(before_ref[...], oh.astype(bf16))
    r0 = jnp.sum(oh0 * base, axis=-1, keepdims=True)
    r1 = jnp.sum(oh1 * base, axis=-1, keepdims=True)
    carry_ref[...] += jnp.sum(oh, axis=0, keepdims=True)
    rec = jnp.zeros((TM, LANES), f32)
    for idx, val in ((M_E0, i1.astype(f32)), (M_E1, i2.astype(f32)), (M_R0, r0), (M_R1, r1),
                     (M_W0, w0), (M_W1, w1)):
        rec = jnp.where(lane == idx, val, rec)
    meta_ref[...] = rec
    meta_t_ref[...] = rec.T[:META_ROWS, :]
    cnt_ref[...] = carry_ref[...]


def _router(x, mod, norm_g, router_w, router_b, layer):
    li = layer // 2
    rw = jnp.pad(router_w[li], ((0, 0), (0, LANES - N_EXPERTS)))
    rw_hi = rw.astype(bf16)
    rw_lo = (rw - rw_hi.astype(f32)).astype(bf16)
    rw = jnp.concatenate([rw_hi, rw_lo], axis=1)
    rb = jnp.pad(router_b[li], (0, LANES - N_EXPERTS), constant_values=NEG_INF).reshape(1, LANES)
    before = jnp.asarray(np.tril(np.ones((TM, TM), np.float32), -1), dtype=bf16)
    return pl.pallas_call(
        _router_kernel,
        grid=(NT,),
        in_specs=[pl.BlockSpec((TM, D), lambda t: (t, 0)),
                  pl.BlockSpec((1, 1, D), lambda t: (layer, 0, 0)),
                  _mod_spec(layer, 4), _mod_spec(layer, 3),
                  pl.BlockSpec((D, 2 * LANES), lambda t: (0, 0)),
                  pl.BlockSpec((1, LANES), lambda t: (0, 0)),
                  pl.BlockSpec((TM, TM), lambda t: (0, 0))],
        out_specs=[pl.BlockSpec((TM, D), lambda t: (t, 0)),
                   pl.BlockSpec((TM, LANES), lambda t: (t, 0)),
                   pl.BlockSpec((META_ROWS, TM), lambda t: (0, t)),
                   pl.BlockSpec((1, LANES), lambda t: (0, 0))],
        out_shape=[jax.ShapeDtypeStruct((T, D), f32),
                   jax.ShapeDtypeStruct((T, LANES), f32),
                   jax.ShapeDtypeStruct((META_ROWS, T), f32),
                   jax.ShapeDtypeStruct((1, LANES), f32)],
        scratch_shapes=[pltpu.VMEM((1, LANES), f32)],
        compiler_params=_cp(("arbitrary",)),
    )(x, norm_g, mod, mod, rw, rb, before)


def _row_copy(src_ref, src_row, dst_ref, dst_row, sem):
    return pltpu.make_async_copy(src_ref.at[pl.ds(src_row, 1)], dst_ref.at[pl.ds(dst_row, 1)], sem)


ZERO_BLOCK = 8


def _dispatch_kernel(lo_ref, hi_ref, nu_ref, dest_ref, h_ref, xg_ref, zero_ref, sem, zsem):
    @pl.when(pl.program_id(0) == 0)
    def _():
        zero_ref[...] = jnp.zeros_like(zero_ref)

        def row_zero(r):
            return _row_copy(zero_ref, 0, xg_ref, r, zsem)

        def block_zero(b):
            return pltpu.make_async_copy(zero_ref.at[pl.ds(0, ZERO_BLOCK)],
                                         xg_ref.at[pl.ds(pl.multiple_of(b * ZERO_BLOCK, ZERO_BLOCK), ZERO_BLOCK)],
                                         zsem)

        def tile_zero(tile):
            return pltpu.make_async_copy(zero_ref, xg_ref.at[pl.ds(pl.multiple_of(tile * TM_E, TM_E), TM_E)],
                                         zsem)

        def each(lo, hi, copy, wait):
            def step(i, c):
                if wait:
                    copy(i).wait()
                else:
                    copy(i).start()
                return c
            lax.fori_loop(lo, hi, step, 0)

        for wait in (False, True):
            for e in range(N_EXPERTS):
                lo, hi = lo_ref[e], hi_ref[e]
                aligned = jnp.minimum((lo + ZERO_BLOCK - 1) // ZERO_BLOCK * ZERO_BLOCK, hi)
                each(lo, aligned, row_zero, wait)
                each(aligned // ZERO_BLOCK, hi // ZERO_BLOCK, block_zero, wait)
            each(nu_ref[0], NT_E, tile_zero, wait)

    def issue(g, c):
        for u in range(ROW_UNROLL):
            r = g * ROW_UNROLL + u
            for k in range(2):
                _row_copy(h_ref, r, xg_ref, dest_ref[0, 0, k * TM + r], sem).start()
        return c

    lax.fori_loop(0, TM // ROW_UNROLL, issue, 0)
    for k in range(2):
        pltpu.make_async_copy(h_ref, xg_ref.at[pl.ds(0, TM)], sem).wait()


def _dispatch(pad_lo, pad_hi, n_used, dest, h):
    grid_spec = pltpu.PrefetchScalarGridSpec(
        num_scalar_prefetch=3,
        grid=(NT,),
        in_specs=[pl.BlockSpec((1, 1, 2 * TM), lambda t, *_: (t, 0, 0), memory_space=pltpu.SMEM),
                  pl.BlockSpec((TM, D), lambda t, *_: (t, 0))],
        out_specs=pl.BlockSpec(memory_space=pl.ANY),
        scratch_shapes=[pltpu.VMEM((TM_E, D), f32), pltpu.SemaphoreType.DMA(()), pltpu.SemaphoreType.DMA(())],
    )
    return pl.pallas_call(
        _dispatch_kernel,
        grid_spec=grid_spec,
        out_shape=jax.ShapeDtypeStruct((R_E, D), f32),
        compiler_params=_cp(("arbitrary",)),
    )(pad_lo, pad_hi, n_used, dest, h)


NJ_EXP = FFN_EXPERT // TF_EXP
ROW_CLASSES = tuple(TM_E * q // 4 for q in (1, 2, 3, 4))


def _expert_kernel(te_ref, nu_ref, rows_ref, xg_ref, w1_ref, w3_ref, w2_ref, y_ref, h_ref):
    t = pl.program_id(0)
    j = pl.program_id(1)
    n_rows = rows_ref[t]

    def hidden_step(h):
        gate = jax.nn.silu(_dot(h, w1_ref[...].astype(bf16))) * _dot(h, w3_ref[...].astype(bf16))
        return _dot(gate.astype(bf16), w2_ref[...].astype(bf16))

    @pl.when((j == 0) & (n_rows == 0))
    def _():
        y_ref[...] = jnp.zeros_like(y_ref)

    for below, m in zip((0,) + ROW_CLASSES, ROW_CLASSES):
        in_class = (n_rows > below) & (n_rows <= m)

        @pl.when(in_class & (j == 0))
        def _():
            h = xg_ref[:m, :].astype(bf16)
            h_ref[:m, :] = h
            y_ref[:m, :] = hidden_step(h)
            if m < TM_E:
                y_ref[m:, :] = jnp.zeros((TM_E - m, D), f32)

        @pl.when(in_class & (j > 0))
        def _():
            y_ref[:m, :] += hidden_step(h_ref[:m, :])


def _experts(tile_expert, n_used, tile_rows, xg, w1, w3, w2, layer):
    li = layer // 2
    tf = TF_EXP
    nj = NJ_EXP

    def jj(t, j, nu):
        return jnp.where(t < nu[0], j, nj - 1)

    def tt(t, nu):
        return jnp.minimum(t, jnp.maximum(nu[0] - 1, 0))

    grid_spec = pltpu.PrefetchScalarGridSpec(
        num_scalar_prefetch=3,
        grid=(NT_E, nj),
        in_specs=[pl.BlockSpec((TM_E, D), lambda t, j, te, nu, nr: (tt(t, nu), 0)),
                  pl.BlockSpec((None, None, D, tf), lambda t, j, te, nu, nr: (li, te[t], 0, jj(t, j, nu))),
                  pl.BlockSpec((None, None, D, tf), lambda t, j, te, nu, nr: (li, te[t], 0, jj(t, j, nu))),
                  pl.BlockSpec((None, None, tf, D), lambda t, j, te, nu, nr: (li, te[t], jj(t, j, nu), 0))],
        out_specs=pl.BlockSpec((TM_E, D), lambda t, j, te, nu, nr: (t, 0)),
        scratch_shapes=[pltpu.VMEM((TM_E, D), bf16)],
    )
    return pl.pallas_call(
        _expert_kernel,
        grid_spec=grid_spec,
        out_shape=jax.ShapeDtypeStruct((R_E, D), f32),
        compiler_params=_cp(("arbitrary", "arbitrary")),
    )(tile_expert, n_used, tile_rows, xg, w1, w3, w2)


def _combine_kernel(dest_ref, dest_next_ref, x_ref, g2_ref, meta_ref, y_ref, *rest, final):
    if final:
        fg_ref, op_ref, os_ref, buf_ref, sems = rest
    else:
        o_ref, buf_ref, sems = rest
    t = pl.program_id(0)
    slot = t % 2

    def gather(rows_ref, s):
        def issue(g, c):
            for u in range(ROW_UNROLL):
                r = g * ROW_UNROLL + u
                for k in range(2):
                    _row_copy(y_ref, rows_ref[0, 0, k * TM + r], buf_ref.at[s, k], r, sems.at[s]).start()
            return c
        lax.fori_loop(0, TM // ROW_UNROLL, issue, 0)

    @pl.when(t == 0)
    def _():
        gather(dest_ref, 0)

    @pl.when(t + 1 < NT)
    def _():
        gather(dest_next_ref, 1 - slot)

    for k in range(2):
        pltpu.make_async_copy(y_ref.at[pl.ds(0, TM)], buf_ref.at[slot, k], sems.at[slot]).wait()
    w0 = meta_ref[:, M_W0:M_W0 + 1]
    w1 = meta_ref[:, M_W1:M_W1 + 1]
    out = x_ref[...] + g2_ref[0] * (w0 * buf_ref[slot, 0] + w1 * buf_ref[slot, 1])
    if not final:
        o_ref[...] = out
        return
    normed = (out * lax.rsqrt(jnp.mean(out * out, axis=-1, keepdims=True) + RMS_EPS)) * fg_ref[...]

    @pl.when(t < NTP)
    def _():
        op_ref[...] = normed

    @pl.when(t >= NTP)
    def _():
        os_ref[...] = normed


def _combine(dest, x, mod, meta, y, layer, final_g=None):
    final = final_g is not None
    in_specs = [pl.BlockSpec((1, 1, 2 * TM), lambda t: (t, 0, 0), memory_space=pltpu.SMEM),
                pl.BlockSpec((1, 1, 2 * TM), lambda t: (jnp.minimum(t + 1, NT - 1), 0, 0),
                             memory_space=pltpu.SMEM),
                pl.BlockSpec((TM, D), lambda t: (t, 0)),
                _mod_spec(layer, 5),
                pl.BlockSpec((TM, LANES), lambda t: (t, 0)),
                pl.BlockSpec(memory_space=pl.ANY)]
    args = [dest, dest, x, mod, meta, y]
    if final:
        in_specs.append(pl.BlockSpec((1, D), lambda t: (0, 0)))
        args.append(final_g.reshape(1, D))
        out_specs = [pl.BlockSpec((TM, D), lambda t: (jnp.minimum(t, NTP - 1), 0)),
                     pl.BlockSpec((TM, D), lambda t: (jnp.maximum(t - NTP, 0), 0))]
        out_shape = [jax.ShapeDtypeStruct((T_P, D), f32), jax.ShapeDtypeStruct((T_S, D), f32)]
    else:
        out_specs = pl.BlockSpec((TM, D), lambda t: (t, 0))
        out_shape = jax.ShapeDtypeStruct((T, D), f32)
    return pl.pallas_call(
        functools.partial(_combine_kernel, final=final),
        grid=(NT,),
        in_specs=in_specs,
        out_specs=out_specs,
        out_shape=out_shape,
        scratch_shapes=[pltpu.VMEM((2, 2, TM, D), f32), pltpu.SemaphoreType.DMA((2,))],
        compiler_params=_cp(("arbitrary",)),
    )(*args)


def _moe(x, mod, norm_g, router_w, router_b, w1, w3, w2, layer, final_g=None):
    h_rows, meta, meta_t, counts = _router(x, mod, norm_g, router_w, router_b, layer)
    cnt = counts[0, :N_EXPERTS].astype(jnp.int32)
    padded = ((cnt + TM_E - 1) // TM_E) * TM_E
    ends = jnp.cumsum(padded)
    starts = ends - padded
    experts = meta_t[M_E0:M_E1 + 1].astype(jnp.int32)
    ranks = meta_t[M_R0:M_R1 + 1].astype(jnp.int32)
    start_of = functools.reduce(lambda acc, e: jnp.where(experts == e, starts[e], acc), range(N_EXPERTS), 0)
    dest = start_of + ranks
    dest = dest.reshape(2, NT, TM).transpose(1, 0, 2).reshape(NT, 1, 2 * TM)
    n_used = (ends[-1] // TM_E).astype(jnp.int32).reshape(1)
    tile_start = jnp.minimum(jnp.arange(NT_E, dtype=jnp.int32), n_used[0] - 1) * TM_E
    tile_expert = jnp.sum((tile_start[:, None] >= ends[None, :]).astype(jnp.int32), axis=1)
    tile_expert = jnp.minimum(tile_expert, N_EXPERTS - 1).astype(jnp.int32)
    tile_ids = jnp.arange(NT_E, dtype=jnp.int32)
    tile_rows = jnp.clip((starts + cnt)[tile_expert] - tile_ids * TM_E, 0, TM_E)
    tile_rows = jnp.where(tile_ids < n_used[0], tile_rows, 0).astype(jnp.int32)
    xg = _dispatch((starts + cnt).astype(jnp.int32), ends.astype(jnp.int32), n_used, dest, h_rows)
    y = _experts(tile_expert, n_used, tile_rows, xg, w1, w3, w2, layer)
    return _combine(dest, x, mod, meta, y, layer, final_g)


def kernel(x_prompt, x_sample, c, cache_k, cache_v, c_ctx, ada_w, ada_b, norm1_g, norm2_g, w_in, conv_w, rpb,
           w_out, ffn_w1, ffn_w3, ffn_w2, fourier_w, router_w, router_b, moe_w1, moe_w3, moe_w2, final_g):
    cvecs = jnp.concatenate([c_ctx[None, :], c, jnp.zeros((MOD_ROWS - 1 - DEC_BATCH, D), f32)], axis=0)
    mod, bias = _prepare(cvecs, ada_w, ada_b, rpb)
    mod = mod.reshape(DEPTH * MOD_ROWS * 6, 1, D)
    n1 = norm1_g.reshape(DEPTH, 1, D)
    n2 = norm2_g.reshape(DEPTH, 1, D)
    cache_kt = jnp.swapaxes(cache_k, -1, -2)
    cache_vt = jnp.swapaxes(cache_v, -1, -2)
    x = (x_prompt.reshape(T_P, D), x_sample.reshape(T_S, D))
    caches = [jnp.zeros((BATCH, N_EVEN, NA_HEADS, HEAD_DIM, SEQ), f32) for _ in range(2)]
    for layer in range(DEPTH):
        li = layer // 2
        if layer % 2 == 0:
            proj, *caches = _inproj(x, mod, n1, w_in, layer, caches)
            yb_p = _ctx_attention(proj)
            yb_s = _nbr_attention(proj, cache_kt, cache_vt, bias, li)
            x = _mixout(proj, yb_p, yb_s, x, mod, conv_w, w_out, layer)
            x = _dense_ffn(x, mod, n2, ffn_w1, ffn_w3, ffn_w2, layer)
        else:
            x = _fourier(x, mod, n1, fourier_w, layer)
            last = layer == DEPTH - 1
            x = _moe(x, mod, n2, router_w, router_b, moe_w1, moe_w3, moe_w2, layer, final_g if last else None)
    y_prompt, y_sample = x
    new_kt, new_vt = caches
    return (y_prompt.reshape(BATCH, SEQ, D), y_sample.reshape(DEC_BATCH, DEC_SEQ, D),
            jnp.swapaxes(new_kt, -1, -2), jnp.swapaxes(new_vt, -1, -2))
```

```python
import functools
from typing import Any, NamedTuple

import numpy as np
import jax
import jax.numpy as jnp
from jax import lax
from jax.experimental import pallas as pl
from jax.experimental.pallas import tpu as pltpu

f32 = jnp.float32
bf16 = jnp.bfloat16

D = 1024
BATCH = 16
SEQ = 256
DEPTH = 4
DEC_BATCH = 8
DEC_SEQ = 1024
PAST_LEN = 512
GRID_W = 64
CONV_CH = 512
NA_HEADS = 8
HEAD_DIM = 64
NA_WIDTH = 512
WIN_ROWS = 8
WIN_COLS = 16
IN_WIDTH = 3072
FOURIER_GROUPS = 4
FFN_DENSE = 2816
N_EXPERTS = 8
FFN_EXPERT = 3584
N_EVEN = 2
RMS_EPS = 1e-6
NEG_INF = -1e30
SCALE = HEAD_DIM ** -0.5

LANES = 128
TM = 1024
T_P = BATCH * SEQ
T_S = DEC_BATCH * DEC_SEQ
T = T_P + T_S
NT = T // TM
NTP = T_P // TM
MOD_ROWS = 16
GRID_ROWS = DEC_SEQ // GRID_W
CHUNK_ROWS = 4
N_CHUNKS = GRID_ROWS // CHUNK_ROWS
CHUNK_Q = CHUNK_ROWS * GRID_W
KWIN_ROWS = 12
KWIN = KWIN_ROWS * GRID_W
KB0 = (0, 0, 4, 8)
KW_ROWS = (8, 12, 12, 8)
CHUNKS_PER_GROUP = 4
TF_DENSE = 256
TF_EXP = 512
TM_E = 1024
N_ASSIGN = 2 * T
NT_E = N_ASSIGN // TM_E + N_EXPERTS
R_E = NT_E * TM_E
VMEM_LIMIT = 56 * 1024 * 1024


def _cp(sem, vmem=VMEM_LIMIT):
    return pltpu.CompilerParams(dimension_semantics=sem, vmem_limit_bytes=vmem)


def _mod_index(t):
    return jnp.where(t < NTP, 0, t - (NTP - 1))


def _mod_spec(layer, part):
    def index(t, *_):
        return ((layer * MOD_ROWS + _mod_index(t)) * 6 + part, 0, 0)
    return pl.BlockSpec((1, 1, D), index)


def _norm_mod(x, g, sc, sh):
    y = x * lax.rsqrt(jnp.mean(x * x, axis=-1, keepdims=True) + RMS_EPS)
    return (y * g) * (1 + sc) + sh


def _dot(a, b):
    return jnp.dot(a, b, preferred_element_type=f32)


def _dot_nt(a, b):
    return lax.dot_general(a, b, (((1,), (1,)), ((), ())), preferred_element_type=f32)


def _mod_kernel(cv_ref, w_ref, b_ref, o_ref):
    s = jax.nn.silu(cv_ref[...]).astype(bf16)
    o_ref[0] = _dot(s, w_ref[0].astype(bf16)) + b_ref[0]


TN_IN = IN_WIDTH // 2


def _x_specs(x):
    if isinstance(x, tuple):
        return [pl.BlockSpec((TM, D), lambda i, *_: (jnp.minimum(i, NTP - 1), 0)),
                pl.BlockSpec((TM, D), lambda i, *_: (jnp.maximum(i - NTP, 0), 0))]
    return [pl.BlockSpec((TM, D), lambda i, *_: (i, 0))]


def _load_x(i, x_refs):
    if len(x_refs) == 2:
        return jnp.where(i < NTP, x_refs[0][...], x_refs[1][...])
    return x_refs[0][...]


def _inproj_kernel(*refs, n_x, n_alias):
    x_refs, refs = refs[:n_x], refs[n_x + n_alias:]
    g_ref, sc_ref, sh_ref, w_ref, proj_ref, kt_ref, vt_ref, h_ref = refs
    i = pl.program_id(0)
    j = pl.program_id(1)

    def project(h):
        acc = _dot(h, w_ref[...].astype(bf16))
        proj_ref[...] = acc.astype(bf16)
        return acc

    @pl.when(j == 0)
    def _():
        h = _norm_mod(_load_x(i, x_refs), g_ref[0], sc_ref[0], sh_ref[0]).astype(bf16)
        h_ref[...] = h
        project(h)

    @pl.when((j == 1) & (i >= NTP))
    def _():
        project(h_ref[...])

    @pl.when((j == 1) & (i < NTP))
    def _():
        acc = project(h_ref[...])
        for ref, col0 in ((kt_ref, NA_WIDTH), (vt_ref, 2 * NA_WIDTH)):
            for s in range(TM // SEQ):
                t = acc[s * SEQ:(s + 1) * SEQ, col0:col0 + NA_WIDTH].T
                for hd in range(NA_HEADS):
                    ref[s, hd] = t[hd * HEAD_DIM:(hd + 1) * HEAD_DIM, :]


def _inproj(x, mod, norm_g, w_in, layer, caches):
    li = layer // 2
    tn = TN_IN
    seqs = TM // SEQ
    xs = x if isinstance(x, tuple) else (x,)
    cache_spec = pl.BlockSpec((seqs, None, NA_HEADS, HEAD_DIM, SEQ),
                              lambda i, j: (jnp.minimum(i, NTP - 1), li, 0, 0, 0))
    cache_shape = jax.ShapeDtypeStruct((BATCH, N_EVEN, NA_HEADS, HEAD_DIM, SEQ), f32)
    aliased = tuple(caches)
    return pl.pallas_call(
        functools.partial(_inproj_kernel, n_x=len(xs), n_alias=len(aliased)),
        grid=(NT, IN_WIDTH // tn),
        in_specs=_x_specs(x) + [pl.BlockSpec(memory_space=pl.ANY)] * len(aliased) + [
            pl.BlockSpec((1, 1, D), lambda i, j: (layer, 0, 0)),
            _mod_spec(layer, 1), _mod_spec(layer, 0),
            pl.BlockSpec((None, D, tn), lambda i, j: (li, 0, j))],
        out_specs=[pl.BlockSpec((TM, tn), lambda i, j: (i, j)), cache_spec, cache_spec],
        out_shape=[jax.ShapeDtypeStruct((T, IN_WIDTH), bf16), cache_shape, cache_shape],
        scratch_shapes=[pltpu.VMEM((TM, D), bf16)],
        input_output_aliases={len(xs) + n: 1 + n for n in range(len(aliased))},
        compiler_params=_cp(("arbitrary", "arbitrary")),
    )(*xs, *aliased, norm_g, mod, mod, w_in)


HEADS_PER_STEP = LANES // HEAD_DIM


class KeyValues(NamedTuple):
    k: jax.Array
    v: jax.Array
    bias: Any = None
    feature_major: bool = False


def _pair_attention(problems):
    first = lax.broadcasted_iota(jnp.int32, (1, LANES), 1) < HEAD_DIM
    mine = (first, jnp.logical_not(first))
    first_t = lax.broadcasted_iota(jnp.int32, (LANES, 1), 0) < HEAD_DIM
    mine_t = (first_t, jnp.logical_not(first_t))
    scores = []
    for q, parts in problems:
        q = q * SCALE
        for hh in range(HEADS_PER_STEP):
            qh = jnp.where(mine[hh], q, 0)
            ss = []
            for part in parts:
                s = _dot(qh, part.k) if part.feature_major else _dot_nt(qh, part.k)
                ss.append(s if part.bias is None else s + part.bias[hh])
            scores.append(ss)
    probs = []
    for ss in scores:
        m = functools.reduce(jnp.maximum, [jnp.max(s, axis=-1, keepdims=True) for s in ss])
        probs.append([jnp.exp((s - m).astype(bf16)) for s in ss])

    def p_times_v(p, part, hh):
        if part.feature_major:
            return _dot_nt(p, jnp.where(mine_t[hh], part.v, 1))
        return _dot(p, jnp.where(mine[hh], part.v, 1))

    results = []
    for n, (_, parts) in enumerate(problems):
        outs = []
        for hh in range(HEADS_PER_STEP):
            pv = functools.reduce(jnp.add, [p_times_v(p, part, hh)
                                            for p, part in zip(probs[n * HEADS_PER_STEP + hh], parts)])
            denom = pv[:, HEAD_DIM:HEAD_DIM + 1] if hh == 0 else pv[:, 0:1]
            outs.append(pv / denom)
        results.append(jnp.where(first, outs[0], outs[1]))
    return results


def _ctx_attn_kernel(q_ref, k_ref, v_ref, o_ref):
    cols = [slice(hp * LANES, (hp + 1) * LANES) for hp in range(NA_HEADS // HEADS_PER_STEP)]
    outs = _pair_attention([(q_ref[:, sl], [KeyValues(k_ref[:, sl], v_ref[:, sl])]) for sl in cols])
    for sl, o in zip(cols, outs):
        o_ref[:, sl] = o.astype(bf16)


def _ctx_attention(proj):
    col = IN_WIDTH // NA_WIDTH - 3
    return pl.pallas_call(
        _ctx_attn_kernel,
        grid=(BATCH,),
        in_specs=[pl.BlockSpec((SEQ, NA_WIDTH), lambda b: (b, col)),
                  pl.BlockSpec((SEQ, NA_WIDTH), lambda b: (b, col + 1)),
                  pl.BlockSpec((SEQ, NA_WIDTH), lambda b: (b, col + 2))],
        out_specs=pl.BlockSpec((SEQ, NA_WIDTH), lambda b: (b, 0)),
        out_shape=jax.ShapeDtypeStruct((T_P, NA_WIDTH), bf16),
        compiler_params=_cp(("arbitrary",)),
    )(proj, proj, proj)


def _win_start(qr):
    return min(max(qr - WIN_ROWS // 2, 0), GRID_ROWS - WIN_ROWS)


N_ROFF = 2 * WIN_ROWS - 1
N_COFF = 2 * WIN_COLS - 1


def _bias_kernel(rpb_ref, o_ref):
    base = pl.program_id(0) * (N_ROFF * N_COFF)
    qc = lax.broadcasted_iota(jnp.int32, (GRID_W, GRID_W), 0)
    kc = lax.broadcasted_iota(jnp.int32, (GRID_W, GRID_W), 1)
    coff = kc - qc + (WIN_COLS - 1)
    cs = jnp.clip(qc - WIN_COLS // 2, 0, GRID_W - WIN_COLS)
    valid = (kc >= cs) & (kc < cs + WIN_COLS)
    neg = jnp.full((GRID_W, GRID_W), NEG_INF, f32)
    blocks = []
    for a in range(N_ROFF):
        t = jnp.zeros((GRID_W, GRID_W), f32)
        for b in range(N_COFF):
            t = jnp.where(coff == b, rpb_ref[base + a * N_COFF + b], t)
        blocks.append(jnp.where(valid, t, neg))
    for c in range(N_CHUNKS):
        for ql in range(CHUNK_ROWS):
            qr = c * CHUNK_ROWS + ql
            st = _win_start(qr)
            for kl in range(KWIN_ROWS):
                kr = KB0[c] + kl
                blk = blocks[kr - qr + WIN_ROWS - 1] if st <= kr < st + WIN_ROWS else neg
                o_ref[c, ql * GRID_W:(ql + 1) * GRID_W, kl * GRID_W:(kl + 1) * GRID_W] = blk


MOD_TN = 6 * D // 4
PREP_STEPS = DEPTH * (6 * D // MOD_TN)
assert PREP_STEPS == N_EVEN * NA_HEADS


def _prep_kernel(cv_ref, w_ref, b_ref, rpb_ref, mod_ref, bias_ref):
    _mod_kernel(cv_ref, w_ref, b_ref, mod_ref)
    _bias_kernel(rpb_ref, bias_ref)


def _prepare(cvecs, ada_w, ada_b, rpb):
    per_layer = 6 * D // MOD_TN
    return pl.pallas_call(
        _prep_kernel,
        grid=(PREP_STEPS,),
        in_specs=[pl.BlockSpec((MOD_ROWS, D), lambda s: (0, 0)),
                  pl.BlockSpec((1, D, MOD_TN), lambda s: (s // per_layer, 0, s % per_layer)),
                  pl.BlockSpec((1, 1, MOD_TN), lambda s: (s // per_layer, 0, s % per_layer)),
                  pl.BlockSpec(memory_space=pltpu.SMEM)],
        out_specs=[pl.BlockSpec((1, MOD_ROWS, MOD_TN), lambda s: (s // per_layer, 0, s % per_layer)),
                   pl.BlockSpec((None, None, N_CHUNKS, CHUNK_Q, KWIN),
                                lambda s: (s // NA_HEADS, s % NA_HEADS, 0, 0, 0))],
        out_shape=[jax.ShapeDtypeStruct((DEPTH, MOD_ROWS, 6 * D), f32),
                   jax.ShapeDtypeStruct((N_EVEN, NA_HEADS, N_CHUNKS, CHUNK_Q, KWIN), f32)],
        compiler_params=_cp(("arbitrary",)),
    )(cvecs, ada_w, ada_b.reshape(DEPTH, 1, 6 * D), rpb.reshape(-1))


def _nbr_attn_kernel(q_ref, k_ref, v_ref, kc_ref, vc_ref, bias_ref, o_ref):
    ctx = KeyValues(kc_ref[...].reshape(LANES, PAST_LEN).astype(bf16),
                    vc_ref[...].reshape(LANES, PAST_LEN).astype(bf16), feature_major=True)
    for c0 in range(0, N_CHUNKS, CHUNKS_PER_GROUP):
        problems = []
        for c in range(c0, c0 + CHUNKS_PER_GROUP):
            rows = slice(c * CHUNK_Q, (c + 1) * CHUNK_Q)
            n_keys = KW_ROWS[c] * GRID_W
            win = slice(KB0[c] * GRID_W, KB0[c] * GRID_W + n_keys)
            bias = [bias_ref[hh, c, :, :n_keys] for hh in range(HEADS_PER_STEP)]
            problems.append((q_ref[rows, :], [KeyValues(k_ref[win, :], v_ref[win, :], bias), ctx]))
        for c, o in zip(range(c0, c0 + CHUNKS_PER_GROUP), _pair_attention(problems)):
            o_ref[c * CHUNK_Q:(c + 1) * CHUNK_Q, :] = o.astype(bf16)


def _nbr_attention(proj, cache_k, cache_v, bias, li):
    qcol = 3 * CONV_CH // LANES
    ncol = NA_WIDTH // LANES
    hp_steps = NA_HEADS // HEADS_PER_STEP

    def col_spec(which):
        return pl.BlockSpec((DEC_SEQ, LANES), lambda hp, b: (NTP + b, qcol + which * ncol + hp))

    ctx_spec = pl.BlockSpec((None, None, HEADS_PER_STEP, HEAD_DIM, PAST_LEN),
                            lambda hp, b: (b, li, hp, 0, 0))
    return pl.pallas_call(
        _nbr_attn_kernel,
        grid=(hp_steps, DEC_BATCH),
        in_specs=[col_spec(0), col_spec(1), col_spec(2), ctx_spec, ctx_spec,
                  pl.BlockSpec((None, HEADS_PER_STEP, N_CHUNKS, CHUNK_Q, KWIN),
                               lambda hp, b: (li, hp, 0, 0, 0))],
        out_specs=pl.BlockSpec((DEC_SEQ, LANES), lambda hp, b: (b, hp)),
        out_shape=jax.ShapeDtypeStruct((T_S, NA_WIDTH), bf16),
        compiler_params=_cp(("arbitrary", "arbitrary")),
    )(proj, proj, proj, cache_k, cache_v, bias)


def _mixout_kernel(*refs, n_x):
    x_refs, (a_ref, ybp_ref, ybs_ref, g1_ref, cw_ref, w_ref, o_ref) = refs[:n_x], refs[n_x:]
    t = pl.program_id(0)
    a_b = a_ref[:, 0:CONV_CH].astype(f32)
    a_c = a_ref[:, CONV_CH:2 * CONV_CH].astype(f32)
    a_x = a_ref[:, 2 * CONV_CH:3 * CONV_CH].astype(f32)
    u = a_c * a_x
    r = lax.broadcasted_iota(jnp.int32, (TM, 1), 0)
    pos = jnp.where(t < NTP, r % SEQ, r)
    last = jnp.where(t < NTP, SEQ - 1, DEC_SEQ - 1)
    u_prev = jnp.where(pos == 0, 0.0, pltpu.roll(u, 1, axis=0))
    u_next = jnp.where(pos == last, 0.0, pltpu.roll(u, TM - 1, axis=0))
    y_a = a_b * (u_prev * cw_ref[0:1, :] + u * cw_ref[1:2, :] + u_next * cw_ref[2:3, :])
    y_b = jnp.where(t < NTP, ybp_ref[...], ybs_ref[...])
    y = (_dot(y_a.astype(bf16), w_ref[0:CONV_CH, :].astype(bf16))
         + _dot(y_b, w_ref[CONV_CH:, :].astype(bf16)))
    o_ref[...] = _load_x(t, x_refs) + g1_ref[0] * y


def _mixout(proj, yb_p, yb_s, x, mod, conv_w, w_out, layer):
    li = layer // 2
    xs = x if isinstance(x, tuple) else (x,)
    return pl.pallas_call(
        functools.partial(_mixout_kernel, n_x=len(xs)),
        grid=(NT,),
        in_specs=_x_specs(x) + [
            pl.BlockSpec((TM, 3 * CONV_CH), lambda t: (t, 0)),
            pl.BlockSpec((TM, NA_WIDTH), lambda t: (jnp.minimum(t, NTP - 1), 0)),
            pl.BlockSpec((TM, NA_WIDTH), lambda t: (jnp.maximum(t - NTP, 0), 0)),
            _mod_spec(layer, 2),
            pl.BlockSpec((None, 3, CONV_CH), lambda t: (li, 0, 0)),
            pl.BlockSpec((None, D, D), lambda t: (li, 0, 0))],
        out_specs=pl.BlockSpec((TM, D), lambda t: (t, 0)),
        out_shape=jax.ShapeDtypeStruct((T, D), f32),
        compiler_params=_cp(("arbitrary",)),
    )(*xs, proj, yb_p, yb_s, mod, conv_w, w_out)


FFN_TILES = 2


def _ffn_kernel(x_ref, g_ref, *refs):
    mods, (w1_ref, w3_ref, w2_ref, o_ref, h_ref) = refs[:3 * FFN_TILES], refs[3 * FFN_TILES:]
    j = pl.program_id(1)
    last = pl.num_programs(1) - 1
    parts = [(slice(n * TM, (n + 1) * TM), mods[3 * n:3 * n + 3]) for n in range(FFN_TILES)]

    def hidden_step(h):
        gate = jax.nn.silu(_dot(h, w1_ref[...].astype(bf16))) * _dot(h, w3_ref[...].astype(bf16))
        return _dot(gate.astype(bf16), w2_ref[...].astype(bf16))

    @pl.when(j == 0)
    def _():
        h = jnp.concatenate([_norm_mod(x_ref[rows, :], g_ref[0], sc_ref[0], sh_ref[0]).astype(bf16)
                             for rows, (sc_ref, sh_ref, _) in parts], axis=0)
        h_ref[...] = h
        o_ref[...] = hidden_step(h)

    @pl.when((j > 0) & (j < last))
    def _():
        o_ref[...] += hidden_step(h_ref[...])

    @pl.when(j == last)
    def _():
        acc = o_ref[...] + hidden_step(h_ref[...])
        for rows, (_, _, g2_ref) in parts:
            o_ref[rows, :] = x_ref[rows, :] + g2_ref[0] * acc[rows, :]


def _dense_ffn(x, mod, norm_g, w1, w3, w2, layer):
    li = layer // 2
    tf = TF_DENSE
    rows = FFN_TILES * TM

    def mod_spec(part, n):
        return pl.BlockSpec((1, 1, D), lambda i, j: (
            (layer * MOD_ROWS + _mod_index(i * FFN_TILES + n)) * 6 + part, 0, 0))

    mod_specs = [mod_spec(part, n) for n in range(FFN_TILES) for part in (4, 3, 5)]
    return pl.pallas_call(
        _ffn_kernel,
        grid=(T // rows, FFN_DENSE // tf),
        in_specs=[pl.BlockSpec((rows, D), lambda i, j: (i, 0)),
                  pl.BlockSpec((1, 1, D), lambda i, j: (layer, 0, 0))] + mod_specs + [
                  pl.BlockSpec((None, D, tf), lambda i, j: (li, 0, j)),
                  pl.BlockSpec((None, D, tf), lambda i, j: (li, 0, j)),
                  pl.BlockSpec((None, tf, D), lambda i, j: (li, j, 0))],
        out_specs=pl.BlockSpec((rows, D), lambda i, j: (i, 0)),
        out_shape=jax.ShapeDtypeStruct((T, D), f32),
        scratch_shapes=[pltpu.VMEM((rows, D), bf16)],
        compiler_params=_cp(("arbitrary", "arbitrary")),
    )(x, norm_g, *([mod] * len(mod_specs)), w1, w3, w2)


GROUP_CH = D // FOURIER_GROUPS


def _dft_mats(n):
    k = np.arange(n, dtype=np.int64)
    ang = 2.0 * np.pi * ((k[:, None] * k[None, :]) % n).astype(np.float64) / n
    return np.cos(ang).astype(np.float32), np.sin(ang).astype(np.float32)


def _fourier_kernel(x_ref, g_ref, sc_ref, sh_ref, g1_ref, cs_ref, ss_ref, cl_ref, sl_ref, wf_ref,
                    o_ref, f_ref):
    t = pl.program_id(0)
    h = _norm_mod(x_ref[...], g_ref[0], sc_ref[0], sh_ref[0]).astype(bf16)
    cs = cs_ref[...].astype(bf16)
    ss = ss_ref[...].astype(bf16)
    ys, zs = [], []
    for g in range(FOURIER_GROUPS):
        hg = h[:, g * GROUP_CH:(g + 1) * GROUP_CH]
        ys.append(_dot(hg, cs))
        zs.append(_dot(hg, ss))
    y = jnp.concatenate(ys, axis=-1).astype(bf16)
    z = jnp.concatenate(zs, axis=-1).astype(bf16)

    @pl.when(t < NTP)
    def _():
        for s in range(TM // SEQ):
            rows = slice(s * SEQ, (s + 1) * SEQ)
            f = _dot(cs, y[rows]) - _dot(ss, z[rows])
            f_ref[rows, :] = (f * ((SEQ * GROUP_CH) ** -0.5)).astype(bf16)

    @pl.when(t >= NTP)
    def _():
        f = _dot(cl_ref[...].astype(bf16), y) - _dot(sl_ref[...].astype(bf16), z)
        f_ref[...] = (f * ((DEC_SEQ * GROUP_CH) ** -0.5)).astype(bf16)

    o_ref[...] = x_ref[...] + g1_ref[0] * _dot(f_ref[...], wf_ref[...].astype(bf16))


def _fourier(x, mod, norm_g, fourier_w, layer):
    li = layer // 2
    c_s, s_s = _dft_mats(SEQ)
    c_l, s_l = _dft_mats(DEC_SEQ)
    const = lambda shape: pl.BlockSpec(shape, lambda t: (0, 0))
    return pl.pallas_call(
        _fourier_kernel,
        grid=(NT,),
        in_specs=[pl.BlockSpec((TM, D), lambda t: (t, 0)),
                  pl.BlockSpec((1, 1, D), lambda t: (layer, 0, 0)),
                  _mod_spec(layer, 1), _mod_spec(layer, 0), _mod_spec(layer, 2),
                  const((SEQ, SEQ)), const((SEQ, SEQ)),
                  const((DEC_SEQ, DEC_SEQ)), const((DEC_SEQ, DEC_SEQ)),
                  pl.BlockSpec((None, D, D), lambda t: (li, 0, 0))],
        out_specs=pl.BlockSpec((TM, D), lambda t: (t, 0)),
        out_shape=jax.ShapeDtypeStruct((T, D), f32),
        scratch_shapes=[pltpu.VMEM((TM, D), bf16)],
        compiler_params=_cp(("arbitrary",)),
    )(x, norm_g, mod, mod, mod, jnp.asarray(c_s), jnp.asarray(s_s), jnp.asarray(c_l), jnp.asarray(s_l),
      fourier_w)


M_E0, M_E1, M_R0, M_R1, M_W0, M_W1 = range(6)
ROW_UNROLL = 8


META_ROWS = 8


def _router_kernel(x_ref, g_ref, sc_ref, sh_ref, rw_ref, rb_ref, h_ref, meta_ref, meta_t_ref, cnt_ref,
                   carry_ref):
    t = pl.program_id(0)

    @pl.when(t == 0)
    def _():
        carry_ref[...] = jnp.zeros_like(carry_ref)

    h = _norm_mod(x_ref[...], g_ref[0], sc_ref[0], sh_ref[0])
    h_ref[...] = h
    h_hi = h.astype(bf16)
    h_lo = (h - h_hi.astype(f32)).astype(bf16)
    hi_terms = _dot(h_hi, rw_ref[...])
    lo_term = _dot(h_lo, rw_ref[:, :LANES])
    logits = ((hi_terms[:, :LANES] + lo_term) + hi_terms[:, LANES:]) + rb_ref[...]
    lane = lax.broadcasted_iota(jnp.int32, (TM, LANES), 1)
    m1 = jnp.max(logits, axis=-1, keepdims=True)
    i1 = jnp.min(jnp.where(logits == m1, lane, LANES), axis=-1, keepdims=True)
    rest = jnp.where(lane == i1, -jnp.inf, logits)
    m2 = jnp.max(rest, axis=-1, keepdims=True)
    i2 = jnp.min(jnp.where(rest == m2, lane, LANES), axis=-1, keepdims=True)
    e = jnp.exp(m2 - m1)
    w0 = 1.0 / (1.0 + e)
    w1 = e / (1.0 + e)
    oh0 = (lane == i1).astype(f32)
    oh1 = (lane == i2).astype(f32)
    oh = oh0 + oh1
    row = lax.broadcasted_iota(jnp.int32, (TM, TM), 0)
    col = lax.broadcasted_iota(jnp.int32, (TM, TM), 1)
    before = jnp.where(col < row, 1.0, 0.0).astype(bf16)
    base = carry_ref[...] + _dot(before, oh.astype(bf16))
    r0 = jnp.sum(oh0 * base, axis=-1, keepdims=True)
    r1 = jnp.sum(oh1 * base, axis=-1, keepdims=True)
    carry_ref[...] += jnp.sum(oh, axis=0, keepdims=True)
    rec = jnp.zeros((TM, LANES), f32)
    for idx, val in ((M_E0, i1.astype(f32)), (M_E1, i2.astype(f32)), (M_R0, r0), (M_R1, r1),
                     (M_W0, w0), (M_W1, w1)):
        rec = jnp.where(lane == idx, val, rec)
    meta_ref[...] = rec
    meta_t_ref[...] = rec.T[:META_ROWS, :]
    cnt_ref[...] = carry_ref[...]


def _router(x, mod, norm_g, router_w, router_b, layer):
    li = layer // 2
    rw = jnp.pad(router_w[li], ((0, 0), (0, LANES - N_EXPERTS)))
    rw_hi = rw.astype(bf16)
    rw_lo = (rw - rw_hi.astype(f32)).astype(bf16)
    rw = jnp.concatenate([rw_hi, rw_lo], axis=1)
    rb = jnp.pad(router_b[li], (0, LANES - N_EXPERTS), constant_values=NEG_INF).reshape(1, LANES)
    return pl.pallas_call(
        _router_kernel,
        grid=(NT,),
        in_specs=[pl.BlockSpec((TM, D), lambda t: (t, 0)),
                  pl.BlockSpec((1, 1, D), lambda t: (layer, 0, 0)),
                  _mod_spec(layer, 4), _mod_spec(layer, 3),
                  pl.BlockSpec((D, 2 * LANES), lambda t: (0, 0)),
                  pl.BlockSpec((1, LANES), lambda t: (0, 0))],
        out_specs=[pl.BlockSpec((TM, D), lambda t: (t, 0)),
                   pl.BlockSpec((TM, LANES), lambda t: (t, 0)),
                   pl.BlockSpec((META_ROWS, TM), lambda t: (0, t)),
                   pl.BlockSpec((1, LANES), lambda t: (0, 0))],
        out_shape=[jax.ShapeDtypeStruct((T, D), f32),
                   jax.ShapeDtypeStruct((T, LANES), f32),
                   jax.ShapeDtypeStruct((META_ROWS, T), f32),
                   jax.ShapeDtypeStruct((1, LANES), f32)],
        scratch_shapes=[pltpu.VMEM((1, LANES), f32)],
        compiler_params=_cp(("arbitrary",)),
    )(x, norm_g, mod, mod, rw, rb)


def _row_copy(src_ref, src_row, dst_ref, dst_row, sem):
    return pltpu.make_async_copy(src_ref.at[pl.ds(src_row, 1)], dst_ref.at[pl.ds(dst_row, 1)], sem)


ZERO_BLOCK = 8


def _dispatch_kernel(lo_ref, hi_ref, nu_ref, dest_ref, h_ref, xg_ref, zero_ref, sem, zsem):
    @pl.when(pl.program_id(0) == 0)
    def _():
        zero_ref[...] = jnp.zeros_like(zero_ref)

        def row_zero(r):
            return _row_copy(zero_ref, 0, xg_ref, r, zsem)

        def block_zero(b):
            return pltpu.make_async_copy(zero_ref.at[pl.ds(0, ZERO_BLOCK)],
                                         xg_ref.at[pl.ds(pl.multiple_of(b * ZERO_BLOCK, ZERO_BLOCK), ZERO_BLOCK)],
                                         zsem)

        def tile_zero(tile):
            return pltpu.make_async_copy(zero_ref, xg_ref.at[pl.ds(pl.multiple_of(tile * TM_E, TM_E), TM_E)],
                                         zsem)

        def each(lo, hi, copy, wait):
            def step(i, c):
                if wait:
                    copy(i).wait()
                else:
                    copy(i).start()
                return c
            lax.fori_loop(lo, hi, step, 0)

        for wait in (False, True):
            for e in range(N_EXPERTS):
                lo, hi = lo_ref[e], hi_ref[e]
                aligned = jnp.minimum((lo + ZERO_BLOCK - 1) // ZERO_BLOCK * ZERO_BLOCK, hi)
                each(lo, aligned, row_zero, wait)
                each(aligned // ZERO_BLOCK, hi // ZERO_BLOCK, block_zero, wait)
            each(nu_ref[0], NT_E, tile_zero, wait)

    def issue(g, c):
        for u in range(ROW_UNROLL):
            r = g * ROW_UNROLL + u
            for k in range(2):
                _row_copy(h_ref, r, xg_ref, dest_ref[0, 0, k * TM + r], sem).start(priority=k)
        return c

    lax.fori_loop(0, TM // ROW_UNROLL, issue, 0)
    for k in range(2):
        pltpu.make_async_copy(h_ref, xg_ref.at[pl.ds(0, TM)], sem).wait()


def _dispatch(pad_lo, pad_hi, n_used, dest, h):
    grid_spec = pltpu.PrefetchScalarGridSpec(
        num_scalar_prefetch=3,
        grid=(NT,),
        in_specs=[pl.BlockSpec((1, 1, 2 * TM), lambda t, *_: (t, 0, 0), memory_space=pltpu.SMEM),
                  pl.BlockSpec((TM, D), lambda t, *_: (t, 0))],
        out_specs=pl.BlockSpec(memory_space=pl.ANY),
        scratch_shapes=[pltpu.VMEM((TM_E, D), f32), pltpu.SemaphoreType.DMA(()), pltpu.SemaphoreType.DMA(())],
    )
    return pl.pallas_call(
        _dispatch_kernel,
        grid_spec=grid_spec,
        out_shape=jax.ShapeDtypeStruct((R_E, D), f32),
        compiler_params=_cp(("arbitrary",)),
    )(pad_lo, pad_hi, n_used, dest, h)


NJ_EXP = FFN_EXPERT // TF_EXP
ROW_CLASSES = tuple(TM_E * q // 4 for q in (1, 2, 3, 4))


def _expert_kernel(te_ref, nu_ref, rows_ref, xg_ref, w1_ref, w3_ref, w2_ref, y_ref, h_ref):
    t = pl.program_id(0)
    j = pl.program_id(1)
    n_rows = rows_ref[t]

    def hidden_step(h):
        gate = jax.nn.silu(_dot(h, w1_ref[...].astype(bf16))) * _dot(h, w3_ref[...].astype(bf16))
        return _dot(gate.astype(bf16), w2_ref[...].astype(bf16))

    @pl.when((j == 0) & (n_rows == 0))
    def _():
        y_ref[...] = jnp.zeros_like(y_ref)

    for below, m in zip((0,) + ROW_CLASSES, ROW_CLASSES):
        in_class = (n_rows > below) & (n_rows <= m)

        @pl.when(in_class & (j == 0))
        def _():
            h = xg_ref[:m, :].astype(bf16)
            h_ref[:m, :] = h
            y_ref[:m, :] = hidden_step(h)
            if m < TM_E:
                y_ref[m:, :] = jnp.zeros((TM_E - m, D), f32)

        @pl.when(in_class & (j > 0))
        def _():
            y_ref[:m, :] += hidden_step(h_ref[:m, :])


def _experts(tile_expert, n_used, tile_rows, xg, w1, w3, w2, layer):
    li = layer // 2
    tf = TF_EXP
    nj = NJ_EXP

    def jj(t, j, nu):
        return jnp.where(t < nu[0], j, nj - 1)

    def tt(t, nu):
        return jnp.minimum(t, jnp.maximum(nu[0] - 1, 0))

    grid_spec = pltpu.PrefetchScalarGridSpec(
        num_scalar_prefetch=3,
        grid=(NT_E, nj),
        in_specs=[pl.BlockSpec((TM_E, D), lambda t, j, te, nu, nr: (tt(t, nu), 0)),
                  pl.BlockSpec((None, None, D, tf), lambda t, j, te, nu, nr: (li, te[t], 0, jj(t, j, nu))),
                  pl.BlockSpec((None, None, D, tf), lambda t, j, te, nu, nr: (li, te[t], 0, jj(t, j, nu))),
                  pl.BlockSpec((None, None, tf, D), lambda t, j, te, nu, nr: (li, te[t], jj(t, j, nu), 0))],
        out_specs=pl.BlockSpec((TM_E, D), lambda t, j, te, nu, nr: (t, 0)),
        scratch_shapes=[pltpu.VMEM((TM_E, D), bf16)],
    )
    return pl.pallas_call(
        _expert_kernel,
        grid_spec=grid_spec,
        out_shape=jax.ShapeDtypeStruct((R_E, D), f32),
        compiler_params=_cp(("arbitrary", "arbitrary")),
    )(tile_expert, n_used, tile_rows, xg, w1, w3, w2)


def _combine_kernel(dest_ref, dest_next_ref, x_ref, g2_ref, meta_ref, y_ref, *rest, final):
    if final:
        fg_ref, op_ref, os_ref, buf_ref, sems = rest
    else:
        o_ref, buf_ref, sems = rest
    t = pl.program_id(0)
    slot = t % 2

    def gather(rows_ref, s):
        def issue(g, c):
            for u in range(ROW_UNROLL):
                r = g * ROW_UNROLL + u
                for k in range(2):
                    _row_copy(y_ref, rows_ref[0, 0, k * TM + r], buf_ref.at[s, k], r, sems.at[s]).start(
                        priority=k)
            return c
        lax.fori_loop(0, TM // ROW_UNROLL, issue, 0)

    @pl.when(t == 0)
    def _():
        gather(dest_ref, 0)

    @pl.when(t + 1 < NT)
    def _():
        gather(dest_next_ref, 1 - slot)

    for k in range(2):
        pltpu.make_async_copy(y_ref.at[pl.ds(0, TM)], buf_ref.at[slot, k], sems.at[slot]).wait()
    w0 = meta_ref[:, M_W0:M_W0 + 1]
    w1 = meta_ref[:, M_W1:M_W1 + 1]
    out = x_ref[...] + g2_ref[0] * (w0 * buf_ref[slot, 0] + w1 * buf_ref[slot, 1])
    if not final:
        o_ref[...] = out
        return
    normed = (out * lax.rsqrt(jnp.mean(out * out, axis=-1, keepdims=True) + RMS_EPS)) * fg_ref[...]

    @pl.when(t < NTP)
    def _():
        op_ref[...] = normed

    @pl.when(t >= NTP)
    def _():
        os_ref[...] = normed


def _combine(dest, x, mod, meta, y, layer, final_g=None):
    final = final_g is not None
    in_specs = [pl.BlockSpec((1, 1, 2 * TM), lambda t: (t, 0, 0), memory_space=pltpu.SMEM),
                pl.BlockSpec((1, 1, 2 * TM), lambda t: (jnp.minimum(t + 1, NT - 1), 0, 0),
                             memory_space=pltpu.SMEM),
                pl.BlockSpec((TM, D), lambda t: (t, 0)),
                _mod_spec(layer, 5),
                pl.BlockSpec((TM, LANES), lambda t: (t, 0)),
                pl.BlockSpec(memory_space=pl.ANY)]
    args = [dest, dest, x, mod, meta, y]
    if final:
        in_specs.append(pl.BlockSpec((1, D), lambda t: (0, 0)))
        args.append(final_g.reshape(1, D))
        out_specs = [pl.BlockSpec((TM, D), lambda t: (jnp.minimum(t, NTP - 1), 0)),
                     pl.BlockSpec((TM, D), lambda t: (jnp.maximum(t - NTP, 0), 0))]
        out_shape = [jax.ShapeDtypeStruct((T_P, D), f32), jax.ShapeDtypeStruct((T_S, D), f32)]
    else:
        out_specs = pl.BlockSpec((TM, D), lambda t: (t, 0))
        out_shape = jax.ShapeDtypeStruct((T, D), f32)
    return pl.pallas_call(
        functools.partial(_combine_kernel, final=final),
        grid=(NT,),
        in_specs=in_specs,
        out_specs=out_specs,
        out_shape=out_shape,
        scratch_shapes=[pltpu.VMEM((2, 2, TM, D), f32), pltpu.SemaphoreType.DMA((2,))],
        compiler_params=_cp(("arbitrary",)),
    )(*args)


def _moe(x, mod, norm_g, router_w, router_b, w1, w3, w2, layer, final_g=None):
    h_rows, meta, meta_t, counts = _router(x, mod, norm_g, router_w, router_b, layer)
    cnt = counts[0, :N_EXPERTS].astype(jnp.int32)
    padded = ((cnt + TM_E - 1) // TM_E) * TM_E
    ends = jnp.cumsum(padded)
    starts = ends - padded
    experts = meta_t[M_E0:M_E1 + 1].astype(jnp.int32)
    ranks = meta_t[M_R0:M_R1 + 1].astype(jnp.int32)
    start_of = functools.reduce(lambda acc, e: jnp.where(experts == e, starts[e], acc), range(N_EXPERTS), 0)
    dest = start_of + ranks
    dest = dest.reshape(2, NT, TM).transpose(1, 0, 2).reshape(NT, 1, 2 * TM)
    n_used = (ends[-1] // TM_E).astype(jnp.int32).reshape(1)
    tile_start = jnp.minimum(jnp.arange(NT_E, dtype=jnp.int32), n_used[0] - 1) * TM_E
    tile_expert = jnp.sum((tile_start[:, None] >= ends[None, :]).astype(jnp.int32), axis=1)
    tile_expert = jnp.minimum(tile_expert, N_EXPERTS - 1).astype(jnp.int32)
    tile_ids = jnp.arange(NT_E, dtype=jnp.int32)
    tile_rows = jnp.clip((starts + cnt)[tile_expert] - tile_ids * TM_E, 0, TM_E)
    tile_rows = jnp.where(tile_ids < n_used[0], tile_rows, 0).astype(jnp.int32)
    xg = _dispatch((starts + cnt).astype(jnp.int32), ends.astype(jnp.int32), n_used, dest, h_rows)
    y = _experts(tile_expert, n_used, tile_rows, xg, w1, w3, w2, layer)
    return _combine(dest, x, mod, meta, y, layer, final_g)


def kernel(x_prompt, x_sample, c, cache_k, cache_v, c_ctx, ada_w, ada_b, norm1_g, norm2_g, w_in, conv_w, rpb,
           w_out, ffn_w1, ffn_w3, ffn_w2, fourier_w, router_w, router_b, moe_w1, moe_w3, moe_w2, final_g):
    cvecs = jnp.concatenate([c_ctx[None, :], c, jnp.zeros((MOD_ROWS - 1 - DEC_BATCH, D), f32)], axis=0)
    mod, bias = _prepare(cvecs, ada_w, ada_b, rpb)
    mod = mod.reshape(DEPTH * MOD_ROWS * 6, 1, D)
    n1 = norm1_g.reshape(DEPTH, 1, D)
    n2 = norm2_g.reshape(DEPTH, 1, D)
    cache_kt = jnp.swapaxes(cache_k, -1, -2)
    cache_vt = jnp.swapaxes(cache_v, -1, -2)
    x = (x_prompt.reshape(T_P, D), x_sample.reshape(T_S, D))
    caches = [jnp.zeros((BATCH, N_EVEN, NA_HEADS, HEAD_DIM, SEQ), f32) for _ in range(2)]
    for layer in range(DEPTH):
        li = layer // 2
        if layer % 2 == 0:
            proj, *caches = _inproj(x, mod, n1, w_in, layer, caches)
            yb_p = _ctx_attention(proj)
            yb_s = _nbr_attention(proj, cache_kt, cache_vt, bias, li)
            x = _mixout(proj, yb_p, yb_s, x, mod, conv_w, w_out, layer)
            x = _dense_ffn(x, mod, n2, ffn_w1, ffn_w3, ffn_w2, layer)
        else:
            x = _fourier(x, mod, n1, fourier_w, layer)
            last = layer == DEPTH - 1
            x = _moe(x, mod, n2, router_w, router_b, moe_w1, moe_w3, moe_w2, layer, final_g if last else None)
    y_prompt, y_sample = x
    new_kt, new_vt = caches
    return (y_prompt.reshape(BATCH, SEQ, D), y_sample.reshape(DEC_BATCH, DEC_SEQ, D),
            jnp.swapaxes(new_kt, -1, -2), jnp.swapaxes(new_vt, -1, -2))
```
